```python
import math
import jax, jax.numpy as jnp
from jax import lax
import numpy as np

D_MODEL = 1024
BATCH = 16
SEQ = 2048
DEPTH = 2

EPS = 1e-6
Q_BLOCK = 128
RET_HEADS = 4
RET_DK = 128
RET_DV = 256
RET_CHUNK = 128
DIFF_HEADS = 8
DIFF_DH = 64
MLA_HEADS = 16
MLA_Q_LORA = 256
MLA_KV_LORA = 128
MLA_NOPE = 64
MLA_ROPE = 32
MLA_V = 64
ROPE_THETA = 10000.0
REL_BUCKETS = 32
REL_MAX_DIST = 128
D_FF = 4 * D_MODEL
N_BRANCH = 3

RET_QK_W = RET_HEADS * RET_DK
RET_V_W = RET_HEADS * RET_DV
DIFF_QK_W = 2 * DIFF_HEADS * DIFF_DH
DIFF_V_W = DIFF_HEADS * 2 * DIFF_DH
MLA_QH = MLA_NOPE + MLA_ROPE
MLA_KVH = MLA_NOPE + MLA_V
MLA_V_W = MLA_HEADS * MLA_V
IN_SPLITS = (RET_QK_W, RET_QK_W, RET_V_W, RET_V_W,
             DIFF_QK_W, DIFF_QK_W, DIFF_V_W,
             MLA_Q_LORA, MLA_KV_LORA, MLA_ROPE,
             N_BRANCH * D_MODEL)
D_IN = sum(IN_SPLITS)

kernel_name = 'hybrid_gated_retention_diffattn_mla_block'


def rms_norm(x, g=None):
    xf = x.astype(jnp.float32)
    y = xf * lax.rsqrt(jnp.mean(xf * xf, axis=-1, keepdims=True) + EPS)
    if g is not None:
        y = y * g.astype(jnp.float32)
    return y.astype(x.dtype)


def rope_tables(inv_freq):
    pos = jnp.arange(SEQ, dtype=jnp.float32)
    ang = pos[:, None] * inv_freq[None, :]
    return jnp.cos(ang), jnp.sin(ang)


def apply_rope(x, cos, sin):
    d2 = x.shape[-1] // 2
    x1, x2 = x[..., :d2], x[..., d2:]
    c = cos[:, None, :].astype(x.dtype)
    s = sin[:, None, :].astype(x.dtype)
    return jnp.concatenate([x1 * c - x2 * s, x2 * c + x1 * s], axis=-1)


def t5_bucket(n):
    max_exact = REL_BUCKETS // 2
    nf = jnp.maximum(n, 1).astype(jnp.float32)
    large = max_exact + (jnp.log(nf / max_exact) / math.log(REL_MAX_DIST / max_exact)
                         * (REL_BUCKETS - max_exact)).astype(jnp.int32)
    large = jnp.minimum(large, REL_BUCKETS - 1)
    return jnp.where(n < max_exact, n, large)


def retention(q, k, v):
    b = q.shape[0]
    n_chunks = SEQ // RET_CHUNK

    def chunks(t):
        return t.reshape(b, n_chunks, RET_CHUNK, RET_HEADS, t.shape[-1]).transpose(1, 0, 3, 2, 4)

    log_g = jnp.log1p(-jnp.exp2(-5.0 - jnp.arange(RET_HEADS, dtype=jnp.float32)))
    idx = jnp.arange(RET_CHUNK, dtype=jnp.float32)
    rel = idx[:, None] - idx[None, :]
    dmask = jnp.where(rel >= 0, jnp.exp(jnp.maximum(rel, 0.0)[None] * log_g[:, None, None]), 0.0)
    xi = jnp.exp((idx + 1.0)[None, :] * log_g[:, None])[:, :, None]
    zeta = jnp.exp((RET_CHUNK - 1.0 - idx)[None, :] * log_g[:, None])[:, :, None]
    g_chunk = jnp.exp(RET_CHUNK * log_g)[:, None, None]

    def step(state, qkv):
        qc, kc, vc = qkv
        inner = jnp.einsum('bhid,bhjd->bhij', qc, kc) * dmask
        out = (jnp.einsum('bhij,bhje->bhie', inner, vc)
               + jnp.einsum('bhid,bhde->bhie', qc * xi, state))
        state = g_chunk * state + jnp.einsum('bhjd,bhje->bhde', kc * zeta, vc)
        return state, out

    state0 = jnp.zeros((b, RET_HEADS, RET_DK, RET_DV), jnp.float32)
    _, o = lax.scan(step, state0, (chunks(q), chunks(k), chunks(v)))
    return o.transpose(1, 0, 3, 2, 4).reshape(b, SEQ, RET_HEADS, RET_DV)


def diff_attention(q, k, v, lam, rel_bias):
    scale = DIFF_DH ** -0.5
    kpos = jnp.arange(SEQ)

    def block(i):
        q0 = i * Q_BLOCK
        qb = lax.dynamic_slice_in_dim(q, q0, Q_BLOCK, axis=1)
        dist = (q0 + jnp.arange(Q_BLOCK))[:, None] - kpos[None, :]
        bias = rel_bias[t5_bucket(jnp.maximum(dist, 0))].astype(jnp.float32)
        logits = (jnp.einsum('bqhcd,bkhcd->bhcqk', qb, k).astype(jnp.float32) * scale
                  + jnp.transpose(bias, (2, 0, 1))[None, :, None])
        logits = jnp.where(dist >= 0, logits, -jnp.inf)
        p = jax.nn.softmax(logits, axis=-1)
        a = p[:, :, 0] - lam * p[:, :, 1]
        return jnp.einsum('bhqk,bkhe->bqhe', a.astype(v.dtype), v)

    out = lax.map(block, jnp.arange(SEQ // Q_BLOCK))
    return jnp.swapaxes(out, 0, 1).reshape(v.shape)


def mla_attention(q_nope, q_pe, k_nope, k_pe, v):
    scale = MLA_QH ** -0.5
    kpos = jnp.arange(SEQ)

    def block(i):
        q0 = i * Q_BLOCK
        qn = lax.dynamic_slice_in_dim(q_nope, q0, Q_BLOCK, axis=1)
        qp = lax.dynamic_slice_in_dim(q_pe, q0, Q_BLOCK, axis=1)
        dist = (q0 + jnp.arange(Q_BLOCK))[:, None] - kpos[None, :]
        logits = (jnp.einsum('bqhd,bkhd->bhqk', qn, k_nope)
                  + jnp.einsum('bqhd,bkd->bhqk', qp, k_pe)).astype(jnp.float32) * scale
        logits = jnp.where(dist >= 0, logits, -jnp.inf)
        p = jax.nn.softmax(logits, axis=-1)
        return jnp.einsum('bhqk,bkhe->bqhe', p.astype(v.dtype), v)

    out = lax.map(block, jnp.arange(SEQ // Q_BLOCK))
    return jnp.swapaxes(out, 0, 1).reshape(v.shape)


def setup_inputs(seed: int = 0) -> dict:
    key = jax.random.key(seed)
    ks = jax.random.split(key, 32)
    f32 = jnp.float32

    def w(k, shape, fan_in):
        return jax.random.normal(k, shape, f32) * fan_in ** -0.5

    def gain(k, shape):
        return 1.0 + 0.05 * jax.random.normal(k, shape, f32)

    L = DEPTH
    return {
        'x': jax.random.normal(ks[0], (BATCH, SEQ, D_MODEL), f32),
        'rel_bias': 0.5 * jax.random.normal(ks[1], (REL_BUCKETS, DIFF_HEADS), f32),
        'pre_mix_g': gain(ks[2], (L, D_MODEL)),
        'w_in': w(ks[3], (L, D_MODEL, D_IN), D_MODEL),
        'w_ret_o': w(ks[4], (L, RET_V_W, D_MODEL), RET_V_W),
        'lambda_q1': 0.1 * jax.random.normal(ks[5], (L, DIFF_DH), f32),
        'lambda_k1': 0.1 * jax.random.normal(ks[6], (L, DIFF_DH), f32),
        'lambda_q2': 0.1 * jax.random.normal(ks[7], (L, DIFF_DH), f32),
        'lambda_k2': 0.1 * jax.random.normal(ks[8], (L, DIFF_DH), f32),
        'diff_subln_g': gain(ks[9], (L, 2 * DIFF_DH)),
        'w_diff_o': w(ks[10], (L, DIFF_V_W, D_MODEL), DIFF_V_W),
        'mla_q_norm_g': gain(ks[11], (L, MLA_Q_LORA)),
        'w_mla_uq': w(ks[12], (L, MLA_Q_LORA, MLA_HEADS * MLA_QH), MLA_Q_LORA),
        'mla_kv_norm_g': gain(ks[13], (L, MLA_KV_LORA)),
        'w_mla_ukv': w(ks[14], (L, MLA_KV_LORA, MLA_HEADS * MLA_KVH), MLA_KV_LORA),
        'w_mla_o': w(ks[15], (L, MLA_V_W, D_MODEL), MLA_V_W),
        'w_out': w(ks[16], (L, D_MODEL, D_MODEL), D_MODEL),
        'post_mix_g': gain(ks[17], (L, D_MODEL)),
        'pre_mlp_g': gain(ks[18], (L, D_MODEL)),
        'w_up': w(ks[19], (L, D_MODEL, D_FF), D_MODEL),
        'w_down': w(ks[20], (L, D_FF, D_MODEL), D_FF),
        'post_mlp_g': gain(ks[21], (L, D_MODEL)),
    }


def reference(x, rel_bias, pre_mix_g, w_in, w_ret_o, lambda_q1, lambda_k1, lambda_q2, lambda_k2,
              diff_subln_g, w_diff_o, mla_q_norm_g, w_mla_uq, mla_kv_norm_g, w_mla_ukv, w_mla_o,
              w_out, post_mix_g, pre_mlp_g, w_up, w_down, post_mlp_g):
    b = x.shape[0]
    offs = np.cumsum(np.array(IN_SPLITS))[:-1].tolist()
    ret_cos, ret_sin = rope_tables(1.0 / (10000.0 ** jnp.linspace(0.0, 1.0, RET_DK // 2)))
    mla_cos, mla_sin = rope_tables(1.0 / (ROPE_THETA ** (jnp.arange(0, MLA_ROPE, 2, dtype=jnp.float32) / MLA_ROPE)))

    for l in range(DEPTH):
        h = rms_norm(x, pre_mix_g[l])
        (rq, rk, rv, rg, dq, dk, dv, cq, ckv, kpe, gates) = jnp.split(h @ w_in[l], offs, axis=-1)

        rq = apply_rope(rq.reshape(b, SEQ, RET_HEADS, RET_DK), ret_cos, ret_sin)
        rk = apply_rope(rk.reshape(b, SEQ, RET_HEADS, RET_DK), ret_cos, ret_sin) * (RET_DK ** -0.5)
        ret = retention(rq, rk, rv.reshape(b, SEQ, RET_HEADS, RET_DV))
        ret = rms_norm(ret).reshape(b, SEQ, RET_V_W).astype(h.dtype)
        branch_a = (jax.nn.silu(rg) * ret) @ w_ret_o[l]

        lambda_init = 0.8 - 0.6 * math.exp(-0.3 * l)
        lam = (jnp.exp(jnp.sum(lambda_q1[l] * lambda_k1[l]).astype(jnp.float32))
               - jnp.exp(jnp.sum(lambda_q2[l] * lambda_k2[l]).astype(jnp.float32)) + lambda_init)
        da = diff_attention(dq.reshape(b, SEQ, DIFF_HEADS, 2, DIFF_DH),
                            dk.reshape(b, SEQ, DIFF_HEADS, 2, DIFF_DH),
                            dv.reshape(b, SEQ, DIFF_HEADS, 2 * DIFF_DH), lam, rel_bias)
        da = rms_norm(da, diff_subln_g[l]) * (1.0 - lambda_init)
        branch_b = da.reshape(b, SEQ, DIFF_V_W).astype(h.dtype) @ w_diff_o[l]

        qh = (rms_norm(cq, mla_q_norm_g[l]) @ w_mla_uq[l]).reshape(b, SEQ, MLA_HEADS, MLA_QH)
        q_nope, q_pe = qh[..., :MLA_NOPE], apply_rope(qh[..., MLA_NOPE:], mla_cos, mla_sin)
        kvh = (rms_norm(ckv, mla_kv_norm_g[l]) @ w_mla_ukv[l]).reshape(b, SEQ, MLA_HEADS, MLA_KVH)
        k_nope, v_mla = kvh[..., :MLA_NOPE], kvh[..., MLA_NOPE:]
        k_pe = apply_rope(kpe[:, :, None, :], mla_cos, mla_sin)[:, :, 0, :]
        mo = mla_attention(q_nope, q_pe, k_nope, k_pe, v_mla)
        branch_c = mo.reshape(b, SEQ, MLA_V_W) @ w_mla_o[l]

        g = jax.nn.sigmoid(gates).reshape(b, SEQ, N_BRANCH, D_MODEL)
        merged = g[:, :, 0] * branch_a + g[:, :, 1] * branch_b + g[:, :, 2] * branch_c
        x = x + rms_norm(merged @ w_out[l], post_mix_g[l])

        h = rms_norm(x, pre_mlp_g[l])
        m = jnp.square(jax.nn.relu(h @ w_up[l])) @ w_down[l]
        x = x + rms_norm(m, post_mlp_g[l])
    return x
```

```python
import functools
import math

import jax
import jax.numpy as jnp
import numpy as np
from jax import lax
from jax.experimental import pallas as pl
from jax.experimental.pallas import tpu as pltpu

F32 = jnp.float32
BF16 = jnp.bfloat16

D_MODEL = 1024
EPS = 1e-6
RET_HEADS = 4
RET_DK = 128
RET_DV = 256
RET_CHUNK = 128
DIFF_HEADS = 8
DIFF_DH = 64
MLA_HEADS = 16
MLA_Q_LORA = 256
MLA_KV_LORA = 128
MLA_NOPE = 64
MLA_ROPE = 32
MLA_V = 64
ROPE_THETA = 10000.0
REL_BUCKETS = 32
REL_MAX_DIST = 128
D_FF = 4 * D_MODEL
N_BRANCH = 3

RET_QK_W = RET_HEADS * RET_DK
RET_V_W = RET_HEADS * RET_DV
DIFF_W = DIFF_HEADS * 2 * DIFF_DH
MLA_QH = MLA_NOPE + MLA_ROPE
MLA_KVH = MLA_NOPE + MLA_V
MLA_V_W = MLA_HEADS * MLA_V
MLA_PAD_W = MLA_HEADS * 128
IN_SPLITS = (RET_QK_W, RET_QK_W, RET_V_W, RET_V_W, DIFF_W, DIFF_W, DIFF_W,
             MLA_Q_LORA, MLA_KV_LORA, MLA_ROPE, N_BRANCH * D_MODEL)

LANES = 128
LOG2E = math.log2(math.e)
NEG_BIG = -1e30
VMEM_LIMIT = 56 * 1024 * 1024

COL_GATES = 0
COL_RV = COL_GATES + N_BRANCH * D_MODEL
COL_RG = COL_RV + RET_V_W
COL_DQ = COL_RG + RET_V_W
COL_DK = COL_DQ + DIFF_W
COL_DV = COL_DK + DIFF_W
COL_RQ = COL_DV + DIFF_W
COL_RK = COL_RQ + RET_QK_W
COL_CQ = COL_RK + RET_QK_W
COL_CKV = COL_CQ + MLA_Q_LORA
COL_KPE = COL_CKV + MLA_KV_LORA
N_IN = COL_KPE + LANES


def _cparams(*sem):
    return pltpu.CompilerParams(dimension_semantics=sem, vmem_limit_bytes=VMEM_LIMIT)


def _rms(xf):
    return xf * lax.rsqrt(jnp.mean(xf * xf, axis=-1, keepdims=True) + EPS)


def _dot(a, b):
    return jnp.dot(a, b, preferred_element_type=F32)


def _dot_nt(a, b):
    return lax.dot_general(a, b, (((1,), (1,)), ((), ())), preferred_element_type=F32)


def _in_proj_kernel(x_ref, g_ref, w_ref, cs_ref, o_ref, h_ref):
    @pl.when(pl.program_id(1) == 0)
    def _():
        h_ref[...] = (_rms(x_ref[...]) * g_ref[...]).astype(BF16)

    o_ref[...] = (_dot(h_ref[...], w_ref[...]) * cs_ref[...]).astype(o_ref.dtype)


def _in_proj(x2, g, w, colscale, *, tm, tn):
    t, d = x2.shape
    n = w.shape[1]
    return pl.pallas_call(
        _in_proj_kernel,
        grid=(t // tm, n // tn),
        in_specs=[pl.BlockSpec((tm, d), lambda i, j: (i, 0)),
                  pl.BlockSpec((1, d), lambda i, j: (0, 0)),
                  pl.BlockSpec((d, tn), lambda i, j: (0, j)),
                  pl.BlockSpec((1, tn), lambda i, j: (0, j))],
        out_specs=pl.BlockSpec((tm, tn), lambda i, j: (i, j)),
        out_shape=jax.ShapeDtypeStruct((t, n), BF16),
        scratch_shapes=[pltpu.VMEM((tm, d), BF16)],
        compiler_params=_cparams("parallel", "arbitrary"),
        name="in_proj",
    )(x2, g, w, colscale)


def _ret_kernel(q_ref, k_ref, v_ref, rg_ref, cos_ref, sin_ref, dm_ref, xi_ref, zt_ref, gc_ref,
                o_ref, st_ref):
    @pl.when(pl.program_id(1) == 0)
    def _():
        st_ref[...] = jnp.zeros_like(st_ref)

    cos = cos_ref[...]
    sin = sin_ref[...]
    for h in range(RET_HEADS):
        qs = slice(h * RET_DK, (h + 1) * RET_DK)
        vs = slice(h * RET_DV, (h + 1) * RET_DV)
        q = q_ref[:, qs].astype(F32)
        k = k_ref[:, qs].astype(F32)
        qr = q * cos + pltpu.roll(q, RET_DK // 2, 1) * sin
        kr = k * cos + pltpu.roll(k, RET_DK // 2, 1) * sin
        v = v_ref[:, vs]
        st = st_ref[h]
        inner = _dot_nt(qr.astype(BF16), kr.astype(BF16)) * dm_ref[h]
        out = _dot(inner.astype(BF16), v) + _dot((qr * xi_ref[h]).astype(BF16), st.astype(BF16))
        kz_t = (kr * zt_ref[h]).T.astype(BF16)
        st_ref[h] = gc_ref[h] * st + _dot(kz_t, v)
        rg = rg_ref[:, vs].astype(F32)
        o_ref[:, vs] = (_rms(out) * (rg * jax.nn.sigmoid(rg))).astype(o_ref.dtype)


def _retention(proj, tabs, *, batch, seq):
    t = proj.shape[0]
    c = RET_CHUNK
    nc = seq // c
    cos, sin, dmask, xi, zeta, gch = tabs
    row = lambda b, n: b * nc + n
    whole3 = lambda b, n: (0, 0, 0)
    return pl.pallas_call(
        _ret_kernel,
        grid=(batch, nc),
        in_specs=[pl.BlockSpec((c, RET_QK_W), lambda b, n: (row(b, n), COL_RQ // RET_QK_W)),
                  pl.BlockSpec((c, RET_QK_W), lambda b, n: (row(b, n), COL_RK // RET_QK_W)),
                  pl.BlockSpec((c, RET_V_W), lambda b, n: (row(b, n), COL_RV // RET_V_W)),
                  pl.BlockSpec((c, RET_V_W), lambda b, n: (row(b, n), COL_RG // RET_V_W)),
                  pl.BlockSpec((c, RET_DK), lambda b, n: (n, 0)),
                  pl.BlockSpec((c, RET_DK), lambda b, n: (n, 0)),
                  pl.BlockSpec((RET_HEADS, c, c), whole3),
                  pl.BlockSpec((RET_HEADS, c, RET_DK), whole3),
                  pl.BlockSpec((RET_HEADS, c, RET_DK), whole3),
                  pl.BlockSpec((RET_HEADS, 1, RET_DV), whole3)],
        out_specs=pl.BlockSpec((c, RET_V_W), lambda b, n: (row(b, n), 0)),
        out_shape=jax.ShapeDtypeStruct((t, RET_V_W), BF16),
        scratch_shapes=[pltpu.VMEM((RET_HEADS, RET_DK, RET_DV), F32)],
        compiler_params=_cparams("parallel", "arbitrary"),
        name="retention",
    )(proj, proj, proj, proj, cos, sin, dmask, xi, zeta, gch)


def _softmax_step(s, vs, m, l, acc):
    m_new = jnp.maximum(m, jnp.max(s, axis=-1, keepdims=True))
    alpha = jnp.exp2(m - m_new)
    p = jnp.exp2(s - m_new)
    l = alpha * l + jnp.sum(p, axis=-1, keepdims=True)
    acc = alpha * acc + _dot(p.astype(BF16), vs)
    return m_new, l, acc


def _diff_kernel(lam_ref, q_ref, k_ref, v_ref, tab_ref, g_ref, o_ref, *, tq, post_scale):
    qi = pl.program_id(2)
    q = q_ref[...].astype(F32)
    lane = lax.broadcasted_iota(jnp.int32, q.shape, 1)
    q2 = jnp.concatenate([jnp.where(lane < DIFF_DH, q, 0.0),
                          jnp.where(lane >= DIFF_DH, q, 0.0)], axis=0).astype(BF16)

    def body(j, carry):
        rows = pl.ds(pl.multiple_of(j * tq, tq), tq)
        bias = tab_ref[0, jnp.minimum(qi - j, 2)]
        s = _dot_nt(q2, k_ref[rows, :]) + jnp.concatenate([bias, bias], axis=0)
        return _softmax_step(s, v_ref[rows, :], *carry)

    init = (jnp.full((2 * tq, 1), NEG_BIG, F32), jnp.zeros((2 * tq, 1), F32),
            jnp.zeros((2 * tq, 2 * DIFF_DH), F32))
    _, l, acc = lax.fori_loop(0, qi + 1, body, init)
    o = acc / l
    a = o[:tq] - lam_ref[0] * o[tq:]
    o_ref[...] = (_rms(a) * g_ref[...] * post_scale).astype(o_ref.dtype)


def _diff_attention(proj, lam, tab, g, *, batch, seq, tq, post_scale):
    t = proj.shape[0]
    nq = seq // tq
    w = 2 * DIFF_DH
    return pl.pallas_call(
        functools.partial(_diff_kernel, tq=tq, post_scale=post_scale),
        grid=(batch, DIFF_HEADS, nq),
        in_specs=[pl.BlockSpec(memory_space=pltpu.SMEM),
                  pl.BlockSpec((tq, w), lambda b, h, i: (b * nq + i, COL_DQ // w + h)),
                  pl.BlockSpec((seq, w), lambda b, h, i: (b, COL_DK // w + h)),
                  pl.BlockSpec((seq, w), lambda b, h, i: (b, COL_DV // w + h)),
                  pl.BlockSpec((1, 3, tq, tq), lambda b, h, i: (h, 0, 0, 0)),
                  pl.BlockSpec((1, w), lambda b, h, i: (0, 0))],
        out_specs=pl.BlockSpec((tq, w), lambda b, h, i: (b * nq + i, h)),
        out_shape=jax.ShapeDtypeStruct((t, DIFF_W), BF16),
        compiler_params=_cparams("parallel", "parallel", "arbitrary"),
        name="diff_attention",
    )(lam, proj, proj, proj, tab, g)


def _mla_prep_kernel(cq_ref, ckv_ref, kpe_ref, gq_ref, gkv_ref, wq_ref, wk_ref, wv_ref,
                     cq_tab, sq_tab, ck_tab, sk_tab, q_ref, k_ref, v_ref):
    hq = (_rms(cq_ref[...].astype(F32)) * gq_ref[...]).astype(BF16)
    hkv = (_rms(ckv_ref[...].astype(F32)) * gkv_ref[...]).astype(BF16)
    qa = _dot(hq, wq_ref[...])
    ka = _dot(hkv, wk_ref[...])
    v_ref[...] = _dot(hkv, wv_ref[...]).astype(v_ref.dtype)
    kpe = kpe_ref[...].astype(F32)
    kpe = kpe * ck_tab[...] + pltpu.roll(kpe, LANES - MLA_ROPE, 1) * sk_tab[...]
    cq = cq_tab[...]
    sq = sq_tab[...]
    for h in range(MLA_HEADS):
        hs = slice(h * LANES, (h + 1) * LANES)
        qh = qa[:, hs]
        q_ref[:, hs] = (qh * cq + pltpu.roll(qh, LANES - MLA_ROPE, 1) * sq).astype(q_ref.dtype)
        k_ref[:, hs] = (ka[:, hs] + kpe).astype(k_ref.dtype)


def _mla_prep(proj, gq, gkv, wq, wk, wv, tabs, *, seq, tm):
    t = proj.shape[0]
    ns = seq // tm
    const = lambda i: (0, 0)
    tab_spec = pl.BlockSpec((tm, LANES), lambda i: (i % ns, 0))
    return pl.pallas_call(
        _mla_prep_kernel,
        grid=(t // tm,),
        in_specs=[pl.BlockSpec((tm, MLA_Q_LORA), lambda i: (i, COL_CQ // MLA_Q_LORA)),
                  pl.BlockSpec((tm, MLA_KV_LORA), lambda i: (i, COL_CKV // MLA_KV_LORA)),
                  pl.BlockSpec((tm, LANES), lambda i: (i, COL_KPE // LANES)),
                  pl.BlockSpec((1, MLA_Q_LORA), const),
                  pl.BlockSpec((1, MLA_KV_LORA), const),
                  pl.BlockSpec((MLA_Q_LORA, MLA_PAD_W), const),
                  pl.BlockSpec((MLA_KV_LORA, MLA_PAD_W), const),
                  pl.BlockSpec((MLA_KV_LORA, MLA_V_W), const),
                  tab_spec, tab_spec, tab_spec, tab_spec],
        out_specs=[pl.BlockSpec((tm, MLA_PAD_W), lambda i: (i, 0)),
                   pl.BlockSpec((tm, MLA_PAD_W), lambda i: (i, 0)),
                   pl.BlockSpec((tm, MLA_V_W), lambda i: (i, 0))],
        out_shape=[jax.ShapeDtypeStruct((t, MLA_PAD_W), BF16),
                   jax.ShapeDtypeStruct((t, MLA_PAD_W), BF16),
                   jax.ShapeDtypeStruct((t, MLA_V_W), BF16)],
        compiler_params=_cparams("parallel"),
        name="mla_prep",
    )(proj, proj, proj, gq, gkv, wq, wk, wv, *tabs)


def _mla_kernel(q_ref, k_ref, v_ref, o_ref, *, tq):
    qi = pl.program_id(2)
    row = lax.broadcasted_iota(jnp.int32, (tq, tq), 0)
    col = lax.broadcasted_iota(jnp.int32, (tq, tq), 1)
    outs = []
    for hh in range(2):
        hs = slice(hh * LANES, (hh + 1) * LANES)
        q = q_ref[:, hs]

        def body(j, carry, hs=hs, q=q):
            rows = pl.ds(pl.multiple_of(j * tq, tq), tq)
            return _softmax_step(_dot_nt(q, k_ref[rows, hs]), v_ref[rows, :], *carry)

        init = (jnp.full((tq, 1), NEG_BIG, F32), jnp.zeros((tq, 1), F32), jnp.zeros((tq, LANES), F32))
        carry = lax.fori_loop(0, qi, body, init)
        rows = pl.ds(pl.multiple_of(qi * tq, tq), tq)
        s = jnp.where(row >= col, _dot_nt(q, k_ref[rows, hs]), NEG_BIG)
        _, l, acc = _softmax_step(s, v_ref[rows, :], *carry)
        outs.append(acc / l)
    lane = lax.broadcasted_iota(jnp.int32, (tq, LANES), 1)
    o_ref[...] = jnp.where(lane < MLA_V, outs[0], outs[1]).astype(o_ref.dtype)


def _mla_attention(qm, km, vm, *, batch, seq, tq):
    t = qm.shape[0]
    nq = seq // tq
    return pl.pallas_call(
        functools.partial(_mla_kernel, tq=tq),
        grid=(batch, MLA_HEADS // 2, nq),
        in_specs=[pl.BlockSpec((tq, 2 * LANES), lambda b, h, i: (b * nq + i, h)),
                  pl.BlockSpec((seq, 2 * LANES), lambda b, h, i: (b, h)),
                  pl.BlockSpec((seq, LANES), lambda b, h, i: (b, h))],
        out_specs=pl.BlockSpec((tq, LANES), lambda b, h, i: (b * nq + i, h)),
        out_shape=jax.ShapeDtypeStruct((t, MLA_V_W), BF16),
        compiler_params=_cparams("parallel", "parallel", "arbitrary"),
        name="mla_attention",
    )(qm, km, vm)


def _merge_kernel(x_ref, a_ref, b_ref, c_ref, g0_ref, g1_ref, g2_ref, wa_ref, wb_ref, wc_ref, wo_ref,
                  pg_ref, o_ref):
    merged = jax.nn.sigmoid(g0_ref[...].astype(F32)) * _dot(a_ref[...], wa_ref[...])
    merged += jax.nn.sigmoid(g1_ref[...].astype(F32)) * _dot(b_ref[...], wb_ref[...])
    merged += jax.nn.sigmoid(g2_ref[...].astype(F32)) * _dot(c_ref[...], wc_ref[...])
    y = _dot(merged.astype(BF16), wo_ref[...])
    o_ref[...] = x_ref[...] + _rms(y) * pg_ref[...]


def _merge(x2, a, b, c, proj, wa, wb, wc, wo, pg, *, tm):
    t, d = x2.shape
    rowblk = pl.BlockSpec((tm, d), lambda i: (i, 0))
    wspec = pl.BlockSpec((d, d), lambda i: (0, 0))
    gate = lambda n: pl.BlockSpec((tm, d), lambda i: (i, COL_GATES // D_MODEL + n))
    return pl.pallas_call(
        _merge_kernel,
        grid=(t // tm,),
        in_specs=[rowblk, rowblk, rowblk, rowblk, gate(0), gate(1), gate(2),
                  wspec, wspec, wspec, wspec, pl.BlockSpec((1, d), lambda i: (0, 0))],
        out_specs=rowblk,
        out_shape=jax.ShapeDtypeStruct((t, d), F32),
        compiler_params=_cparams("parallel"),
        name="merge",
    )(x2, a, b, c, proj, proj, proj, wa, wb, wc, wo, pg)


def _mlp_kernel(x_ref, g_ref, wu_ref, wd_ref, pg_ref, o_ref, h_ref, acc_ref):
    k = pl.program_id(1)

    @pl.when(k == 0)
    def _():
        h_ref[...] = (_rms(x_ref[...]) * g_ref[...]).astype(BF16)
        acc_ref[...] = jnp.zeros_like(acc_ref)

    u = jnp.maximum(_dot(h_ref[...], wu_ref[...]), 0.0)
    acc_ref[...] += _dot((u * u).astype(BF16), wd_ref[...])

    @pl.when(k == pl.num_programs(1) - 1)
    def _():
        o_ref[...] = x_ref[...] + _rms(acc_ref[...]) * pg_ref[...]


def _mlp(x2, g, wu, wd, pg, *, tm, tf):
    t, d = x2.shape
    ff = wu.shape[1]
    vec = pl.BlockSpec((1, d), lambda i, k: (0, 0))
    return pl.pallas_call(
        _mlp_kernel,
        grid=(t // tm, ff // tf),
        in_specs=[pl.BlockSpec((tm, d), lambda i, k: (i, 0)), vec,
                  pl.BlockSpec((d, tf), lambda i, k: (0, k)),
                  pl.BlockSpec((tf, d), lambda i, k: (k, 0)), vec],
        out_specs=pl.BlockSpec((tm, d), lambda i, k: (i, 0)),
        out_shape=jax.ShapeDtypeStruct((t, d), F32),
        scratch_shapes=[pltpu.VMEM((tm, d), BF16), pltpu.VMEM((tm, d), F32)],
        compiler_params=_cparams("parallel", "arbitrary"),
        name="mlp",
    )(x2, g, wu, wd, pg)


def _rot_half_cols(w):
    half = w.shape[-1] // 2
    return jnp.concatenate([-w[..., half:], w[..., :half]], axis=-1)


def _prep_w_in(w_in_l):
    offs = np.cumsum(np.array(IN_SPLITS))[:-1].tolist()
    rq, rk, rv, rg, dq, dk, dv, cq, ckv, kpe, gates = jnp.split(w_in_l, offs, axis=-1)
    zeros = jnp.zeros((w_in_l.shape[0], MLA_NOPE), w_in_l.dtype)
    kpe_blk = jnp.concatenate([zeros, kpe, _rot_half_cols(kpe)], axis=-1)
    return jnp.concatenate([gates, rv, rg, dq, dk, dv, rq, rk, cq, ckv, kpe_blk], axis=-1).astype(BF16)


def _in_colscale():
    cs = np.ones((1, N_IN), np.float32)
    cs[0, COL_DQ:COL_DQ + DIFF_W] = DIFF_DH ** -0.5 * LOG2E
    cs[0, COL_RK:COL_RK + RET_QK_W] = RET_DK ** -0.5
    return jnp.asarray(cs)


def _prep_w_uq(w):
    w = w.reshape(MLA_Q_LORA, MLA_HEADS, MLA_QH)
    pe = w[..., MLA_NOPE:]
    return jnp.concatenate([w, _rot_half_cols(pe)], axis=-1).reshape(MLA_Q_LORA, MLA_PAD_W).astype(BF16)


def _prep_w_ukv(w):
    w = w.reshape(MLA_KV_LORA, MLA_HEADS, MLA_KVH)
    wk = jnp.concatenate([w[..., :MLA_NOPE], jnp.zeros_like(w[..., :MLA_NOPE])], axis=-1)
    return (wk.reshape(MLA_KV_LORA, MLA_PAD_W).astype(BF16),
            w[..., MLA_NOPE:].reshape(MLA_KV_LORA, MLA_V_W).astype(BF16))


def _ret_tables(seq):
    pos = jnp.arange(seq, dtype=F32)
    inv_freq = 1.0 / (10000.0 ** jnp.linspace(0.0, 1.0, RET_DK // 2))
    ang = pos[:, None] * inv_freq[None, :]
    cos, sin = jnp.cos(ang), jnp.sin(ang)
    cos_t = jnp.concatenate([cos, cos], axis=-1)
    sin_t = jnp.concatenate([-sin, sin], axis=-1)
    log_g = jnp.log1p(-jnp.exp2(-5.0 - jnp.arange(RET_HEADS, dtype=F32)))
    idx = jnp.arange(RET_CHUNK, dtype=F32)
    rel = idx[:, None] - idx[None, :]
    dmask = jnp.where(rel >= 0, jnp.exp(jnp.maximum(rel, 0.0)[None] * log_g[:, None, None]), 0.0)
    xi = jnp.exp((idx + 1.0)[None, :] * log_g[:, None])[:, :, None]
    zeta = jnp.exp((RET_CHUNK - 1.0 - idx)[None, :] * log_g[:, None])[:, :, None]
    g_chunk = jnp.exp(RET_CHUNK * log_g)[:, None, None]
    return (cos_t, sin_t, dmask,
            jnp.broadcast_to(xi, (RET_HEADS, RET_CHUNK, RET_DK)),
            jnp.broadcast_to(zeta, (RET_HEADS, RET_CHUNK, RET_DK)),
            jnp.broadcast_to(g_chunk, (RET_HEADS, 1, RET_DV)))


def _mla_tables(seq):
    pos = jnp.arange(seq, dtype=F32)
    inv_freq = 1.0 / (ROPE_THETA ** (jnp.arange(0, MLA_ROPE, 2, dtype=F32) / MLA_ROPE))
    ang = pos[:, None] * inv_freq[None, :]
    cos, sin = jnp.cos(ang), jnp.sin(ang)
    z_nope = jnp.zeros((seq, MLA_NOPE), F32)
    z_rope = jnp.zeros((seq, MLA_ROPE), F32)
    cos_k = jnp.concatenate([z_nope, cos, cos, z_rope], axis=-1)
    sin_k = jnp.concatenate([z_nope, sin, sin, z_rope], axis=-1)
    qscale = MLA_QH ** -0.5 * LOG2E
    cos_q = jnp.concatenate([jnp.ones((seq, MLA_NOPE), F32), cos, cos, z_rope], axis=-1) * qscale
    sin_q = sin_k * qscale
    return cos_q, sin_q, cos_k, sin_k


def _t5_bucket(n):
    max_exact = REL_BUCKETS // 2
    nf = jnp.maximum(n, 1).astype(F32)
    large = max_exact + (jnp.log(nf / max_exact) / math.log(REL_MAX_DIST / max_exact)
                         * (REL_BUCKETS - max_exact)).astype(jnp.int32)
    large = jnp.minimum(large, REL_BUCKETS - 1)
    return jnp.where(n < max_exact, n, large)


def _bias_table(rel_bias, tq):
    assert tq >= REL_MAX_DIST
    i = jnp.arange(tq)[:, None]
    j = jnp.arange(tq)[None, :]
    dist = jnp.stack([i - j, i - j + tq, jnp.full((tq, tq), 2 * tq)])
    bias = rel_bias[_t5_bucket(jnp.maximum(dist, 0))].astype(F32) * LOG2E
    bias = jnp.where((dist >= 0)[..., None], bias, NEG_BIG)
    return jnp.transpose(bias, (3, 0, 1, 2))


def _pick(t, want):
    return want if t % want == 0 else t


def kernel(x, rel_bias, pre_mix_g, w_in, w_ret_o, lambda_q1, lambda_k1, lambda_q2, lambda_k2, diff_subln_g, w_diff_o, mla_q_norm_g, w_mla_uq, mla_kv_norm_g, w_mla_ukv, w_mla_o, w_out, post_mix_g, pre_mlp_g, w_up, w_down, post_mlp_g):
    batch, seq, d = x.shape
    t = batch * seq
    depth = w_in.shape[0]
    tq = _pick(seq, 256)
    ret_tabs = _ret_tables(seq)
    mla_tabs = _mla_tables(seq)
    bias_tab = _bias_table(rel_bias, tq)
    colscale = _in_colscale()
    row = lambda v: v.reshape(1, -1).astype(F32)

    x2 = x.reshape(t, d)
    for l in range(depth):
        proj = _in_proj(x2, row(pre_mix_g[l]), _prep_w_in(w_in[l]), colscale,
                        tm=_pick(t, 1024), tn=512)

        ret = _retention(proj, ret_tabs, batch=batch, seq=seq)

        lambda_init = 0.8 - 0.6 * math.exp(-0.3 * l)
        lam = (jnp.exp(jnp.sum(lambda_q1[l] * lambda_k1[l]).astype(F32))
               - jnp.exp(jnp.sum(lambda_q2[l] * lambda_k2[l]).astype(F32)) + lambda_init)
        da = _diff_attention(proj, lam.reshape(1), bias_tab, row(diff_subln_g[l]),
                             batch=batch, seq=seq, tq=tq, post_scale=1.0 - lambda_init)

        wk, wv = _prep_w_ukv(w_mla_ukv[l])
        qm, km, vm = _mla_prep(proj, row(mla_q_norm_g[l]), row(mla_kv_norm_g[l]),
                               _prep_w_uq(w_mla_uq[l]), wk, wv, mla_tabs, seq=seq, tm=_pick(seq, 512))
        mo = _mla_attention(qm, km, vm, batch=batch, seq=seq, tq=tq)

        x2 = _merge(x2, ret, da, mo, proj, w_ret_o[l].astype(BF16), w_diff_o[l].astype(BF16),
                    w_mla_o[l].astype(BF16), w_out[l].astype(BF16), row(post_mix_g[l]), tm=_pick(t, 256))

        x2 = _mlp(x2, row(pre_mlp_g[l]), w_up[l].astype(BF16), w_down[l].astype(BF16),
                  row(post_mlp_g[l]), tm=_pick(t, 1024), tf=512)
    return x2.reshape(batch, seq, d)
```

```python
import functools
import math

import jax
import jax.numpy as jnp
import numpy as np
from jax import lax
from jax.experimental import pallas as pl
from jax.experimental.pallas import tpu as pltpu

F32 = jnp.float32
BF16 = jnp.bfloat16

D_MODEL = 1024
EPS = 1e-6
RET_HEADS = 4
RET_DK = 128
RET_DV = 256
RET_CHUNK = 128
DIFF_HEADS = 8
DIFF_DH = 64
MLA_HEADS = 16
MLA_Q_LORA = 256
MLA_KV_LORA = 128
MLA_NOPE = 64
MLA_ROPE = 32
MLA_V = 64
ROPE_THETA = 10000.0
REL_BUCKETS = 32
REL_MAX_DIST = 128
D_FF = 4 * D_MODEL
N_BRANCH = 3

RET_QK_W = RET_HEADS * RET_DK
RET_V_W = RET_HEADS * RET_DV
DIFF_W = DIFF_HEADS * 2 * DIFF_DH
MLA_QH = MLA_NOPE + MLA_ROPE
MLA_KVH = MLA_NOPE + MLA_V
MLA_V_W = MLA_HEADS * MLA_V
MLA_PAD_W = MLA_HEADS * 128
IN_SPLITS = (RET_QK_W, RET_QK_W, RET_V_W, RET_V_W, DIFF_W, DIFF_W, DIFF_W,
             MLA_Q_LORA, MLA_KV_LORA, MLA_ROPE, N_BRANCH * D_MODEL)

LANES = 128
LOG2E = math.log2(math.e)
NEG_BIG = -1e30
VMEM_LIMIT = 56 * 1024 * 1024

COL_GATES = 0
COL_RV = COL_GATES + N_BRANCH * D_MODEL
COL_RG = COL_RV + RET_V_W
COL_DQ = COL_RG + RET_V_W
COL_DK = COL_DQ + DIFF_W
COL_DV = COL_DK + DIFF_W
COL_RQ = COL_DV + DIFF_W
COL_RK = COL_RQ + RET_QK_W
COL_CQ = COL_RK + RET_QK_W
COL_CKV = COL_CQ + MLA_Q_LORA
COL_KPE = COL_CKV + MLA_KV_LORA
N_IN = COL_KPE + LANES


def _cparams(*sem):
    return pltpu.CompilerParams(dimension_semantics=sem, vmem_limit_bytes=VMEM_LIMIT)


def _rms(xf):
    return xf * lax.rsqrt(jnp.mean(xf * xf, axis=-1, keepdims=True) + EPS)


def _dot(a, b):
    return jnp.dot(a, b, preferred_element_type=F32)


def _dot_nt(a, b):
    return lax.dot_general(a, b, (((1,), (1,)), ((), ())), preferred_element_type=F32)


def _in_proj_kernel(x_ref, g_ref, w_ref, cs_ref, o_ref, h_ref):
    @pl.when(pl.program_id(1) == 0)
    def _():
        h_ref[...] = (_rms(x_ref[...]) * g_ref[...]).astype(BF16)

    o_ref[...] = (_dot(h_ref[...], w_ref[...]) * cs_ref[...]).astype(o_ref.dtype)


def _in_proj(x2, g, w, colscale, *, tm, tn):
    t, d = x2.shape
    n = w.shape[1]
    return pl.pallas_call(
        _in_proj_kernel,
        grid=(t // tm, n // tn),
        in_specs=[pl.BlockSpec((tm, d), lambda i, j: (i, 0)),
                  pl.BlockSpec((1, d), lambda i, j: (0, 0)),
                  pl.BlockSpec((d, tn), lambda i, j: (0, j)),
                  pl.BlockSpec((1, tn), lambda i, j: (0, j))],
        out_specs=pl.BlockSpec((tm, tn), lambda i, j: (i, j)),
        out_shape=jax.ShapeDtypeStruct((t, n), BF16),
        scratch_shapes=[pltpu.VMEM((tm, d), BF16)],
        compiler_params=_cparams("parallel", "arbitrary"),
        name="in_proj",
    )(x2, g, w, colscale)


def _ret_kernel(q_ref, k_ref, v_ref, rg_ref, cos_ref, sin_ref, dm_ref, xi_ref, zt_ref, gc_ref,
                o_ref, st_ref):
    @pl.when(pl.program_id(1) == 0)
    def _():
        st_ref[...] = jnp.zeros_like(st_ref)

    cos = cos_ref[...]
    sin = sin_ref[...]
    for h in range(RET_HEADS):
        qs = slice(h * RET_DK, (h + 1) * RET_DK)
        vs = slice(h * RET_DV, (h + 1) * RET_DV)
        q = q_ref[:, qs].astype(F32)
        k = k_ref[:, qs].astype(F32)
        qr = q * cos + pltpu.roll(q, RET_DK // 2, 1) * sin
        kr = k * cos + pltpu.roll(k, RET_DK // 2, 1) * sin
        v = v_ref[:, vs]
        st = st_ref[h]
        inner = _dot_nt(qr.astype(BF16), kr.astype(BF16)) * dm_ref[h]
        out = _dot(inner.astype(BF16), v) + _dot((qr * xi_ref[h]).astype(BF16), st.astype(BF16))
        kz_t = (kr * zt_ref[h]).T.astype(BF16)
        st_ref[h] = gc_ref[h] * st + _dot(kz_t, v)
        rg = rg_ref[:, vs].astype(F32)
        o_ref[:, vs] = (_rms(out) * (rg * jax.nn.sigmoid(rg))).astype(o_ref.dtype)


def _retention(proj, tabs, *, batch, seq):
    t = proj.shape[0]
    c = RET_CHUNK
    nc = seq // c
    cos, sin, dmask, xi, zeta, gch = tabs
    row = lambda b, n: b * nc + n
    whole3 = lambda b, n: (0, 0, 0)
    return pl.pallas_call(
        _ret_kernel,
        grid=(batch, nc),
        in_specs=[pl.BlockSpec((c, RET_QK_W), lambda b, n: (row(b, n), COL_RQ // RET_QK_W)),
                  pl.BlockSpec((c, RET_QK_W), lambda b, n: (row(b, n), COL_RK // RET_QK_W)),
                  pl.BlockSpec((c, RET_V_W), lambda b, n: (row(b, n), COL_RV // RET_V_W)),
                  pl.BlockSpec((c, RET_V_W), lambda b, n: (row(b, n), COL_RG // RET_V_W)),
                  pl.BlockSpec((c, RET_DK), lambda b, n: (n, 0)),
                  pl.BlockSpec((c, RET_DK), lambda b, n: (n, 0)),
                  pl.BlockSpec((RET_HEADS, c, c), whole3),
                  pl.BlockSpec((RET_HEADS, c, RET_DK), whole3),
                  pl.BlockSpec((RET_HEADS, c, RET_DK), whole3),
                  pl.BlockSpec((RET_HEADS, 1, RET_DV), whole3)],
        out_specs=pl.BlockSpec((c, RET_V_W), lambda b, n: (row(b, n), 0)),
        out_shape=jax.ShapeDtypeStruct((t, RET_V_W), BF16),
        scratch_shapes=[pltpu.VMEM((RET_HEADS, RET_DK, RET_DV), F32)],
        compiler_params=_cparams("parallel", "arbitrary"),
        name="retention",
    )(proj, proj, proj, proj, cos, sin, dmask, xi, zeta, gch)


def _softmax_update(s, vt, m_ref, l_ref, acc_ref):
    m = m_ref[...]
    m_new = jnp.maximum(m, jnp.max(s, axis=0, keepdims=True))
    alpha = jnp.exp2(m - m_new)
    p = jnp.exp2(s - m_new)
    m_ref[...] = m_new
    l_ref[...] = alpha * l_ref[...] + jnp.sum(p, axis=0, keepdims=True)
    acc_ref[...] = alpha * acc_ref[...] + _dot(vt, p.astype(BF16))


def _softmax_reset(m_ref, l_ref, acc_ref):
    m_ref[...] = jnp.full(m_ref.shape, NEG_BIG, F32)
    l_ref[...] = jnp.zeros(l_ref.shape, F32)
    acc_ref[...] = jnp.zeros(acc_ref.shape, F32)


def _softmax_scratch(group, dv, tk, n):
    return [pltpu.VMEM((2, group, tk, n), F32), pltpu.VMEM((group, 1, n), F32),
            pltpu.VMEM((group, 1, n), F32), pltpu.VMEM((group, dv, n), F32)]


def _tile(i, size):
    return pl.ds(pl.multiple_of(i * size, size), size)


def _causal_tile_pipeline(nq, logits, softmax, finalize):
    def step(qi, j, slot):
        last = j == qi
        nqi = jnp.where(last, qi + 1, qi)
        nj = jnp.where(last, 0, j + 1)
        logits(jnp.minimum(nqi, nq - 1), nj, 1 - slot)
        softmax(j, slot)

        @pl.when(last)
        def _():
            finalize(qi)

        return nqi, nj

    n_pairs = nq * (nq + 1) // 2
    zero = jnp.int32(0)
    logits(zero, zero, 0)
    carry = lax.fori_loop(0, n_pairs // 2, lambda _, c: step(*step(*c, 0), 1), (zero, zero))
    if n_pairs % 2:
        step(*carry, 0)


def _diff_kernel(lam_ref, q_ref, k_ref, v_ref, tab_ref, g_ref, o_ref, vt_ref, s_ref, m_ref, l_ref, acc_ref,
                 *, tq, group):
    w = 2 * DIFF_DH
    nq = vt_ref.shape[1]
    hslice = [slice(g * w, (g + 1) * w) for g in range(group)]
    state = [(m_ref.at[g], l_ref.at[g], acc_ref.at[g]) for g in range(group)]
    for g in range(group):
        _softmax_reset(*state[g])
        for c in range(nq):
            vt_ref[g, c] = v_ref[c * tq:(c + 1) * tq, hslice[g]].astype(F32).T.astype(BF16)

    lane = lax.broadcasted_iota(jnp.int32, (1, w), 1)
    first_half = (lane < DIFF_DH).astype(BF16)
    second_half = (lane >= DIFF_DH).astype(BF16)

    def logits(qi, j, slot):
        idx = jnp.minimum(qi - j, 2)
        for g in range(group):
            q = q_ref[_tile(qi, tq), hslice[g]]
            q2 = jnp.concatenate([q * first_half, q * second_half], axis=0)
            bias = tab_ref[g, idx]
            s_ref[slot, g] = (_dot_nt(k_ref[_tile(j, tq), hslice[g]], q2)
                              + jnp.concatenate([bias, bias], axis=1))

    def softmax(j, slot):
        for g in range(group):
            _softmax_update(s_ref[slot, g], vt_ref[g, j], *state[g])

    def finalize(qi):
        for g in range(group):
            m, l, acc = state[g]
            o = acc[...] / l[...]
            a = o[:, :tq] - lam_ref[0] * o[:, tq:]
            a = a * lax.rsqrt(jnp.mean(a * a, axis=0, keepdims=True) + EPS)
            o_ref[_tile(qi, tq), hslice[g]] = (a.T * g_ref[...]).astype(o_ref.dtype)
            _softmax_reset(m, l, acc)

    _causal_tile_pipeline(nq, logits, softmax, finalize)


def _diff_attention(proj, lam, tab, g, *, batch, seq, tq, group):
    t = proj.shape[0]
    nq = seq // tq
    w = group * 2 * DIFF_DH
    return pl.pallas_call(
        functools.partial(_diff_kernel, tq=tq, group=group),
        grid=(batch, DIFF_HEADS // group),
        in_specs=[pl.BlockSpec(memory_space=pltpu.SMEM),
                  pl.BlockSpec((seq, w), lambda b, h: (b, COL_DQ // w + h)),
                  pl.BlockSpec((seq, w), lambda b, h: (b, COL_DK // w + h)),
                  pl.BlockSpec((seq, w), lambda b, h: (b, COL_DV // w + h)),
                  pl.BlockSpec((group, 3, tq, tq), lambda b, h: (h, 0, 0, 0)),
                  pl.BlockSpec((1, 2 * DIFF_DH), lambda b, h: (0, 0))],
        out_specs=pl.BlockSpec((seq, w), lambda b, h: (b, h)),
        out_shape=jax.ShapeDtypeStruct((t, DIFF_W), BF16),
        scratch_shapes=[pltpu.VMEM((group, nq, 2 * DIFF_DH, tq), BF16)]
        + _softmax_scratch(group, 2 * DIFF_DH, tq, 2 * tq),
        compiler_params=_cparams("parallel", "parallel"),
        name="diff_attention",
    )(lam, proj, proj, proj, tab, g)


def _mla_prep_kernel(cq_ref, ckv_ref, kpe_ref, gq_ref, gkv_ref, wq_ref, wk_ref, wvt_ref,
                     cq_tab, sq_tab, ck_tab, sk_tab, q_ref, k_ref, vt_ref):
    hq = (_rms(cq_ref[...].astype(F32)) * gq_ref[...]).astype(BF16)
    hkv = (_rms(ckv_ref[...].astype(F32)) * gkv_ref[...]).astype(BF16)
    qa = _dot(hq, wq_ref[...])
    ka = _dot(hkv, wk_ref[...])
    tk = vt_ref.shape[-1]
    for c in range(vt_ref.shape[0]):
        vt_ref[c] = _dot_nt(wvt_ref[...], hkv[c * tk:(c + 1) * tk]).astype(vt_ref.dtype)
    kpe = kpe_ref[...].astype(F32)
    kpe = kpe * ck_tab[...] + pltpu.roll(kpe, LANES - MLA_ROPE, 1) * sk_tab[...]
    cq = cq_tab[...]
    sq = sq_tab[...]
    for h in range(MLA_HEADS):
        hs = slice(h * LANES, (h + 1) * LANES)
        qh = qa[:, hs]
        q_ref[:, hs] = (qh * cq + pltpu.roll(qh, LANES - MLA_ROPE, 1) * sq).astype(q_ref.dtype)
        k_ref[:, hs] = (ka[:, hs] + kpe).astype(k_ref.dtype)


def _mla_prep(proj, gq, gkv, wq, wk, wvt, tabs, *, seq, tm, tk):
    t = proj.shape[0]
    ns = seq // tm
    nc = tm // tk
    const = lambda i: (0, 0)
    tab_spec = pl.BlockSpec((tm, LANES), lambda i: (i % ns, 0))
    return pl.pallas_call(
        _mla_prep_kernel,
        grid=(t // tm,),
        in_specs=[pl.BlockSpec((tm, MLA_Q_LORA), lambda i: (i, COL_CQ // MLA_Q_LORA)),
                  pl.BlockSpec((tm, MLA_KV_LORA), lambda i: (i, COL_CKV // MLA_KV_LORA)),
                  pl.BlockSpec((tm, LANES), lambda i: (i, COL_KPE // LANES)),
                  pl.BlockSpec((1, MLA_Q_LORA), const),
                  pl.BlockSpec((1, MLA_KV_LORA), const),
                  pl.BlockSpec((MLA_Q_LORA, MLA_PAD_W), const),
                  pl.BlockSpec((MLA_KV_LORA, MLA_PAD_W), const),
                  pl.BlockSpec((MLA_V_W, MLA_KV_LORA), const),
                  tab_spec, tab_spec, tab_spec, tab_spec],
        out_specs=[pl.BlockSpec((tm, MLA_PAD_W), lambda i: (i, 0)),
                   pl.BlockSpec((tm, MLA_PAD_W), lambda i: (i, 0)),
                   pl.BlockSpec((nc, MLA_V_W, tk), lambda i: (i, 0, 0))],
        out_shape=[jax.ShapeDtypeStruct((t, MLA_PAD_W), BF16),
                   jax.ShapeDtypeStruct((t, MLA_PAD_W), BF16),
                   jax.ShapeDtypeStruct((t // tk, MLA_V_W, tk), BF16)],
        compiler_params=_cparams("parallel"),
        name="mla_prep",
    )(proj, proj, proj, gq, gkv, wq, wk, wvt, *tabs)


def _mla_kernel(q_ref, k_ref, vt_ref, mask_ref, o_ref, s_ref, m_ref, l_ref, acc_ref, *, tq, group):
    hslice = [slice(g * LANES, (g + 1) * LANES) for g in range(group)]
    vslice = [slice(g * MLA_V, (g + 1) * MLA_V) for g in range(group)]
    state = [(m_ref.at[g], l_ref.at[g], acc_ref.at[g]) for g in range(group)]
    for g in range(group):
        _softmax_reset(*state[g])

    def logits(qi, j, slot):
        mask = mask_ref[(j == qi).astype(jnp.int32)]
        for g in range(group):
            s_ref[slot, g] = _dot_nt(k_ref[_tile(j, tq), hslice[g]], q_ref[_tile(qi, tq), hslice[g]]) + mask

    def softmax(j, slot):
        for g in range(group):
            _softmax_update(s_ref[slot, g], vt_ref[j, vslice[g], :], *state[g])

    def finalize(qi):
        o = jnp.concatenate([acc[...] / l[...] for _, l, acc in state], axis=0)
        o_ref[_tile(qi, tq), :] = o.T.astype(o_ref.dtype)
        for g in range(group):
            _softmax_reset(*state[g])

    _causal_tile_pipeline(vt_ref.shape[0], logits, softmax, finalize)


def _mla_attention(qm, km, vt, mask, *, batch, seq, tq, group):
    t = qm.shape[0]
    nq = seq // tq
    return pl.pallas_call(
        functools.partial(_mla_kernel, tq=tq, group=group),
        grid=(batch, MLA_HEADS // group),
        in_specs=[pl.BlockSpec((seq, group * LANES), lambda b, h: (b, h)),
                  pl.BlockSpec((seq, group * LANES), lambda b, h: (b, h)),
                  pl.BlockSpec((nq, group * MLA_V, tq), lambda b, h: (b, h, 0)),
                  pl.BlockSpec((2, tq, tq), lambda b, h: (0, 0, 0))],
        out_specs=pl.BlockSpec((seq, group * MLA_V), lambda b, h: (b, h)),
        out_shape=jax.ShapeDtypeStruct((t, MLA_V_W), BF16),
        scratch_shapes=_softmax_scratch(group, MLA_V, tq, tq),
        compiler_params=_cparams("parallel", "parallel"),
        name="mla_attention",
    )(qm, km, vt, mask)


def _merge_kernel(x_ref, a_ref, b_ref, c_ref, g0_ref, g1_ref, g2_ref, wa_ref, wb_ref, wc_ref, wo_ref,
                  pg_ref, o_ref):
    merged = jax.nn.sigmoid(g0_ref[...].astype(F32)) * _dot(a_ref[...], wa_ref[...])
    merged += jax.nn.sigmoid(g1_ref[...].astype(F32)) * _dot(b_ref[...], wb_ref[...])
    merged += jax.nn.sigmoid(g2_ref[...].astype(F32)) * _dot(c_ref[...], wc_ref[...])
    y = _dot(merged.astype(BF16), wo_ref[...])
    o_ref[...] = x_ref[...] + _rms(y) * pg_ref[...]


def _merge(x2, a, b, c, proj, wa, wb, wc, wo, pg, *, tm):
    t, d = x2.shape
    rowblk = pl.BlockSpec((tm, d), lambda i: (i, 0))
    wspec = pl.BlockSpec((d, d), lambda i: (0, 0))
    gate = lambda n: pl.BlockSpec((tm, d), lambda i: (i, COL_GATES // D_MODEL + n))
    return pl.pallas_call(
        _merge_kernel,
        grid=(t // tm,),
        in_specs=[rowblk, rowblk, rowblk, rowblk, gate(0), gate(1), gate(2),
                  wspec, wspec, wspec, wspec, pl.BlockSpec((1, d), lambda i: (0, 0))],
        out_specs=rowblk,
        out_shape=jax.ShapeDtypeStruct((t, d), F32),
        compiler_params=_cparams("parallel"),
        name="merge",
    )(x2, a, b, c, proj, proj, proj, wa, wb, wc, wo, pg)


def _mlp_kernel(x_ref, g_ref, wu_ref, wd_ref, pg_ref, o_ref, h_ref, acc_ref):
    k = pl.program_id(1)

    @pl.when(k == 0)
    def _():
        h_ref[...] = (_rms(x_ref[...]) * g_ref[...]).astype(BF16)
        acc_ref[...] = jnp.zeros_like(acc_ref)

    u = jnp.maximum(_dot(h_ref[...], wu_ref[...]), 0.0)
    acc_ref[...] += _dot((u * u).astype(BF16), wd_ref[...])

    @pl.when(k == pl.num_programs(1) - 1)
    def _():
        o_ref[...] = x_ref[...] + _rms(acc_ref[...]) * pg_ref[...]


def _mlp(x2, g, wu, wd, pg, *, tm, tf):
    t, d = x2.shape
    ff = wu.shape[1]
    vec = pl.BlockSpec((1, d), lambda i, k: (0, 0))
    return pl.pallas_call(
        _mlp_kernel,
        grid=(t // tm, ff // tf),
        in_specs=[pl.BlockSpec((tm, d), lambda i, k: (i, 0)), vec,
                  pl.BlockSpec((d, tf), lambda i, k: (0, k)),
                  pl.BlockSpec((tf, d), lambda i, k: (k, 0)), vec],
        out_specs=pl.BlockSpec((tm, d), lambda i, k: (i, 0)),
        out_shape=jax.ShapeDtypeStruct((t, d), F32),
        scratch_shapes=[pltpu.VMEM((tm, d), BF16), pltpu.VMEM((tm, d), F32)],
        compiler_params=_cparams("parallel", "arbitrary"),
        name="mlp",
    )(x2, g, wu, wd, pg)


def _rot_half_cols(w):
    half = w.shape[-1] // 2
    return jnp.concatenate([-w[..., half:], w[..., :half]], axis=-1)


def _prep_w_in(w_in_l):
    offs = np.cumsum(np.array(IN_SPLITS))[:-1].tolist()
    rq, rk, rv, rg, dq, dk, dv, cq, ckv, kpe, gates = jnp.split(w_in_l, offs, axis=-1)
    zeros = jnp.zeros((w_in_l.shape[0], MLA_NOPE), w_in_l.dtype)
    kpe_blk = jnp.concatenate([zeros, kpe, _rot_half_cols(kpe)], axis=-1)
    return jnp.concatenate([gates, rv, rg, dq, dk, dv, rq, rk, cq, ckv, kpe_blk], axis=-1).astype(BF16)


def _in_colscale():
    cs = np.ones((1, N_IN), np.float32)
    cs[0, COL_DQ:COL_DQ + DIFF_W] = DIFF_DH ** -0.5 * LOG2E
    cs[0, COL_RK:COL_RK + RET_QK_W] = RET_DK ** -0.5
    return jnp.asarray(cs)


def _prep_w_uq(w):
    w = w.reshape(MLA_Q_LORA, MLA_HEADS, MLA_QH)
    pe = w[..., MLA_NOPE:]
    return jnp.concatenate([w, _rot_half_cols(pe)], axis=-1).reshape(MLA_Q_LORA, MLA_PAD_W).astype(BF16)


def _prep_w_ukv(w):
    w = w.reshape(MLA_KV_LORA, MLA_HEADS, MLA_KVH)
    wk = jnp.concatenate([w[..., :MLA_NOPE], jnp.zeros_like(w[..., :MLA_NOPE])], axis=-1)
    return (wk.reshape(MLA_KV_LORA, MLA_PAD_W).astype(BF16),
            w[..., MLA_NOPE:].reshape(MLA_KV_LORA, MLA_V_W).T.astype(BF16))


def _ret_tables(seq):
    pos = jnp.arange(seq, dtype=F32)
    inv_freq = 1.0 / (10000.0 ** jnp.linspace(0.0, 1.0, RET_DK // 2))
    ang = pos[:, None] * inv_freq[None, :]
    cos, sin = jnp.cos(ang), jnp.sin(ang)
    cos_t = jnp.concatenate([cos, cos], axis=-1)
    sin_t = jnp.concatenate([-sin, sin], axis=-1)
    log_g = jnp.log1p(-jnp.exp2(-5.0 - jnp.arange(RET_HEADS, dtype=F32)))
    idx = jnp.arange(RET_CHUNK, dtype=F32)
    rel = idx[:, None] - idx[None, :]
    dmask = jnp.where(rel >= 0, jnp.exp(jnp.maximum(rel, 0.0)[None] * log_g[:, None, None]), 0.0)
    xi = jnp.exp((idx + 1.0)[None, :] * log_g[:, None])[:, :, None]
    zeta = jnp.exp((RET_CHUNK - 1.0 - idx)[None, :] * log_g[:, None])[:, :, None]
    g_chunk = jnp.exp(RET_CHUNK * log_g)[:, None, None]
    return (cos_t, sin_t, dmask,
            jnp.broadcast_to(xi, (RET_HEADS, RET_CHUNK, RET_DK)),
            jnp.broadcast_to(zeta, (RET_HEADS, RET_CHUNK, RET_DK)),
            jnp.broadcast_to(g_chunk, (RET_HEADS, 1, RET_DV)))


def _mla_tables(seq):
    pos = jnp.arange(seq, dtype=F32)
    inv_freq = 1.0 / (ROPE_THETA ** (jnp.arange(0, MLA_ROPE, 2, dtype=F32) / MLA_ROPE))
    ang = pos[:, None] * inv_freq[None, :]
    cos, sin = jnp.cos(ang), jnp.sin(ang)
    z_nope = jnp.zeros((seq, MLA_NOPE), F32)
    z_rope = jnp.zeros((seq, MLA_ROPE), F32)
    cos_k = jnp.concatenate([z_nope, cos, cos, z_rope], axis=-1)
    sin_k = jnp.concatenate([z_nope, sin, sin, z_rope], axis=-1)
    qscale = MLA_QH ** -0.5 * LOG2E
    cos_q = jnp.concatenate([jnp.ones((seq, MLA_NOPE), F32), cos, cos, z_rope], axis=-1) * qscale
    sin_q = sin_k * qscale
    return cos_q, sin_q, cos_k, sin_k


def _t5_bucket(n):
    max_exact = REL_BUCKETS // 2
    nf = jnp.maximum(n, 1).astype(F32)
    large = max_exact + (jnp.log(nf / max_exact) / math.log(REL_MAX_DIST / max_exact)
                         * (REL_BUCKETS - max_exact)).astype(jnp.int32)
    large = jnp.minimum(large, REL_BUCKETS - 1)
    return jnp.where(n < max_exact, n, large)


def _bias_table(rel_bias, tq):
    assert tq >= REL_MAX_DIST
    kk = jnp.arange(tq)[:, None]
    qq = jnp.arange(tq)[None, :]
    dist = jnp.stack([qq - kk, qq - kk + tq, jnp.full((tq, tq), 2 * tq)])
    onehot = jax.nn.one_hot(_t5_bucket(jnp.maximum(dist, 0)), REL_BUCKETS, dtype=F32)
    bias = jnp.einsum("cijb,bh->hcij", onehot, rel_bias.astype(F32),
                      precision=lax.Precision.HIGHEST) * LOG2E
    return jnp.where((dist >= 0)[None], bias, NEG_BIG)


def _causal_table(tq):
    kk = jnp.arange(tq)[:, None]
    qq = jnp.arange(tq)[None, :]
    return jnp.stack([jnp.zeros((tq, tq), F32), jnp.where(qq >= kk, 0.0, NEG_BIG).astype(F32)])


def _pick(t, want):
    return want if t % want == 0 else t


def kernel(x, rel_bias, pre_mix_g, w_in, w_ret_o, lambda_q1, lambda_k1, lambda_q2, lambda_k2, diff_subln_g, w_diff_o, mla_q_norm_g, w_mla_uq, mla_kv_norm_g, w_mla_ukv, w_mla_o, w_out, post_mix_g, pre_mlp_g, w_up, w_down, post_mlp_g):
    batch, seq, d = x.shape
    t = batch * seq
    depth = w_in.shape[0]
    tq = _pick(seq, 256)
    ret_tabs = _ret_tables(seq)
    mla_tabs = _mla_tables(seq)
    bias_tab = _bias_table(rel_bias, tq)
    causal_tab = _causal_table(tq)
    colscale = _in_colscale()
    row = lambda v: v.reshape(1, -1).astype(F32)

    x2 = x.reshape(t, d)
    for l in range(depth):
        proj = _in_proj(x2, row(pre_mix_g[l]), _prep_w_in(w_in[l]), colscale,
                        tm=_pick(t, 1024), tn=512)

        ret = _retention(proj, ret_tabs, batch=batch, seq=seq)

        lambda_init = 0.8 - 0.6 * math.exp(-0.3 * l)
        lam = (jnp.exp(jnp.sum(lambda_q1[l] * lambda_k1[l]).astype(F32))
               - jnp.exp(jnp.sum(lambda_q2[l] * lambda_k2[l]).astype(F32)) + lambda_init)
        da = _diff_attention(proj, lam.reshape(1), bias_tab, row(diff_subln_g[l]) * (1.0 - lambda_init),
                             batch=batch, seq=seq, tq=tq, group=2)

        wk, wvt = _prep_w_ukv(w_mla_ukv[l])
        qm, km, vt = _mla_prep(proj, row(mla_q_norm_g[l]), row(mla_kv_norm_g[l]),
                               _prep_w_uq(w_mla_uq[l]), wk, wvt, mla_tabs, seq=seq, tm=_pick(seq, 512), tk=tq)
        mo = _mla_attention(qm, km, vt, causal_tab, batch=batch, seq=seq, tq=tq, group=4)

        x2 = _merge(x2, ret, da, mo, proj, w_ret_o[l].astype(BF16), w_diff_o[l].astype(BF16),
                    w_mla_o[l].astype(BF16), w_out[l].astype(BF16), row(post_mix_g[l]), tm=_pick(t, 256))

        x2 = _mlp(x2, row(pre_mlp_g[l]), w_up[l].astype(BF16), w_down[l].astype(BF16),
                  row(post_mlp_g[l]), tm=_pick(t, 1024), tf=512)
    return x2.reshape(batch, seq, d)
```

```python
import functools
import math

import jax
import jax.numpy as jnp
import numpy as np
from jax import lax
from jax.experimental import pallas as pl
from jax.experimental.pallas import tpu as pltpu

F32 = jnp.float32
BF16 = jnp.bfloat16

D_MODEL = 1024
EPS = 1e-6
RET_HEADS = 4
RET_DK = 128
RET_DV = 256
RET_CHUNK = 128
DIFF_HEADS = 8
DIFF_DH = 64
MLA_HEADS = 16
MLA_Q_LORA = 256
MLA_KV_LORA = 128
MLA_NOPE = 64
MLA_ROPE = 32
MLA_V = 64
ROPE_THETA = 10000.0
REL_BUCKETS = 32
REL_MAX_DIST = 128
D_FF = 4 * D_MODEL
N_BRANCH = 3

RET_QK_W = RET_HEADS * RET_DK
RET_V_W = RET_HEADS * RET_DV
DIFF_W = DIFF_HEADS * 2 * DIFF_DH
MLA_QH = MLA_NOPE + MLA_ROPE
MLA_KVH = MLA_NOPE + MLA_V
MLA_V_W = MLA_HEADS * MLA_V
MLA_PAD_W = MLA_HEADS * 128
IN_SPLITS = (RET_QK_W, RET_QK_W, RET_V_W, RET_V_W, DIFF_W, DIFF_W, DIFF_W,
             MLA_Q_LORA, MLA_KV_LORA, MLA_ROPE, N_BRANCH * D_MODEL)

LANES = 128
LOG2E = math.log2(math.e)
NEG_BIG = -1e30
VMEM_LIMIT = 56 * 1024 * 1024

COL_GATES = 0
COL_RV = COL_GATES + N_BRANCH * D_MODEL
COL_RG = COL_RV + RET_V_W
COL_DQ = COL_RG + RET_V_W
COL_DK = COL_DQ + DIFF_W
COL_DV = COL_DK + DIFF_W
COL_RQ = COL_DV + DIFF_W
COL_RK = COL_RQ + RET_QK_W
COL_CQ = COL_RK + RET_QK_W
COL_CKV = COL_CQ + MLA_Q_LORA
COL_KPE = COL_CKV + MLA_KV_LORA
N_IN = COL_KPE + LANES


def _cparams(*sem):
    return pltpu.CompilerParams(dimension_semantics=sem, vmem_limit_bytes=VMEM_LIMIT)


def _rms(xf):
    return xf * lax.rsqrt(jnp.mean(xf * xf, axis=-1, keepdims=True) + EPS)


def _dot(a, b):
    return jnp.dot(a, b, preferred_element_type=F32)


def _dot_nt(a, b):
    return lax.dot_general(a, b, (((1,), (1,)), ((), ())), preferred_element_type=F32)


def _in_proj_kernel(x_ref, g_ref, w_ref, cs_ref, o_ref, h_ref):
    @pl.when(pl.program_id(1) == 0)
    def _():
        h_ref[...] = (_rms(x_ref[...]) * g_ref[...]).astype(BF16)

    o_ref[...] = (_dot(h_ref[...], w_ref[...]) * cs_ref[...]).astype(o_ref.dtype)


def _in_proj(x2, g, w, colscale, *, tm, tn):
    t, d = x2.shape
    n = w.shape[1]
    return pl.pallas_call(
        _in_proj_kernel,
        grid=(t // tm, n // tn),
        in_specs=[pl.BlockSpec((tm, d), lambda i, j: (i, 0)),
                  pl.BlockSpec((1, d), lambda i, j: (0, 0)),
                  pl.BlockSpec((d, tn), lambda i, j: (0, j)),
                  pl.BlockSpec((1, tn), lambda i, j: (0, j))],
        out_specs=pl.BlockSpec((tm, tn), lambda i, j: (i, j)),
        out_shape=jax.ShapeDtypeStruct((t, n), BF16),
        scratch_shapes=[pltpu.VMEM((tm, d), BF16)],
        compiler_params=_cparams("parallel", "arbitrary"),
        name="in_proj",
    )(x2, g, w, colscale)


def _ret_kernel(q_ref, k_ref, v_ref, rg_ref, cos_ref, sin_ref, dm_ref, xi_ref, zt_ref, gc_ref,
                o_ref, st_ref):
    @pl.when(pl.program_id(1) == 0)
    def _():
        st_ref[...] = jnp.zeros_like(st_ref)

    cos = cos_ref[...]
    sin = sin_ref[...]
    for h in range(RET_HEADS):
        qs = slice(h * RET_DK, (h + 1) * RET_DK)
        vs = slice(h * RET_DV, (h + 1) * RET_DV)
        q = q_ref[:, qs].astype(F32)
        k = k_ref[:, qs].astype(F32)
        qr = q * cos + pltpu.roll(q, RET_DK // 2, 1) * sin
        kr = k * cos + pltpu.roll(k, RET_DK // 2, 1) * sin
        v = v_ref[:, vs]
        st = st_ref[h]
        inner = _dot_nt(qr.astype(BF16), kr.astype(BF16)) * dm_ref[h]
        out = _dot(inner.astype(BF16), v) + _dot((qr * xi_ref[h]).astype(BF16), st.astype(BF16))
        kz_t = (kr * zt_ref[h]).T.astype(BF16)
        st_ref[h] = gc_ref[h] * st + _dot(kz_t, v)
        rg = rg_ref[:, vs].astype(F32)
        o_ref[:, vs] = (_rms(out) * (rg * jax.nn.sigmoid(rg))).astype(o_ref.dtype)


def _retention(proj, tabs, *, batch, seq):
    t = proj.shape[0]
    c = RET_CHUNK
    nc = seq // c
    cos, sin, dmask, xi, zeta, gch = tabs
    row = lambda b, n: b * nc + n
    whole3 = lambda b, n: (0, 0, 0)
    return pl.pallas_call(
        _ret_kernel,
        grid=(batch, nc),
        in_specs=[pl.BlockSpec((c, RET_QK_W), lambda b, n: (row(b, n), COL_RQ // RET_QK_W)),
                  pl.BlockSpec((c, RET_QK_W), lambda b, n: (row(b, n), COL_RK // RET_QK_W)),
                  pl.BlockSpec((c, RET_V_W), lambda b, n: (row(b, n), COL_RV // RET_V_W)),
                  pl.BlockSpec((c, RET_V_W), lambda b, n: (row(b, n), COL_RG // RET_V_W)),
                  pl.BlockSpec((c, RET_DK), lambda b, n: (n, 0)),
                  pl.BlockSpec((c, RET_DK), lambda b, n: (n, 0)),
                  pl.BlockSpec((RET_HEADS, c, c), whole3),
                  pl.BlockSpec((RET_HEADS, c, RET_DK), whole3),
                  pl.BlockSpec((RET_HEADS, c, RET_DK), whole3),
                  pl.BlockSpec((RET_HEADS, 1, RET_DV), whole3)],
        out_specs=pl.BlockSpec((c, RET_V_W), lambda b, n: (row(b, n), 0)),
        out_shape=jax.ShapeDtypeStruct((t, RET_V_W), BF16),
        scratch_shapes=[pltpu.VMEM((RET_HEADS, RET_DK, RET_DV), F32)],
        compiler_params=_cparams("parallel", "arbitrary"),
        name="retention",
    )(proj, proj, proj, proj, cos, sin, dmask, xi, zeta, gch)


SUM_ROWS = 16


def _store_logits(s, s_ref, smax_ref):
    s_ref[...] = s
    smax_ref[...] = jnp.max(s, axis=0, keepdims=True)


def _softmax_update(s_ref, smax_ref, vt, m_ref, acc_ref):
    m = m_ref[...]
    m_new = jnp.maximum(m, smax_ref[...])
    alpha = jnp.exp2(m - m_new)
    p = jnp.exp2(s_ref[...] - m_new).astype(BF16)
    m_ref[...] = m_new
    vt_ones = jnp.concatenate([vt, jnp.ones((SUM_ROWS, vt.shape[1]), BF16)], axis=0)
    acc_ref[...] = alpha * acc_ref[...] + _dot(vt_ones, p)


def _softmax_result(acc_ref):
    dv = acc_ref.shape[0] - SUM_ROWS
    return acc_ref[:dv, :] / acc_ref[dv:dv + 1, :]


def _softmax_reset(m_ref, acc_ref):
    m_ref[...] = jnp.full(m_ref.shape, NEG_BIG, F32)
    acc_ref[...] = jnp.zeros(acc_ref.shape, F32)


def _softmax_scratch(group, dv, tk, n):
    return [pltpu.VMEM((2, group, tk, n), F32), pltpu.VMEM((2, group, 1, n), F32),
            pltpu.VMEM((group, 1, n), F32), pltpu.VMEM((group, dv + SUM_ROWS, n), F32)]


def _tile(i, size):
    return pl.ds(pl.multiple_of(i * size, size), size)


def _causal_tile_pipeline(nq, logits, softmax, finalize, split_diagonal):
    def step(qi, j, slot):
        last = j == qi
        nqi = jnp.where(last, qi + 1, qi)
        nj = jnp.where(last, 0, j + 1)
        nqi_valid = jnp.minimum(nqi, nq - 1)

        def issue(diagonal):
            logits(nqi_valid, nj, 1 - slot, diagonal)
            softmax(j, slot)

        if split_diagonal:
            lax.cond(nj == nqi_valid, lambda: issue(True), lambda: issue(False))
        else:
            issue(None)

        @pl.when(last)
        def _():
            finalize(qi)

        return nqi, nj

    n_pairs = nq * (nq + 1) // 2
    zero = jnp.int32(0)
    logits(zero, zero, 0, True)
    carry = lax.fori_loop(0, n_pairs // 2, lambda _, c: step(*step(*c, 0), 1), (zero, zero))
    if n_pairs % 2:
        step(*carry, 0)


def _diff_kernel(lam_ref, q_ref, k_ref, v_ref, tab_ref, g_ref, o_ref, vt_ref, s_ref, smax_ref, m_ref, acc_ref,
                 *, tq, group):
    w = 2 * DIFF_DH
    nq = vt_ref.shape[1]
    hslice = [slice(g * w, (g + 1) * w) for g in range(group)]
    state = [(m_ref.at[g], acc_ref.at[g]) for g in range(group)]
    for g in range(group):
        _softmax_reset(*state[g])
        for c in range(nq):
            vt_ref[g, c] = v_ref[c * tq:(c + 1) * tq, hslice[g]].astype(F32).T.astype(BF16)

    lane = lax.broadcasted_iota(jnp.int32, (1, w), 1)
    first_half = (lane < DIFF_DH).astype(BF16)
    second_half = (lane >= DIFF_DH).astype(BF16)

    def logits(qi, j, slot, diagonal):
        del diagonal
        idx = jnp.minimum(qi - j, 2)
        for g in range(group):
            q = q_ref[_tile(qi, tq), hslice[g]]
            q2 = jnp.concatenate([q * first_half, q * second_half], axis=0)
            bias = tab_ref[g, idx]
            s = _dot_nt(k_ref[_tile(j, tq), hslice[g]], q2) + jnp.concatenate([bias, bias], axis=1)
            _store_logits(s, s_ref.at[slot, g], smax_ref.at[slot, g])

    def softmax(j, slot):
        for g in range(group):
            _softmax_update(s_ref.at[slot, g], smax_ref.at[slot, g], vt_ref[g, j], *state[g])

    def finalize(qi):
        for g in range(group):
            o = _softmax_result(state[g][1])
            a = o[:, :tq] - lam_ref[0] * o[:, tq:]
            a = a * lax.rsqrt(jnp.mean(a * a, axis=0, keepdims=True) + EPS)
            o_ref[_tile(qi, tq), hslice[g]] = (a.T * g_ref[...]).astype(o_ref.dtype)
            _softmax_reset(*state[g])

    _causal_tile_pipeline(nq, logits, softmax, finalize, split_diagonal=False)


def _diff_attention(proj, lam, tab, g, *, batch, seq, tq, group):
    t = proj.shape[0]
    nq = seq // tq
    w = group * 2 * DIFF_DH
    return pl.pallas_call(
        functools.partial(_diff_kernel, tq=tq, group=group),
        grid=(batch, DIFF_HEADS // group),
        in_specs=[pl.BlockSpec(memory_space=pltpu.SMEM),
                  pl.BlockSpec((seq, w), lambda b, h: (b, COL_DQ // w + h)),
                  pl.BlockSpec((seq, w), lambda b, h: (b, COL_DK // w + h)),
                  pl.BlockSpec((seq, w), lambda b, h: (b, COL_DV // w + h)),
                  pl.BlockSpec((group, 3, tq, tq), lambda b, h: (h, 0, 0, 0)),
                  pl.BlockSpec((1, 2 * DIFF_DH), lambda b, h: (0, 0))],
        out_specs=pl.BlockSpec((seq, w), lambda b, h: (b, h)),
        out_shape=jax.ShapeDtypeStruct((t, DIFF_W), BF16),
        scratch_shapes=[pltpu.VMEM((group, nq, 2 * DIFF_DH, tq), BF16)]
        + _softmax_scratch(group, 2 * DIFF_DH, tq, 2 * tq),
        compiler_params=_cparams("parallel", "parallel"),
        name="diff_attention",
    )(lam, proj, proj, proj, tab, g)


def _mla_prep_kernel(cq_ref, ckv_ref, kpe_ref, gq_ref, gkv_ref, wq_ref, wk_ref, wvt_ref,
                     cq_tab, sq_tab, ck_tab, sk_tab, q_ref, k_ref, vt_ref):
    hq = (_rms(cq_ref[...].astype(F32)) * gq_ref[...]).astype(BF16)
    hkv = (_rms(ckv_ref[...].astype(F32)) * gkv_ref[...]).astype(BF16)
    qa = _dot(hq, wq_ref[...])
    ka = _dot(hkv, wk_ref[...])
    tk = vt_ref.shape[-1]
    for c in range(vt_ref.shape[0]):
        vt_ref[c] = _dot_nt(wvt_ref[...], hkv[c * tk:(c + 1) * tk]).astype(vt_ref.dtype)
    kpe = kpe_ref[...].astype(F32)
    kpe = kpe * ck_tab[...] + pltpu.roll(kpe, LANES - MLA_ROPE, 1) * sk_tab[...]
    cq = cq_tab[...]
    sq = sq_tab[...]
    for h in range(MLA_HEADS):
        hs = slice(h * LANES, (h + 1) * LANES)
        qh = qa[:, hs]
        q_ref[:, hs] = (qh * cq + pltpu.roll(qh, LANES - MLA_ROPE, 1) * sq).astype(q_ref.dtype)
        k_ref[:, hs] = (ka[:, hs] + kpe).astype(k_ref.dtype)


def _mla_prep(proj, gq, gkv, wq, wk, wvt, tabs, *, seq, tm, tk):
    t = proj.shape[0]
    ns = seq // tm
    nc = tm // tk
    const = lambda i: (0, 0)
    tab_spec = pl.BlockSpec((tm, LANES), lambda i: (i % ns, 0))
    return pl.pallas_call(
        _mla_prep_kernel,
        grid=(t // tm,),
        in_specs=[pl.BlockSpec((tm, MLA_Q_LORA), lambda i: (i, COL_CQ // MLA_Q_LORA)),
                  pl.BlockSpec((tm, MLA_KV_LORA), lambda i: (i, COL_CKV // MLA_KV_LORA)),
                  pl.BlockSpec((tm, LANES), lambda i: (i, COL_KPE // LANES)),
                  pl.BlockSpec((1, MLA_Q_LORA), const),
                  pl.BlockSpec((1, MLA_KV_LORA), const),
                  pl.BlockSpec((MLA_Q_LORA, MLA_PAD_W), const),
                  pl.BlockSpec((MLA_KV_LORA, MLA_PAD_W), const),
                  pl.BlockSpec((MLA_V_W, MLA_KV_LORA), const),
                  tab_spec, tab_spec, tab_spec, tab_spec],
        out_specs=[pl.BlockSpec((tm, MLA_PAD_W), lambda i: (i, 0)),
                   pl.BlockSpec((tm, MLA_PAD_W), lambda i: (i, 0)),
                   pl.BlockSpec((nc, MLA_V_W, tk), lambda i: (i, 0, 0))],
        out_shape=[jax.ShapeDtypeStruct((t, MLA_PAD_W), BF16),
                   jax.ShapeDtypeStruct((t, MLA_PAD_W), BF16),
                   jax.ShapeDtypeStruct((t // tk, MLA_V_W, tk), BF16)],
        compiler_params=_cparams("parallel"),
        name="mla_prep",
    )(proj, proj, proj, gq, gkv, wq, wk, wvt, *tabs)


def _mla_kernel(q_ref, k_ref, vt_ref, mask_ref, o_ref, s_ref, smax_ref, m_ref, acc_ref, *, tq, group):
    hslice = [slice(g * LANES, (g + 1) * LANES) for g in range(group)]
    vslice = [slice(g * MLA_V, (g + 1) * MLA_V) for g in range(group)]
    state = [(m_ref.at[g], acc_ref.at[g]) for g in range(group)]
    for g in range(group):
        _softmax_reset(*state[g])

    def logits(qi, j, slot, diagonal):
        for g in range(group):
            s = _dot_nt(k_ref[_tile(j, tq), hslice[g]], q_ref[_tile(qi, tq), hslice[g]])
            if diagonal:
                s = s + mask_ref[...]
            _store_logits(s, s_ref.at[slot, g], smax_ref.at[slot, g])

    def softmax(j, slot):
        for g in range(group):
            _softmax_update(s_ref.at[slot, g], smax_ref.at[slot, g], vt_ref[j, vslice[g], :], *state[g])

    def finalize(qi):
        o = jnp.concatenate([_softmax_result(acc) for _, acc in state], axis=0)
        o_ref[_tile(qi, tq), :] = o.T.astype(o_ref.dtype)
        for g in range(group):
            _softmax_reset(*state[g])

    _causal_tile_pipeline(vt_ref.shape[0], logits, softmax, finalize, split_diagonal=True)


def _mla_attention(qm, km, vt, mask, *, batch, seq, tq, group):
    t = qm.shape[0]
    nq = seq // tq
    return pl.pallas_call(
        functools.partial(_mla_kernel, tq=tq, group=group),
        grid=(batch, MLA_HEADS // group),
        in_specs=[pl.BlockSpec((seq, group * LANES), lambda b, h: (b, h)),
                  pl.BlockSpec((seq, group * LANES), lambda b, h: (b, h)),
                  pl.BlockSpec((nq, group * MLA_V, tq), lambda b, h: (b, h, 0)),
                  pl.BlockSpec((tq, tq), lambda b, h: (0, 0))],
        out_specs=pl.BlockSpec((seq, group * MLA_V), lambda b, h: (b, h)),
        out_shape=jax.ShapeDtypeStruct((t, MLA_V_W), BF16),
        scratch_shapes=_softmax_scratch(group, MLA_V, tq, tq),
        compiler_params=_cparams("parallel", "parallel"),
        name="mla_attention",
    )(qm, km, vt, mask)


def _merge_kernel(x_ref, a_ref, b_ref, c_ref, g0_ref, g1_ref, g2_ref, wa_ref, wb_ref, wc_ref, wo_ref,
                  pg_ref, o_ref):
    merged = jax.nn.sigmoid(g0_ref[...].astype(F32)) * _dot(a_ref[...], wa_ref[...])
    merged += jax.nn.sigmoid(g1_ref[...].astype(F32)) * _dot(b_ref[...], wb_ref[...])
    merged += jax.nn.sigmoid(g2_ref[...].astype(F32)) * _dot(c_ref[...], wc_ref[...])
    y = _dot(merged.astype(BF16), wo_ref[...])
    o_ref[...] = x_ref[...] + _rms(y) * pg_ref[...]


def _merge(x2, a, b, c, proj, wa, wb, wc, wo, pg, *, tm):
    t, d = x2.shape
    rowblk = pl.BlockSpec((tm, d), lambda i: (i, 0))
    wspec = pl.BlockSpec((d, d), lambda i: (0, 0))
    gate = lambda n: pl.BlockSpec((tm, d), lambda i: (i, COL_GATES // D_MODEL + n))
    return pl.pallas_call(
        _merge_kernel,
        grid=(t // tm,),
        in_specs=[rowblk, rowblk, rowblk, rowblk, gate(0), gate(1), gate(2),
                  wspec, wspec, wspec, wspec, pl.BlockSpec((1, d), lambda i: (0, 0))],
        out_specs=rowblk,
        out_shape=jax.ShapeDtypeStruct((t, d), F32),
        compiler_params=_cparams("parallel"),
        name="merge",
    )(x2, a, b, c, proj, proj, proj, wa, wb, wc, wo, pg)


def _mlp_kernel(x_ref, g_ref, wu_ref, wd_ref, pg_ref, o_ref, h_ref, acc_ref):
    k = pl.program_id(1)

    @pl.when(k == 0)
    def _():
        h_ref[...] = (_rms(x_ref[...]) * g_ref[...]).astype(BF16)
        acc_ref[...] = jnp.zeros_like(acc_ref)

    u = jnp.maximum(_dot(h_ref[...], wu_ref[...]), 0.0)
    acc_ref[...] += _dot((u * u).astype(BF16), wd_ref[...])

    @pl.when(k == pl.num_programs(1) - 1)
    def _():
        o_ref[...] = x_ref[...] + _rms(acc_ref[...]) * pg_ref[...]


def _mlp(x2, g, wu, wd, pg, *, tm, tf):
    t, d = x2.shape
    ff = wu.shape[1]
    vec = pl.BlockSpec((1, d), lambda i, k: (0, 0))
    return pl.pallas_call(
        _mlp_kernel,
        grid=(t // tm, ff // tf),
        in_specs=[pl.BlockSpec((tm, d), lambda i, k: (i, 0)), vec,
                  pl.BlockSpec((d, tf), lambda i, k: (0, k)),
                  pl.BlockSpec((tf, d), lambda i, k: (k, 0)), vec],
        out_specs=pl.BlockSpec((tm, d), lambda i, k: (i, 0)),
        out_shape=jax.ShapeDtypeStruct((t, d), F32),
        scratch_shapes=[pltpu.VMEM((tm, d), BF16), pltpu.VMEM((tm, d), F32)],
        compiler_params=_cparams("parallel", "arbitrary"),
        name="mlp",
    )(x2, g, wu, wd, pg)


def _rot_half_cols(w):
    half = w.shape[-1] // 2
    return jnp.concatenate([-w[..., half:], w[..., :half]], axis=-1)


def _prep_w_in(w_in_l):
    offs = np.cumsum(np.array(IN_SPLITS))[:-1].tolist()
    rq, rk, rv, rg, dq, dk, dv, cq, ckv, kpe, gates = jnp.split(w_in_l, offs, axis=-1)
    zeros = jnp.zeros((w_in_l.shape[0], MLA_NOPE), w_in_l.dtype)
    kpe_blk = jnp.concatenate([zeros, kpe, _rot_half_cols(kpe)], axis=-1)
    return jnp.concatenate([gates, rv, rg, dq, dk, dv, rq, rk, cq, ckv, kpe_blk], axis=-1).astype(BF16)


def _in_colscale():
    cs = np.ones((1, N_IN), np.float32)
    cs[0, COL_DQ:COL_DQ + DIFF_W] = DIFF_DH ** -0.5 * LOG2E
    cs[0, COL_RK:COL_RK + RET_QK_W] = RET_DK ** -0.5
    return jnp.asarray(cs)


def _prep_w_uq(w):
    w = w.reshape(MLA_Q_LORA, MLA_HEADS, MLA_QH)
    pe = w[..., MLA_NOPE:]
    return jnp.concatenate([w, _rot_half_cols(pe)], axis=-1).reshape(MLA_Q_LORA, MLA_PAD_W).astype(BF16)


def _prep_w_ukv(w):
    w = w.reshape(MLA_KV_LORA, MLA_HEADS, MLA_KVH)
    wk = jnp.concatenate([w[..., :MLA_NOPE], jnp.zeros_like(w[..., :MLA_NOPE])], axis=-1)
    return (wk.reshape(MLA_KV_LORA, MLA_PAD_W).astype(BF16),
            w[..., MLA_NOPE:].reshape(MLA_KV_LORA, MLA_V_W).T.astype(BF16))


def _ret_tables(seq):
    pos = jnp.arange(seq, dtype=F32)
    inv_freq = 1.0 / (10000.0 ** jnp.linspace(0.0, 1.0, RET_DK // 2))
    ang = pos[:, None] * inv_freq[None, :]
    cos, sin = jnp.cos(ang), jnp.sin(ang)
    cos_t = jnp.concatenate([cos, cos], axis=-1)
    sin_t = jnp.concatenate([-sin, sin], axis=-1)
    log_g = jnp.log1p(-jnp.exp2(-5.0 - jnp.arange(RET_HEADS, dtype=F32)))
    idx = jnp.arange(RET_CHUNK, dtype=F32)
    rel = idx[:, None] - idx[None, :]
    dmask = jnp.where(rel >= 0, jnp.exp(jnp.maximum(rel, 0.0)[None] * log_g[:, None, None]), 0.0)
    xi = jnp.exp((idx + 1.0)[None, :] * log_g[:, None])[:, :, None]
    zeta = jnp.exp((RET_CHUNK - 1.0 - idx)[None, :] * log_g[:, None])[:, :, None]
    g_chunk = jnp.exp(RET_CHUNK * log_g)[:, None, None]
    return (cos_t, sin_t, dmask,
            jnp.broadcast_to(xi, (RET_HEADS, RET_CHUNK, RET_DK)),
            jnp.broadcast_to(zeta, (RET_HEADS, RET_CHUNK, RET_DK)),
            jnp.broadcast_to(g_chunk, (RET_HEADS, 1, RET_DV)))


def _mla_tables(seq):
    pos = jnp.arange(seq, dtype=F32)
    inv_freq = 1.0 / (ROPE_THETA ** (jnp.arange(0, MLA_ROPE, 2, dtype=F32) / MLA_ROPE))
    ang = pos[:, None] * inv_freq[None, :]
    cos, sin = jnp.cos(ang), jnp.sin(ang)
    z_nope = jnp.zeros((seq, MLA_NOPE), F32)
    z_rope = jnp.zeros((seq, MLA_ROPE), F32)
    cos_k = jnp.concatenate([z_nope, cos, cos, z_rope], axis=-1)
    sin_k = jnp.concatenate([z_nope, sin, sin, z_rope], axis=-1)
    qscale = MLA_QH ** -0.5 * LOG2E
    cos_q = jnp.concatenate([jnp.ones((seq, MLA_NOPE), F32), cos, cos, z_rope], axis=-1) * qscale
    sin_q = sin_k * qscale
    return cos_q, sin_q, cos_k, sin_k


def _t5_bucket(n):
    max_exact = REL_BUCKETS // 2
    nf = jnp.maximum(n, 1).astype(F32)
    large = max_exact + (jnp.log(nf / max_exact) / math.log(REL_MAX_DIST / max_exact)
                         * (REL_BUCKETS - max_exact)).astype(jnp.int32)
    large = jnp.minimum(large, REL_BUCKETS - 1)
    return jnp.where(n < max_exact, n, large)


def _bias_table(rel_bias, tq):
    assert tq >= REL_MAX_DIST
    kk = jnp.arange(tq)[:, None]
    qq = jnp.arange(tq)[None, :]
    dist = jnp.stack([qq - kk, qq - kk + tq, jnp.full((tq, tq), 2 * tq)])
    onehot = jax.nn.one_hot(_t5_bucket(jnp.maximum(dist, 0)), REL_BUCKETS, dtype=F32)
    bias = jnp.einsum("cijb,bh->hcij", onehot, rel_bias.astype(F32),
                      precision=lax.Precision.HIGHEST) * LOG2E
    return jnp.where((dist >= 0)[None], bias, NEG_BIG)


def _causal_table(tq):
    kk = jnp.arange(tq)[:, None]
    qq = jnp.arange(tq)[None, :]
    return jnp.where(qq >= kk, 0.0, NEG_BIG).astype(F32)


def _pick(t, want):
    return want if t % want == 0 else t


def kernel(x, rel_bias, pre_mix_g, w_in, w_ret_o, lambda_q1, lambda_k1, lambda_q2, lambda_k2, diff_subln_g, w_diff_o, mla_q_norm_g, w_mla_uq, mla_kv_norm_g, w_mla_ukv, w_mla_o, w_out, post_mix_g, pre_mlp_g, w_up, w_down, post_mlp_g):
    batch, seq, d = x.shape
    t = batch * seq
    depth = w_in.shape[0]
    tq = _pick(seq, 256)
    ret_tabs = _ret_tables(seq)
    mla_tabs = _mla_tables(seq)
    bias_tab = _bias_table(rel_bias, tq)
    causal_tab = _causal_table(tq)
    colscale = _in_colscale()
    row = lambda v: v.reshape(1, -1).astype(F32)

    x2 = x.reshape(t, d)
    for l in range(depth):
        proj = _in_proj(x2, row(pre_mix_g[l]), _prep_w_in(w_in[l]), colscale,
                        tm=_pick(t, 2048), tn=512)

        ret = _retention(proj, ret_tabs, batch=batch, seq=seq)

        lambda_init = 0.8 - 0.6 * math.exp(-0.3 * l)
        lam = (jnp.exp(jnp.sum(lambda_q1[l] * lambda_k1[l]).astype(F32))
               - jnp.exp(jnp.sum(lambda_q2[l] * lambda_k2[l]).astype(F32)) + lambda_init)
        da = _diff_attention(proj, lam.reshape(1), bias_tab, row(diff_subln_g[l]) * (1.0 - lambda_init),
                             batch=batch, seq=seq, tq=tq, group=4)

        wk, wvt = _prep_w_ukv(w_mla_ukv[l])
        qm, km, vt = _mla_prep(proj, row(mla_q_norm_g[l]), row(mla_kv_norm_g[l]),
                               _prep_w_uq(w_mla_uq[l]), wk, wvt, mla_tabs, seq=seq, tm=_pick(seq, 512), tk=tq)
        mo = _mla_attention(qm, km, vt, causal_tab, batch=batch, seq=seq, tq=tq, group=8)

        x2 = _merge(x2, ret, da, mo, proj, w_ret_o[l].astype(BF16), w_diff_o[l].astype(BF16),
                    w_mla_o[l].astype(BF16), w_out[l].astype(BF16), row(post_mix_g[l]), tm=_pick(t, 256))

        x2 = _mlp(x2, row(pre_mlp_g[l]), w_up[l].astype(BF16), w_down[l].astype(BF16),
                  row(post_mlp_g[l]), tm=_pick(t, 1024), tf=1024)
    return x2.reshape(batch, seq, d)
```

```python
import functools
import math

import jax
import jax.numpy as jnp
import numpy as np
from jax import lax
from jax.experimental import pallas as pl
from jax.experimental.pallas import tpu as pltpu

F32 = jnp.float32
BF16 = jnp.bfloat16

D_MODEL = 1024
EPS = 1e-6
RET_HEADS = 4
RET_DK = 128
RET_DV = 256
RET_CHUNK = 128
DIFF_HEADS = 8
DIFF_DH = 64
MLA_HEADS = 16
MLA_Q_LORA = 256
MLA_KV_LORA = 128
MLA_NOPE = 64
MLA_ROPE = 32
MLA_V = 64
ROPE_THETA = 10000.0
REL_BUCKETS = 32
REL_MAX_DIST = 128
D_FF = 4 * D_MODEL
N_BRANCH = 3

RET_QK_W = RET_HEADS * RET_DK
RET_V_W = RET_HEADS * RET_DV
DIFF_W = DIFF_HEADS * 2 * DIFF_DH
MLA_QH = MLA_NOPE + MLA_ROPE
MLA_KVH = MLA_NOPE + MLA_V
MLA_V_W = MLA_HEADS * MLA_V
MLA_PAD_W = MLA_HEADS * 128
IN_SPLITS = (RET_QK_W, RET_QK_W, RET_V_W, RET_V_W, DIFF_W, DIFF_W, DIFF_W,
             MLA_Q_LORA, MLA_KV_LORA, MLA_ROPE, N_BRANCH * D_MODEL)

LANES = 128
LOG2E = math.log2(math.e)
NEG_BIG = -1e30
VMEM_LIMIT = 56 * 1024 * 1024

COL_GATES = 0
COL_RV = COL_GATES + N_BRANCH * D_MODEL
COL_RG = COL_RV + RET_V_W
COL_DQ = COL_RG + RET_V_W
COL_DK = COL_DQ + DIFF_W
COL_DV = COL_DK + DIFF_W
COL_RQ = COL_DV + DIFF_W
COL_RK = COL_RQ + RET_QK_W
COL_CQ = COL_RK + RET_QK_W
COL_CKV = COL_CQ + MLA_Q_LORA
COL_KPE = COL_CKV + MLA_KV_LORA
N_IN = COL_KPE + LANES


def _cparams(*sem):
    return pltpu.CompilerParams(dimension_semantics=sem, vmem_limit_bytes=VMEM_LIMIT)


def _rms(xf):
    return xf * lax.rsqrt(jnp.mean(xf * xf, axis=-1, keepdims=True) + EPS)


def _dot(a, b):
    return jnp.dot(a, b, preferred_element_type=F32)


def _dot_nt(a, b):
    return lax.dot_general(a, b, (((1,), (1,)), ((), ())), preferred_element_type=F32)


def _in_proj_kernel(x_ref, g_ref, w_ref, cs_ref, o_ref, *, tn):
    h = (_rms(x_ref[...]) * g_ref[...]).astype(BF16)
    for c in range(w_ref.shape[1] // tn):
        cols = slice(c * tn, (c + 1) * tn)
        o_ref[:, cols] = (_dot(h, w_ref[:, cols]) * cs_ref[:, cols]).astype(o_ref.dtype)


def _resident(shape):
    return pl.BlockSpec(shape, lambda *_: (0,) * len(shape), pipeline_mode=pl.Buffered(1))


def _in_proj(x2, g, w, colscale, *, tm, tn):
    t, d = x2.shape
    n = w.shape[1]
    return pl.pallas_call(
        functools.partial(_in_proj_kernel, tn=tn),
        grid=(t // tm,),
        in_specs=[pl.BlockSpec((tm, d), lambda i: (i, 0)),
                  _resident((1, d)), _resident((d, n)), _resident((1, n))],
        out_specs=pl.BlockSpec((tm, n), lambda i: (i, 0)),
        out_shape=jax.ShapeDtypeStruct((t, n), BF16),
        compiler_params=_cparams("parallel"),
        name="in_proj",
    )(x2, g, w, colscale)


def _ret_kernel(q_ref, k_ref, v_ref, rg_ref, cos_ref, sin_ref, dm_ref, xi_ref, zt_ref, gc_ref,
                o_ref, st_ref):
    @pl.when(pl.program_id(1) == 0)
    def _():
        st_ref[...] = jnp.zeros_like(st_ref)

    cos = cos_ref[...]
    sin = sin_ref[...]
    for h in range(RET_HEADS):
        qs = slice(h * RET_DK, (h + 1) * RET_DK)
        vs = slice(h * RET_DV, (h + 1) * RET_DV)
        q = q_ref[:, qs].astype(F32)
        k = k_ref[:, qs].astype(F32)
        qr = q * cos + pltpu.roll(q, RET_DK // 2, 1) * sin
        kr = k * cos + pltpu.roll(k, RET_DK // 2, 1) * sin
        v = v_ref[:, vs]
        st = st_ref[h]
        inner = _dot_nt(qr.astype(BF16), kr.astype(BF16)) * dm_ref[h]
        out = _dot(inner.astype(BF16), v) + _dot((qr * xi_ref[h]).astype(BF16), st.astype(BF16))
        kz_t = (kr * zt_ref[h]).T.astype(BF16)
        st_ref[h] = gc_ref[h] * st + _dot(kz_t, v)
        rg = rg_ref[:, vs].astype(F32)
        o_ref[:, vs] = (_rms(out) * (rg * jax.nn.sigmoid(rg))).astype(o_ref.dtype)


def _retention(proj, tabs, *, batch, seq):
    t = proj.shape[0]
    c = RET_CHUNK
    nc = seq // c
    cos, sin, dmask, xi, zeta, gch = tabs
    row = lambda b, n: b * nc + n
    whole3 = lambda b, n: (0, 0, 0)
    return pl.pallas_call(
        _ret_kernel,
        grid=(batch, nc),
        in_specs=[pl.BlockSpec((c, RET_QK_W), lambda b, n: (row(b, n), COL_RQ // RET_QK_W)),
                  pl.BlockSpec((c, RET_QK_W), lambda b, n: (row(b, n), COL_RK // RET_QK_W)),
                  pl.BlockSpec((c, RET_V_W), lambda b, n: (row(b, n), COL_RV // RET_V_W)),
                  pl.BlockSpec((c, RET_V_W), lambda b, n: (row(b, n), COL_RG // RET_V_W)),
                  pl.BlockSpec((c, RET_DK), lambda b, n: (n, 0)),
                  pl.BlockSpec((c, RET_DK), lambda b, n: (n, 0)),
                  pl.BlockSpec((RET_HEADS, c, c), whole3),
                  pl.BlockSpec((RET_HEADS, c, RET_DK), whole3),
                  pl.BlockSpec((RET_HEADS, c, RET_DK), whole3),
                  pl.BlockSpec((RET_HEADS, 1, RET_DV), whole3)],
        out_specs=pl.BlockSpec((c, RET_V_W), lambda b, n: (row(b, n), 0)),
        out_shape=jax.ShapeDtypeStruct((t, RET_V_W), BF16),
        scratch_shapes=[pltpu.VMEM((RET_HEADS, RET_DK, RET_DV), F32)],
        compiler_params=_cparams("parallel", "arbitrary"),
        name="retention",
    )(proj, proj, proj, proj, cos, sin, dmask, xi, zeta, gch)


SUM_ROWS = 16


def _store_logits(s, s_ref, smax_ref):
    s_ref[...] = s
    smax_ref[...] = jnp.max(s, axis=0, keepdims=True)


def _softmax_update(s_ref, smax_ref, vt, m_ref, acc_ref):
    m = m_ref[...]
    m_new = jnp.maximum(m, smax_ref[...])
    alpha = jnp.exp2(m - m_new)
    p = jnp.exp2(s_ref[...] - m_new).astype(BF16)
    m_ref[...] = m_new
    vt_ones = jnp.concatenate([vt, jnp.ones((SUM_ROWS, vt.shape[1]), BF16)], axis=0)
    acc_ref[...] = alpha * acc_ref[...] + _dot(vt_ones, p)


def _softmax_result(acc_ref):
    dv = acc_ref.shape[0] - SUM_ROWS
    return acc_ref[:dv, :] / acc_ref[dv:dv + 1, :]


def _softmax_reset(m_ref, acc_ref):
    m_ref[...] = jnp.full(m_ref.shape, NEG_BIG, F32)
    acc_ref[...] = jnp.zeros(acc_ref.shape, F32)


def _softmax_scratch(group, dv, tk, n):
    return [pltpu.VMEM((2, group, tk, n), F32), pltpu.VMEM((2, group, 1, n), F32),
            pltpu.VMEM((group, 1, n), F32), pltpu.VMEM((group, dv + SUM_ROWS, n), F32)]


def _tile(i, size):
    return pl.ds(pl.multiple_of(i * size, size), size)


def _causal_tile_pipeline(nq, group, logits, softmax, finalize, near_tiles):
    def step(qi, j, slot):
        last = j == qi
        nqi = jnp.where(last, qi + 1, qi)
        nj = jnp.where(last, 0, j + 1)
        nqi_valid = jnp.minimum(nqi, nq - 1)

        def issue(near):
            for g in range(group):
                softmax(g, j, slot)
                logits(g, nqi_valid, nj, 1 - slot, near)

        lax.cond(nqi_valid - nj < near_tiles, lambda: issue(True), lambda: issue(False))

        @pl.when(last)
        def _():
            finalize(qi)

        return nqi, nj

    n_pairs = nq * (nq + 1) // 2
    zero = jnp.int32(0)
    for g in range(group):
        logits(g, zero, zero, 0, True)
    carry = lax.fori_loop(0, n_pairs // 2, lambda _, c: step(*step(*c, 0), 1), (zero, zero))
    if n_pairs % 2:
        step(*carry, 0)


def _diff_kernel(lam_ref, q_ref, k_ref, v_ref, tab_ref, g_ref, o_ref, vt_ref, s_ref, smax_ref, m_ref, acc_ref,
                 *, tq, group):
    w = 2 * DIFF_DH
    nq = vt_ref.shape[1]
    hslice = [slice(g * w, (g + 1) * w) for g in range(group)]
    state = [(m_ref.at[g], acc_ref.at[g]) for g in range(group)]
    for g in range(group):
        _softmax_reset(*state[g])
        for c in range(nq):
            vt_ref[g, c] = v_ref[c * tq:(c + 1) * tq, hslice[g]].astype(F32).T.astype(BF16)

    lane = lax.broadcasted_iota(jnp.int32, (1, w), 1)
    first_half = (lane < DIFF_DH).astype(BF16)
    second_half = (lane >= DIFF_DH).astype(BF16)

    def logits(g, qi, j, slot, near):
        q = q_ref[_tile(qi, tq), hslice[g]]
        q2 = jnp.concatenate([q * first_half, q * second_half], axis=0)
        s = _dot_nt(k_ref[_tile(j, tq), hslice[g]], q2)
        if near:
            bias = tab_ref[g, qi - j]
            s = s + jnp.concatenate([bias, bias], axis=1)
        _store_logits(s, s_ref.at[slot, g], smax_ref.at[slot, g])

    def softmax(g, j, slot):
        _softmax_update(s_ref.at[slot, g], smax_ref.at[slot, g], vt_ref[g, j], *state[g])

    def finalize(qi):
        for g in range(group):
            o = _softmax_result(state[g][1])
            a = o[:, :tq] - lam_ref[0] * o[:, tq:]
            a = a * lax.rsqrt(jnp.mean(a * a, axis=0, keepdims=True) + EPS)
            o_ref[_tile(qi, tq), hslice[g]] = (a.T * g_ref[...]).astype(o_ref.dtype)
            _softmax_reset(*state[g])

    _causal_tile_pipeline(nq, group, logits, softmax, finalize, near_tiles=tab_ref.shape[1])


def _diff_attention(proj, lam, tab, g, *, batch, seq, tq, group):
    t = proj.shape[0]
    nq = seq // tq
    w = group * 2 * DIFF_DH
    return pl.pallas_call(
        functools.partial(_diff_kernel, tq=tq, group=group),
        grid=(batch, DIFF_HEADS // group),
        in_specs=[pl.BlockSpec(memory_space=pltpu.SMEM),
                  pl.BlockSpec((seq, w), lambda b, h: (b, COL_DQ // w + h)),
                  pl.BlockSpec((seq, w), lambda b, h: (b, COL_DK // w + h)),
                  pl.BlockSpec((seq, w), lambda b, h: (b, COL_DV // w + h)),
                  pl.BlockSpec((group, 2, tq, tq), lambda b, h: (h, 0, 0, 0)),
                  pl.BlockSpec((1, 2 * DIFF_DH), lambda b, h: (0, 0))],
        out_specs=pl.BlockSpec((seq, w), lambda b, h: (b, h)),
        out_shape=jax.ShapeDtypeStruct((t, DIFF_W), BF16),
        scratch_shapes=[pltpu.VMEM((group, nq, 2 * DIFF_DH, tq), BF16)]
        + _softmax_scratch(group, 2 * DIFF_DH, tq, 2 * tq),
        compiler_params=_cparams("parallel", "parallel"),
        name="diff_attention",
    )(lam, proj, proj, proj, tab, g)


def _mla_prep_kernel(cq_ref, ckv_ref, kpe_ref, gq_ref, gkv_ref, wq_ref, wk_ref, wvt_ref,
                     cq_tab, sq_tab, ck_tab, sk_tab, q_ref, k_ref, vt_ref):
    hq = (_rms(cq_ref[...].astype(F32)) * gq_ref[...]).astype(BF16)
    hkv = (_rms(ckv_ref[...].astype(F32)) * gkv_ref[...]).astype(BF16)
    qa = _dot(hq, wq_ref[...])
    ka = _dot(hkv, wk_ref[...])
    tk = vt_ref.shape[-1]
    for c in range(vt_ref.shape[0]):
        vt_ref[c] = _dot_nt(wvt_ref[...], hkv[c * tk:(c + 1) * tk]).astype(vt_ref.dtype)
    kpe = kpe_ref[...].astype(F32)
    kpe = kpe * ck_tab[...] + pltpu.roll(kpe, LANES - MLA_ROPE, 1) * sk_tab[...]
    cq = cq_tab[...]
    sq = sq_tab[...]
    for h in range(MLA_HEADS):
        hs = slice(h * LANES, (h + 1) * LANES)
        qh = qa[:, hs]
        q_ref[:, hs] = (qh * cq + pltpu.roll(qh, LANES - MLA_ROPE, 1) * sq).astype(q_ref.dtype)
        k_ref[:, hs] = (ka[:, hs] + kpe).astype(k_ref.dtype)


def _mla_prep(proj, gq, gkv, wq, wk, wvt, tabs, *, seq, tm, tk):
    t = proj.shape[0]
    ns = seq // tm
    nc = tm // tk
    const = lambda i: (0, 0)
    tab_spec = pl.BlockSpec((tm, LANES), lambda i: (i % ns, 0))
    return pl.pallas_call(
        _mla_prep_kernel,
        grid=(t // tm,),
        in_specs=[pl.BlockSpec((tm, MLA_Q_LORA), lambda i: (i, COL_CQ // MLA_Q_LORA)),
                  pl.BlockSpec((tm, MLA_KV_LORA), lambda i: (i, COL_CKV // MLA_KV_LORA)),
                  pl.BlockSpec((tm, LANES), lambda i: (i, COL_KPE // LANES)),
                  pl.BlockSpec((1, MLA_Q_LORA), const),
                  pl.BlockSpec((1, MLA_KV_LORA), const),
                  pl.BlockSpec((MLA_Q_LORA, MLA_PAD_W), const),
                  pl.BlockSpec((MLA_KV_LORA, MLA_PAD_W), const),
                  pl.BlockSpec((MLA_V_W, MLA_KV_LORA), const),
                  tab_spec, tab_spec, tab_spec, tab_spec],
        out_specs=[pl.BlockSpec((tm, MLA_PAD_W), lambda i: (i, 0)),
                   pl.BlockSpec((tm, MLA_PAD_W), lambda i: (i, 0)),
                   pl.BlockSpec((nc, MLA_V_W, tk), lambda i: (i, 0, 0))],
        out_shape=[jax.ShapeDtypeStruct((t, MLA_PAD_W), BF16),
                   jax.ShapeDtypeStruct((t, MLA_PAD_W), BF16),
                   jax.ShapeDtypeStruct((t // tk, MLA_V_W, tk), BF16)],
        compiler_params=_cparams("parallel"),
        name="mla_prep",
    )(proj, proj, proj, gq, gkv, wq, wk, wvt, *tabs)


def _mla_kernel(q_ref, k_ref, vt_ref, mask_ref, o_ref, s_ref, smax_ref, m_ref, acc_ref, *, tq, group):
    hslice = [slice(g * LANES, (g + 1) * LANES) for g in range(group)]
    vslice = [slice(g * MLA_V, (g + 1) * MLA_V) for g in range(group)]
    state = [(m_ref.at[g], acc_ref.at[g]) for g in range(group)]
    for g in range(group):
        _softmax_reset(*state[g])

    def logits(g, qi, j, slot, near):
        s = _dot_nt(k_ref[_tile(j, tq), hslice[g]], q_ref[_tile(qi, tq), hslice[g]])
        if near:
            s = s + mask_ref[...]
        _store_logits(s, s_ref.at[slot, g], smax_ref.at[slot, g])

    def softmax(g, j, slot):
        _softmax_update(s_ref.at[slot, g], smax_ref.at[slot, g], vt_ref[j, vslice[g], :], *state[g])

    def finalize(qi):
        o = jnp.concatenate([_softmax_result(acc) for _, acc in state], axis=0)
        o_ref[_tile(qi, tq), :] = o.T.astype(o_ref.dtype)
        for g in range(group):
            _softmax_reset(*state[g])

    _causal_tile_pipeline(vt_ref.shape[0], group, logits, softmax, finalize, near_tiles=1)


def _mla_attention(qm, km, vt, mask, *, batch, seq, tq, group):
    t = qm.shape[0]
    nq = seq // tq
    return pl.pallas_call(
        functools.partial(_mla_kernel, tq=tq, group=group),
        grid=(batch, MLA_HEADS // group),
        in_specs=[pl.BlockSpec((seq, group * LANES), lambda b, h: (b, h)),
                  pl.BlockSpec((seq, group * LANES), lambda b, h: (b, h)),
                  pl.BlockSpec((nq, group * MLA_V, tq), lambda b, h: (b, h, 0)),
                  pl.BlockSpec((tq, tq), lambda b, h: (0, 0))],
        out_specs=pl.BlockSpec((seq, group * MLA_V), lambda b, h: (b, h)),
        out_shape=jax.ShapeDtypeStruct((t, MLA_V_W), BF16),
        scratch_shapes=_softmax_scratch(group, MLA_V, tq, tq),
        compiler_params=_cparams("parallel", "parallel"),
        name="mla_attention",
    )(qm, km, vt, mask)


def _merge_kernel(x_ref, a_ref, b_ref, c_ref, g0_ref, g1_ref, g2_ref, wa_ref, wb_ref, wc_ref, wo_ref,
                  pg_ref, o_ref):
    merged = jax.nn.sigmoid(g0_ref[...].astype(F32)) * _dot(a_ref[...], wa_ref[...])
    merged += jax.nn.sigmoid(g1_ref[...].astype(F32)) * _dot(b_ref[...], wb_ref[...])
    merged += jax.nn.sigmoid(g2_ref[...].astype(F32)) * _dot(c_ref[...], wc_ref[...])
    y = _dot(merged.astype(BF16), wo_ref[...])
    o_ref[...] = x_ref[...] + _rms(y) * pg_ref[...]


def _merge(x2, a, b, c, proj, wa, wb, wc, wo, pg, *, tm):
    t, d = x2.shape
    rowblk = pl.BlockSpec((tm, d), lambda i: (i, 0))
    wspec = _resident((d, d))
    gate = lambda n: pl.BlockSpec((tm, d), lambda i: (i, COL_GATES // D_MODEL + n))
    return pl.pallas_call(
        _merge_kernel,
        grid=(t // tm,),
        in_specs=[rowblk, rowblk, rowblk, rowblk, gate(0), gate(1), gate(2),
                  wspec, wspec, wspec, wspec, _resident((1, d))],
        out_specs=rowblk,
        out_shape=jax.ShapeDtypeStruct((t, d), F32),
        compiler_params=_cparams("parallel"),
        name="merge",
    )(x2, a, b, c, proj, proj, proj, wa, wb, wc, wo, pg)


def _mlp_kernel(x_ref, g_ref, wu_ref, wd_ref, pg_ref, o_ref, *, tf):
    x = x_ref[...]
    h = (_rms(x) * g_ref[...]).astype(BF16)
    acc = None
    for c in range(wu_ref.shape[1] // tf):
        u = jnp.maximum(_dot(h, wu_ref[:, c * tf:(c + 1) * tf]), 0.0)
        part = _dot((u * u).astype(BF16), wd_ref[c * tf:(c + 1) * tf, :])
        acc = part if acc is None else acc + part
    o_ref[...] = x + _rms(acc) * pg_ref[...]


def _mlp(x2, g, wu, wd, pg, *, tm, tf):
    t, d = x2.shape
    ff = wu.shape[1]
    rowblk = pl.BlockSpec((tm, d), lambda i: (i, 0))
    return pl.pallas_call(
        functools.partial(_mlp_kernel, tf=tf),
        grid=(t // tm,),
        in_specs=[rowblk, _resident((1, d)), _resident((d, ff)), _resident((ff, d)), _resident((1, d))],
        out_specs=rowblk,
        out_shape=jax.ShapeDtypeStruct((t, d), F32),
        compiler_params=_cparams("parallel"),
        name="mlp",
    )(x2, g, wu, wd, pg)


def _rot_half_cols(w):
    half = w.shape[-1] // 2
    return jnp.concatenate([-w[..., half:], w[..., :half]], axis=-1)


def _prep_w_in(w_in_l):
    offs = np.cumsum(np.array(IN_SPLITS))[:-1].tolist()
    rq, rk, rv, rg, dq, dk, dv, cq, ckv, kpe, gates = jnp.split(w_in_l, offs, axis=-1)
    zeros = jnp.zeros((w_in_l.shape[0], MLA_NOPE), w_in_l.dtype)
    kpe_blk = jnp.concatenate([zeros, kpe, _rot_half_cols(kpe)], axis=-1)
    return jnp.concatenate([gates, rv, rg, dq, dk, dv, rq, rk, cq, ckv, kpe_blk], axis=-1).astype(BF16)


def _in_colscale():
    cs = np.ones((1, N_IN), np.float32)
    cs[0, COL_DQ:COL_DQ + DIFF_W] = DIFF_DH ** -0.5 * LOG2E
    cs[0, COL_RK:COL_RK + RET_QK_W] = RET_DK ** -0.5
    return jnp.asarray(cs)


def _prep_w_uq(w):
    w = w.reshape(MLA_Q_LORA, MLA_HEADS, MLA_QH)
    pe = w[..., MLA_NOPE:]
    return jnp.concatenate([w, _rot_half_cols(pe)], axis=-1).reshape(MLA_Q_LORA, MLA_PAD_W).astype(BF16)


def _prep_w_ukv(w):
    w = w.reshape(MLA_KV_LORA, MLA_HEADS, MLA_KVH)
    wk = jnp.concatenate([w[..., :MLA_NOPE], jnp.zeros_like(w[..., :MLA_NOPE])], axis=-1)
    return (wk.reshape(MLA_KV_LORA, MLA_PAD_W).astype(BF16),
            w[..., MLA_NOPE:].reshape(MLA_KV_LORA, MLA_V_W).T.astype(BF16))


def _ret_tables(seq):
    pos = np.arange(seq, dtype=np.float64)
    inv_freq = 1.0 / (10000.0 ** np.linspace(0.0, 1.0, RET_DK // 2))
    ang = pos[:, None] * inv_freq[None, :]
    cos, sin = np.cos(ang), np.sin(ang)
    cos_t = np.concatenate([cos, cos], axis=-1)
    sin_t = np.concatenate([-sin, sin], axis=-1)
    log_g = np.log1p(-np.exp2(-5.0 - np.arange(RET_HEADS, dtype=np.float64)))
    idx = np.arange(RET_CHUNK, dtype=np.float64)
    rel = idx[:, None] - idx[None, :]
    dmask = np.where(rel >= 0, np.exp(np.maximum(rel, 0.0)[None] * log_g[:, None, None]), 0.0)
    xi = np.exp((idx + 1.0)[None, :] * log_g[:, None])[:, :, None]
    zeta = np.exp((RET_CHUNK - 1.0 - idx)[None, :] * log_g[:, None])[:, :, None]
    g_chunk = np.exp(RET_CHUNK * log_g)[:, None, None]
    tabs = (cos_t, sin_t, dmask,
            np.broadcast_to(xi, (RET_HEADS, RET_CHUNK, RET_DK)),
            np.broadcast_to(zeta, (RET_HEADS, RET_CHUNK, RET_DK)),
            np.broadcast_to(g_chunk, (RET_HEADS, 1, RET_DV)))
    return tuple(jnp.asarray(t, F32) for t in tabs)


def _mla_tables(seq):
    pos = np.arange(seq, dtype=np.float64)
    inv_freq = 1.0 / (ROPE_THETA ** (np.arange(0, MLA_ROPE, 2, dtype=np.float64) / MLA_ROPE))
    ang = pos[:, None] * inv_freq[None, :]
    cos, sin = np.cos(ang), np.sin(ang)
    z_nope = np.zeros((seq, MLA_NOPE))
    z_rope = np.zeros((seq, MLA_ROPE))
    cos_k = np.concatenate([z_nope, cos, cos, z_rope], axis=-1)
    sin_k = np.concatenate([z_nope, sin, sin, z_rope], axis=-1)
    qscale = MLA_QH ** -0.5 * LOG2E
    cos_q = np.concatenate([np.ones((seq, MLA_NOPE)), cos, cos, z_rope], axis=-1) * qscale
    sin_q = sin_k * qscale
    return tuple(jnp.asarray(t, F32) for t in (cos_q, sin_q, cos_k, sin_k))


def _t5_bucket(n):
    max_exact = REL_BUCKETS // 2
    nf = jnp.maximum(n, 1).astype(F32)
    large = max_exact + (jnp.log(nf / max_exact) / math.log(REL_MAX_DIST / max_exact)
                         * (REL_BUCKETS - max_exact)).astype(jnp.int32)
    large = jnp.minimum(large, REL_BUCKETS - 1)
    return jnp.where(n < max_exact, n, large)


def _toeplitz(g, tq):
    h = g.shape[0]
    padded = jnp.concatenate([g, jnp.zeros((h, 1), g.dtype)], axis=1)
    skewed = jnp.tile(padded, (1, tq))[:, :tq * (2 * tq - 1)].reshape(h, tq, 2 * tq - 1)
    return skewed[:, :, tq - 1:]


def _bias_table(rel_bias, tq):
    assert tq >= REL_MAX_DIST
    bias = rel_bias[_t5_bucket(jnp.arange(2 * tq))].astype(F32).T
    bias = (bias - rel_bias[REL_BUCKETS - 1].astype(F32)[:, None]) * LOG2E
    masked = jnp.full((bias.shape[0], tq - 1), NEG_BIG, F32)
    diagonal = jnp.concatenate([masked, bias[:, :tq]], axis=1)
    behind = bias[:, 1:]
    return jnp.stack([_toeplitz(diagonal, tq), _toeplitz(behind, tq)], axis=1)


def _causal_table(tq):
    kk = np.arange(tq)[:, None]
    qq = np.arange(tq)[None, :]
    return jnp.asarray(np.where(qq >= kk, 0.0, NEG_BIG), F32)


def _pick(t, want):
    return want if t % want == 0 else t


def kernel(x, rel_bias, pre_mix_g, w_in, w_ret_o, lambda_q1, lambda_k1, lambda_q2, lambda_k2, diff_subln_g, w_diff_o, mla_q_norm_g, w_mla_uq, mla_kv_norm_g, w_mla_ukv, w_mla_o, w_out, post_mix_g, pre_mlp_g, w_up, w_down, post_mlp_g):
    batch, seq, d = x.shape
    t = batch * seq
    depth = w_in.shape[0]
    tq = _pick(seq, 256)
    ret_tabs = _ret_tables(seq)
    mla_tabs = _mla_tables(seq)
    bias_tab = _bias_table(rel_bias, tq)
    causal_tab = _causal_table(tq)
    colscale = _in_colscale()
    row = lambda v: v.reshape(1, -1).astype(F32)

    x2 = x.reshape(t, d)
    for l in range(depth):
        proj = _in_proj(x2, row(pre_mix_g[l]), _prep_w_in(w_in[l]), colscale,
                        tm=_pick(t, 512), tn=512)

        ret = _retention(proj, ret_tabs, batch=batch, seq=seq)

        lambda_init = 0.8 - 0.6 * math.exp(-0.3 * l)
        lam = (jnp.exp(jnp.sum(lambda_q1[l] * lambda_k1[l]).astype(F32))
               - jnp.exp(jnp.sum(lambda_q2[l] * lambda_k2[l]).astype(F32)) + lambda_init)
        da = _diff_attention(proj, lam.reshape(1), bias_tab, row(diff_subln_g[l]) * (1.0 - lambda_init),
                             batch=batch, seq=seq, tq=tq, group=4)

        wk, wvt = _prep_w_ukv(w_mla_ukv[l])
        qm, km, vt = _mla_prep(proj, row(mla_q_norm_g[l]), row(mla_kv_norm_g[l]),
                               _prep_w_uq(w_mla_uq[l]), wk, wvt, mla_tabs, seq=seq, tm=_pick(seq, 512), tk=tq)
        mo = _mla_attention(qm, km, vt, causal_tab, batch=batch, seq=seq, tq=tq, group=8)

        x2 = _merge(x2, ret, da, mo, proj, w_ret_o[l].astype(BF16), w_diff_o[l].astype(BF16),
                    w_mla_o[l].astype(BF16), w_out[l].astype(BF16), row(post_mix_g[l]), tm=_pick(t, 512))

        x2 = _mlp(x2, row(pre_mlp_g[l]), w_up[l].astype(BF16), w_down[l].astype(BF16),
                  row(post_mlp_g[l]), tm=_pick(t, 1024), tf=1024)
    return x2.reshape(batch, seq, d)
```

```python
import functools
import math

import jax
import jax.numpy as jnp
import numpy as np
from jax import lax
from jax.experimental import pallas as pl
from jax.experimental.pallas import tpu as pltpu

F32 = jnp.float32
BF16 = jnp.bfloat16

D_MODEL = 1024
EPS = 1e-6
RET_HEADS = 4
RET_DK = 128
RET_DV = 256
RET_CHUNK = 128
DIFF_HEADS = 8
DIFF_DH = 64
MLA_HEADS = 16
MLA_Q_LORA = 256
MLA_KV_LORA = 128
MLA_NOPE = 64
MLA_ROPE = 32
MLA_V = 64
ROPE_THETA = 10000.0
REL_BUCKETS = 32
REL_MAX_DIST = 128
D_FF = 4 * D_MODEL
N_BRANCH = 3

RET_QK_W = RET_HEADS * RET_DK
RET_V_W = RET_HEADS * RET_DV
DIFF_W = DIFF_HEADS * 2 * DIFF_DH
MLA_QH = MLA_NOPE + MLA_ROPE
MLA_KVH = MLA_NOPE + MLA_V
MLA_V_W = MLA_HEADS * MLA_V
MLA_PAD_W = MLA_HEADS * 128
IN_SPLITS = (RET_QK_W, RET_QK_W, RET_V_W, RET_V_W, DIFF_W, DIFF_W, DIFF_W,
             MLA_Q_LORA, MLA_KV_LORA, MLA_ROPE, N_BRANCH * D_MODEL)

LANES = 128
LOG2E = math.log2(math.e)
NEG_BIG = -1e30
VMEM_LIMIT = 56 * 1024 * 1024

COL_RQ = 0
COL_RK = COL_RQ + RET_QK_W
COL_RV = COL_RK + RET_QK_W
COL_RG = COL_RV + RET_V_W
COL_DQ = COL_RG + RET_V_W
COL_DK = COL_DQ + DIFF_W
COL_DV = COL_DK + DIFF_W
COL_GATES = COL_DV + DIFF_W
COL_CQ = COL_GATES + N_BRANCH * D_MODEL
COL_CKV = COL_CQ + MLA_Q_LORA
COL_KPE = COL_CKV + MLA_KV_LORA
N_IN = COL_KPE + LANES
N_IN_MAIN = COL_GATES
N_IN_SMALL = N_IN - COL_CQ


def _cparams(*sem):
    return pltpu.CompilerParams(dimension_semantics=sem, vmem_limit_bytes=VMEM_LIMIT)


def _rms(xf):
    return xf * lax.rsqrt(jnp.mean(xf * xf, axis=-1, keepdims=True) + EPS)


def _dot(a, b):
    return jnp.dot(a, b, preferred_element_type=F32)


def _dot_nt(a, b):
    return lax.dot_general(a, b, (((1,), (1,)), ((), ())), preferred_element_type=F32)


def _in_proj_kernel(x_ref, g_ref, wm_ref, wg_ref, ws_ref, cs_ref, o_ref, *, tn):
    h = (_rms(x_ref[...]) * g_ref[...]).astype(BF16)
    base = 0
    for w_ref in (wm_ref, wg_ref, ws_ref):
        for c in range(w_ref.shape[1] // tn):
            cols = slice(base + c * tn, base + (c + 1) * tn)
            acc = _dot(h, w_ref[:, c * tn:(c + 1) * tn])
            o_ref[:, cols] = (acc * cs_ref[:, cols]).astype(o_ref.dtype)
        base += w_ref.shape[1]


def _resident(shape):
    return pl.BlockSpec(shape, lambda *_: (0,) * len(shape), pipeline_mode=pl.Buffered(1))


def _layer_slab(w, layer):
    return pl.BlockSpec((None,) + w.shape[1:], lambda *_: (layer, 0, 0), pipeline_mode=pl.Buffered(1))


def _in_proj(x2, g, weights, layer, colscale, *, tm, tn):
    t, d = x2.shape
    n = sum(w.shape[2] for w in weights)
    assert all(w.shape[2] % tn == 0 for w in weights)
    return pl.pallas_call(
        functools.partial(_in_proj_kernel, tn=tn),
        grid=(t // tm,),
        in_specs=[pl.BlockSpec((tm, d), lambda i: (i, 0)), _resident((1, d))]
        + [_layer_slab(w, layer) for w in weights] + [_resident((1, n))],
        out_specs=pl.BlockSpec((tm, n), lambda i: (i, 0)),
        out_shape=jax.ShapeDtypeStruct((t, n), BF16),
        compiler_params=_cparams("parallel"),
        name="in_proj",
    )(x2, g, *weights, colscale)


def _ret_kernel(q_ref, k_ref, v_ref, rg_ref, cos_ref, sin_ref, dm_ref, xi_ref, zt_ref, gc_ref,
                o_ref, st_ref):
    @pl.when(pl.program_id(1) == 0)
    def _():
        st_ref[...] = jnp.zeros_like(st_ref)

    cos = cos_ref[...]
    sin = sin_ref[...]
    for h in range(RET_HEADS):
        qs = slice(h * RET_DK, (h + 1) * RET_DK)
        vs = slice(h * RET_DV, (h + 1) * RET_DV)
        q = q_ref[:, qs].astype(F32)
        k = k_ref[:, qs].astype(F32)
        qr = q * cos + pltpu.roll(q, RET_DK // 2, 1) * sin
        kr = k * cos + pltpu.roll(k, RET_DK // 2, 1) * sin
        v = v_ref[:, vs]
        st = st_ref[h]
        inner = _dot_nt(qr.astype(BF16), kr.astype(BF16)) * dm_ref[h]
        out = _dot(inner.astype(BF16), v) + _dot((qr * xi_ref[h]).astype(BF16), st.astype(BF16))
        kz_t = (kr * zt_ref[h]).T.astype(BF16)
        st_ref[h] = gc_ref[h] * st + _dot(kz_t, v)
        rg = rg_ref[:, vs].astype(F32)
        o_ref[:, vs] = (_rms(out) * (rg * jax.nn.sigmoid(rg))).astype(o_ref.dtype)


def _retention(proj, tabs, *, batch, seq):
    t = proj.shape[0]
    c = RET_CHUNK
    nc = seq // c
    cos, sin, dmask, xi, zeta, gch = tabs
    row = lambda b, n: b * nc + n
    whole3 = lambda b, n: (0, 0, 0)
    return pl.pallas_call(
        _ret_kernel,
        grid=(batch, nc),
        in_specs=[pl.BlockSpec((c, RET_QK_W), lambda b, n: (row(b, n), COL_RQ // RET_QK_W)),
                  pl.BlockSpec((c, RET_QK_W), lambda b, n: (row(b, n), COL_RK // RET_QK_W)),
                  pl.BlockSpec((c, RET_V_W), lambda b, n: (row(b, n), COL_RV // RET_V_W)),
                  pl.BlockSpec((c, RET_V_W), lambda b, n: (row(b, n), COL_RG // RET_V_W)),
                  pl.BlockSpec((c, RET_DK), lambda b, n: (n, 0)),
                  pl.BlockSpec((c, RET_DK), lambda b, n: (n, 0)),
                  pl.BlockSpec((RET_HEADS, c, c), whole3),
                  pl.BlockSpec((RET_HEADS, c, RET_DK), whole3),
                  pl.BlockSpec((RET_HEADS, c, RET_DK), whole3),
                  pl.BlockSpec((RET_HEADS, 1, RET_DV), whole3)],
        out_specs=pl.BlockSpec((c, RET_V_W), lambda b, n: (row(b, n), 0)),
        out_shape=jax.ShapeDtypeStruct((t, RET_V_W), BF16),
        scratch_shapes=[pltpu.VMEM((RET_HEADS, RET_DK, RET_DV), F32)],
        compiler_params=_cparams("parallel", "arbitrary"),
        name="retention",
    )(proj, proj, proj, proj, cos, sin, dmask, xi, zeta, gch)


SUM_ROWS = 16


def _store_logits(s, s_ref, smax_ref):
    s_ref[...] = s
    smax_ref[...] = jnp.max(s, axis=0, keepdims=True)


def _softmax_update(s_ref, smax_ref, vt, m_ref, acc_ref):
    m = m_ref[...]
    m_new = jnp.maximum(m, smax_ref[...])
    alpha = jnp.exp2(m - m_new)
    p = jnp.exp2(s_ref[...] - m_new).astype(BF16)
    m_ref[...] = m_new
    vt_ones = jnp.concatenate([vt, jnp.ones((SUM_ROWS, vt.shape[1]), BF16)], axis=0)
    acc_ref[...] = alpha * acc_ref[...] + _dot(vt_ones, p)


def _softmax_result(acc_ref):
    dv = acc_ref.shape[0] - SUM_ROWS
    return acc_ref[:dv, :] / acc_ref[dv:dv + 1, :]


def _softmax_reset(m_ref, acc_ref):
    m_ref[...] = jnp.full(m_ref.shape, NEG_BIG, F32)
    acc_ref[...] = jnp.zeros(acc_ref.shape, F32)


def _softmax_scratch(group, dv, tk, n):
    return [pltpu.VMEM((2, group, tk, n), F32), pltpu.VMEM((2, group, 1, n), F32),
            pltpu.VMEM((group, 1, n), F32), pltpu.VMEM((group, dv + SUM_ROWS, n), F32)]


def _tile(i, size):
    return pl.ds(pl.multiple_of(i * size, size), size)


def _causal_tile_pipeline(nq, group, logits, softmax, finalize, near_tiles, interleave):
    def step(qi, j, slot):
        last = j == qi
        nqi = jnp.where(last, qi + 1, qi)
        nj = jnp.where(last, 0, j + 1)
        nqi_valid = jnp.minimum(nqi, nq - 1)

        def issue(near):
            if interleave:
                for g in range(group):
                    softmax(g, j, slot)
                    logits(g, nqi_valid, nj, 1 - slot, near)
            else:
                for g in range(group):
                    logits(g, nqi_valid, nj, 1 - slot, near)
                for g in range(group):
                    softmax(g, j, slot)

        lax.cond(nqi_valid - nj < near_tiles, lambda: issue(True), lambda: issue(False))

        @pl.when(last)
        def _():
            finalize(qi)

        return nqi, nj

    n_pairs = nq * (nq + 1) // 2
    zero = jnp.int32(0)
    for g in range(group):
        logits(g, zero, zero, 0, True)
    carry = lax.fori_loop(0, n_pairs // 2, lambda _, c: step(*step(*c, 0), 1), (zero, zero))
    if n_pairs % 2:
        step(*carry, 0)


def _diff_kernel(lam_ref, q_ref, k_ref, v_ref, tab_ref, g_ref, o_ref, vt_ref, s_ref, smax_ref, m_ref, acc_ref,
                 *, tq, group):
    w = 2 * DIFF_DH
    nq = vt_ref.shape[1]
    hslice = [slice(g * w, (g + 1) * w) for g in range(group)]
    state = [(m_ref.at[g], acc_ref.at[g]) for g in range(group)]
    for g in range(group):
        _softmax_reset(*state[g])
        for c in range(nq):
            vt_ref[g, c] = v_ref[c * tq:(c + 1) * tq, hslice[g]].astype(F32).T.astype(BF16)

    lane = lax.broadcasted_iota(jnp.int32, (1, w), 1)
    first_half = (lane < DIFF_DH).astype(BF16)
    second_half = (lane >= DIFF_DH).astype(BF16)

    def logits(g, qi, j, slot, near):
        q = q_ref[_tile(qi, tq), hslice[g]]
        q2 = jnp.concatenate([q * first_half, q * second_half], axis=0)
        s = _dot_nt(k_ref[_tile(j, tq), hslice[g]], q2)
        if near:
            bias = tab_ref[g, qi - j]
            s = s + jnp.concatenate([bias, bias], axis=1)
        _store_logits(s, s_ref.at[slot, g], smax_ref.at[slot, g])

    def softmax(g, j, slot):
        _softmax_update(s_ref.at[slot, g], smax_ref.at[slot, g], vt_ref[g, j], *state[g])

    def finalize(qi):
        for g in range(group):
            o = _softmax_result(state[g][1])
            a = o[:, :tq] - lam_ref[0] * o[:, tq:]
            a = a * lax.rsqrt(jnp.mean(a * a, axis=0, keepdims=True) + EPS)
            o_ref[_tile(qi, tq), hslice[g]] = (a.T * g_ref[...]).astype(o_ref.dtype)
            _softmax_reset(*state[g])

    _causal_tile_pipeline(nq, group, logits, softmax, finalize, near_tiles=tab_ref.shape[1],
                          interleave=True)


def _diff_attention(proj, lam, tab, g, *, batch, seq, tq, group):
    t = proj.shape[0]
    nq = seq // tq
    w = group * 2 * DIFF_DH
    return pl.pallas_call(
        functools.partial(_diff_kernel, tq=tq, group=group),
        grid=(batch, DIFF_HEADS // group),
        in_specs=[pl.BlockSpec(memory_space=pltpu.SMEM),
                  pl.BlockSpec((seq, w), lambda b, h: (b, COL_DQ // w + h)),
                  pl.BlockSpec((seq, w), lambda b, h: (b, COL_DK // w + h)),
                  pl.BlockSpec((seq, w), lambda b, h: (b, COL_DV // w + h)),
                  pl.BlockSpec((group, 2, tq, tq), lambda b, h: (h, 0, 0, 0)),
                  pl.BlockSpec((1, 2 * DIFF_DH), lambda b, h: (0, 0))],
        out_specs=pl.BlockSpec((seq, w), lambda b, h: (b, h)),
        out_shape=jax.ShapeDtypeStruct((t, DIFF_W), BF16),
        scratch_shapes=[pltpu.VMEM((group, nq, 2 * DIFF_DH, tq), BF16)]
        + _softmax_scratch(group, 2 * DIFF_DH, tq, 2 * tq),
        compiler_params=_cparams("parallel", "parallel"),
        name="diff_attention",
    )(lam, proj, proj, proj, tab, g)


def _mla_prep_kernel(cq_ref, ckv_ref, kpe_ref, gq_ref, gkv_ref, wq_ref, wk_ref, wvt_ref,
                     cq_tab, sq_tab, ck_tab, sk_tab, qt_ref, k_ref, vt_ref):
    hq = (_rms(cq_ref[...].astype(F32)) * gq_ref[...]).astype(BF16)
    hkv = (_rms(ckv_ref[...].astype(F32)) * gkv_ref[...]).astype(BF16)
    qa = _dot(hq, wq_ref[...])
    ka = _dot(hkv, wk_ref[...])
    tk = vt_ref.shape[-1]
    for c in range(vt_ref.shape[0]):
        vt_ref[c] = _dot_nt(wvt_ref[...], hkv[c * tk:(c + 1) * tk]).astype(vt_ref.dtype)
    kpe = kpe_ref[...].astype(F32)
    kpe = kpe * ck_tab[...] + pltpu.roll(kpe, LANES - MLA_ROPE, 1) * sk_tab[...]
    cq = cq_tab[...]
    sq = sq_tab[...]
    for h in range(MLA_HEADS):
        hs = slice(h * LANES, (h + 1) * LANES)
        qh = qa[:, hs]
        qh = qh * cq + pltpu.roll(qh, LANES - MLA_ROPE, 1) * sq
        for c in range(qt_ref.shape[0]):
            qt_ref[c, hs, :] = qh[c * tk:(c + 1) * tk].T.astype(qt_ref.dtype)
        k_ref[:, hs] = (ka[:, hs] + kpe).astype(k_ref.dtype)


def _mla_prep(proj, gq, gkv, wq, wk, wvt, tabs, *, seq, tm, tk):
    t = proj.shape[0]
    ns = seq // tm
    nc = tm // tk
    const = lambda i: (0, 0)
    tab_spec = pl.BlockSpec((tm, LANES), lambda i: (i % ns, 0))
    return pl.pallas_call(
        _mla_prep_kernel,
        grid=(t // tm,),
        in_specs=[pl.BlockSpec((tm, MLA_Q_LORA), lambda i: (i, COL_CQ // MLA_Q_LORA)),
                  pl.BlockSpec((tm, MLA_KV_LORA), lambda i: (i, COL_CKV // MLA_KV_LORA)),
                  pl.BlockSpec((tm, LANES), lambda i: (i, COL_KPE // LANES)),
                  pl.BlockSpec((1, MLA_Q_LORA), const),
                  pl.BlockSpec((1, MLA_KV_LORA), const),
                  pl.BlockSpec((MLA_Q_LORA, MLA_PAD_W), const),
                  pl.BlockSpec((MLA_KV_LORA, MLA_PAD_W), const),
                  pl.BlockSpec((MLA_V_W, MLA_KV_LORA), const),
                  tab_spec, tab_spec, tab_spec, tab_spec],
        out_specs=[pl.BlockSpec((nc, MLA_PAD_W, tk), lambda i: (i, 0, 0)),
                   pl.BlockSpec((tm, MLA_PAD_W), lambda i: (i, 0)),
                   pl.BlockSpec((nc, MLA_V_W, tk), lambda i: (i, 0, 0))],
        out_shape=[jax.ShapeDtypeStruct((t // tk, MLA_PAD_W, tk), BF16),
                   jax.ShapeDtypeStruct((t, MLA_PAD_W), BF16),
                   jax.ShapeDtypeStruct((t // tk, MLA_V_W, tk), BF16)],
        compiler_params=_cparams("parallel"),
        name="mla_prep",
    )(proj, proj, proj, gq, gkv, wq, wk, wvt, *tabs)


def _mla_kernel(qt_ref, k_ref, vt_ref, mask_ref, o_ref, s_ref, smax_ref, m_ref, acc_ref, *, tq, group):
    hslice = [slice(g * LANES, (g + 1) * LANES) for g in range(group)]
    vslice = [slice(g * MLA_V, (g + 1) * MLA_V) for g in range(group)]
    state = [(m_ref.at[g], acc_ref.at[g]) for g in range(group)]
    for g in range(group):
        _softmax_reset(*state[g])

    def logits(g, qi, j, slot, near):
        s = _dot(k_ref[_tile(j, tq), hslice[g]], qt_ref[qi, hslice[g], :])
        if near:
            s = s + mask_ref[...]
        _store_logits(s, s_ref.at[slot, g], smax_ref.at[slot, g])

    def softmax(g, j, slot):
        _softmax_update(s_ref.at[slot, g], smax_ref.at[slot, g], vt_ref[j, vslice[g], :], *state[g])

    def finalize(qi):
        o = jnp.concatenate([_softmax_result(acc) for _, acc in state], axis=0)
        o_ref[_tile(qi, tq), :] = o.T.astype(o_ref.dtype)
        for g in range(group):
            _softmax_reset(*state[g])

    _causal_tile_pipeline(vt_ref.shape[0], group, logits, softmax, finalize, near_tiles=1,
                          interleave=False)


def _mla_attention(qt, km, vt, mask, *, batch, seq, tq, group):
    t = km.shape[0]
    nq = seq // tq
    return pl.pallas_call(
        functools.partial(_mla_kernel, tq=tq, group=group),
        grid=(batch, MLA_HEADS // group),
        in_specs=[pl.BlockSpec((nq, group * LANES, tq), lambda b, h: (b, h, 0)),
                  pl.BlockSpec((seq, group * LANES), lambda b, h: (b, h)),
                  pl.BlockSpec((nq, group * MLA_V, tq), lambda b, h: (b, h, 0)),
                  pl.BlockSpec((tq, tq), lambda b, h: (0, 0))],
        out_specs=pl.BlockSpec((seq, group * MLA_V), lambda b, h: (b, h)),
        out_shape=jax.ShapeDtypeStruct((t, MLA_V_W), BF16),
        scratch_shapes=_softmax_scratch(group, MLA_V, tq, tq),
        compiler_params=_cparams("parallel", "parallel"),
        name="mla_attention",
    )(qt, km, vt, mask)


def _merge_kernel(x_ref, a_ref, b_ref, c_ref, g0_ref, g1_ref, g2_ref, wa_ref, wb_ref, wc_ref, wo_ref,
                  pg_ref, o_ref):
    merged = jax.nn.sigmoid(g0_ref[...].astype(F32)) * _dot(a_ref[...], wa_ref[...])
    merged += jax.nn.sigmoid(g1_ref[...].astype(F32)) * _dot(b_ref[...], wb_ref[...])
    merged += jax.nn.sigmoid(g2_ref[...].astype(F32)) * _dot(c_ref[...], wc_ref[...])
    y = _dot(merged.astype(BF16), wo_ref[...])
    o_ref[...] = x_ref[...] + _rms(y) * pg_ref[...]


def _merge(x2, a, b, c, proj, wa, wb, wc, wo, layer, pg, *, tm):
    t, d = x2.shape
    rowblk = pl.BlockSpec((tm, d), lambda i: (i, 0))
    gate = lambda n: pl.BlockSpec((tm, d), lambda i: (i, COL_GATES // D_MODEL + n))
    return pl.pallas_call(
        _merge_kernel,
        grid=(t // tm,),
        in_specs=[rowblk, rowblk, rowblk, rowblk, gate(0), gate(1), gate(2)]
        + [_layer_slab(w, layer) for w in (wa, wb, wc, wo)] + [_resident((1, d))],
        out_specs=rowblk,
        out_shape=jax.ShapeDtypeStruct((t, d), F32),
        compiler_params=_cparams("parallel"),
        name="merge",
    )(x2, a, b, c, proj, proj, proj, wa, wb, wc, wo, pg)


def _mlp_kernel(x_ref, g_ref, wu_ref, wd_ref, pg_ref, o_ref, *, tf):
    x = x_ref[...]
    h = (_rms(x) * g_ref[...]).astype(BF16)
    acc = None
    for c in range(wu_ref.shape[1] // tf):
        u = jnp.maximum(_dot(h, wu_ref[:, c * tf:(c + 1) * tf]), 0.0)
        part = _dot((u * u).astype(BF16), wd_ref[c * tf:(c + 1) * tf, :])
        acc = part if acc is None else acc + part
    o_ref[...] = x + _rms(acc) * pg_ref[...]


def _mlp(x2, g, wu, wd, layer, pg, *, tm, tf):
    t, d = x2.shape
    rowblk = pl.BlockSpec((tm, d), lambda i: (i, 0))
    return pl.pallas_call(
        functools.partial(_mlp_kernel, tf=tf),
        grid=(t // tm,),
        in_specs=[rowblk, _resident((1, d)), _layer_slab(wu, layer), _layer_slab(wd, layer), _resident((1, d))],
        out_specs=rowblk,
        out_shape=jax.ShapeDtypeStruct((t, d), F32),
        compiler_params=_cparams("parallel"),
        name="mlp",
    )(x2, g, wu, wd, pg)


def _rot_half_cols(w):
    half = w.shape[-1] // 2
    return jnp.concatenate([-w[..., half:], w[..., :half]], axis=-1)


def _w_in_prep_kernel(w_ref, wm_ref, wg_ref, ws_ref):
    kpe_at = N_IN_MAIN + MLA_Q_LORA + MLA_KV_LORA
    gates_at = kpe_at + MLA_ROPE
    wm_ref[0] = w_ref[0, :, :N_IN_MAIN].astype(BF16)
    wg_ref[0] = w_ref[0, :, gates_at:].astype(BF16)
    ws_ref[0, :, :kpe_at - N_IN_MAIN] = w_ref[0, :, N_IN_MAIN:kpe_at].astype(BF16)
    blk = w_ref[0, :, kpe_at:kpe_at + LANES]
    lane = lax.broadcasted_iota(jnp.int32, blk.shape, 1)
    half = MLA_ROPE // 2
    kpe = pltpu.roll(blk, MLA_NOPE, 1)
    neg_hi = -pltpu.roll(blk, MLA_NOPE + MLA_ROPE - half, 1)
    lo = pltpu.roll(blk, MLA_NOPE + MLA_ROPE + half, 1)
    out = jnp.where(lane < MLA_NOPE, 0.0,
                    jnp.where(lane < MLA_NOPE + MLA_ROPE, kpe,
                              jnp.where(lane < MLA_NOPE + MLA_ROPE + half, neg_hi, lo)))
    ws_ref[0, :, kpe_at - N_IN_MAIN:] = out.astype(BF16)


def _prep_w_in(w_in, *, tr):
    depth, d, n = w_in.shape
    n_gates = N_BRANCH * D_MODEL
    return pl.pallas_call(
        _w_in_prep_kernel,
        grid=(depth, d // tr),
        in_specs=[pl.BlockSpec((1, tr, n), lambda l, r: (l, r, 0))],
        out_specs=[pl.BlockSpec((1, tr, N_IN_MAIN), lambda l, r: (l, r, 0)),
                   pl.BlockSpec((1, tr, n_gates), lambda l, r: (l, r, 0)),
                   pl.BlockSpec((1, tr, N_IN_SMALL), lambda l, r: (l, r, 0))],
        out_shape=[jax.ShapeDtypeStruct((depth, d, N_IN_MAIN), BF16),
                   jax.ShapeDtypeStruct((depth, d, n_gates), BF16),
                   jax.ShapeDtypeStruct((depth, d, N_IN_SMALL), BF16)],
        compiler_params=_cparams("parallel", "parallel"),
        name="w_in_prep",
    )(w_in)


def _in_colscale():
    cs = np.ones((1, N_IN), np.float32)
    cs[0, COL_DQ:COL_DQ + DIFF_W] = DIFF_DH ** -0.5 * LOG2E
    cs[0, COL_RK:COL_RK + RET_QK_W] = RET_DK ** -0.5
    return jnp.asarray(cs)


def _prep_w_uq(w):
    w = w.reshape(MLA_Q_LORA, MLA_HEADS, MLA_QH)
    pe = w[..., MLA_NOPE:]
    return jnp.concatenate([w, _rot_half_cols(pe)], axis=-1).reshape(MLA_Q_LORA, MLA_PAD_W).astype(BF16)


def _prep_w_ukv(w):
    w = w.reshape(MLA_KV_LORA, MLA_HEADS, MLA_KVH)
    wk = jnp.concatenate([w[..., :MLA_NOPE], jnp.zeros_like(w[..., :MLA_NOPE])], axis=-1)
    return (wk.reshape(MLA_KV_LORA, MLA_PAD_W).astype(BF16),
            w[..., MLA_NOPE:].reshape(MLA_KV_LORA, MLA_V_W).T.astype(BF16))


def _ret_tables(seq):
    pos = np.arange(seq, dtype=np.float64)
    inv_freq = 1.0 / (10000.0 ** np.linspace(0.0, 1.0, RET_DK // 2))
    ang = pos[:, None] * inv_freq[None, :]
    cos, sin = np.cos(ang), np.sin(ang)
    cos_t = np.concatenate([cos, cos], axis=-1)
    sin_t = np.concatenate([-sin, sin], axis=-1)
    log_g = np.log1p(-np.exp2(-5.0 - np.arange(RET_HEADS, dtype=np.float64)))
    idx = np.arange(RET_CHUNK, dtype=np.float64)
    rel = idx[:, None] - idx[None, :]
    dmask = np.where(rel >= 0, np.exp(np.maximum(rel, 0.0)[None] * log_g[:, None, None]), 0.0)
    xi = np.exp((idx + 1.0)[None, :] * log_g[:, None])[:, :, None]
    zeta = np.exp((RET_CHUNK - 1.0 - idx)[None, :] * log_g[:, None])[:, :, None]
    g_chunk = np.exp(RET_CHUNK * log_g)[:, None, None]
    tabs = (cos_t, sin_t, dmask,
            np.broadcast_to(xi, (RET_HEADS, RET_CHUNK, RET_DK)),
            np.broadcast_to(zeta, (RET_HEADS, RET_CHUNK, RET_DK)),
            np.broadcast_to(g_chunk, (RET_HEADS, 1, RET_DV)))
    return tuple(jnp.asarray(t, F32) for t in tabs)


def _mla_tables(seq):
    pos = np.arange(seq, dtype=np.float64)
    inv_freq = 1.0 / (ROPE_THETA ** (np.arange(0, MLA_ROPE, 2, dtype=np.float64) / MLA_ROPE))
    ang = pos[:, None] * inv_freq[None, :]
    cos, sin = np.cos(ang), np.sin(ang)
    z_nope = np.zeros((seq, MLA_NOPE))
    z_rope = np.zeros((seq, MLA_ROPE))
    cos_k = np.concatenate([z_nope, cos, cos, z_rope], axis=-1)
    sin_k = np.concatenate([z_nope, sin, sin, z_rope], axis=-1)
    qscale = MLA_QH ** -0.5 * LOG2E
    cos_q = np.concatenate([np.ones((seq, MLA_NOPE)), cos, cos, z_rope], axis=-1) * qscale
    sin_q = sin_k * qscale
    return tuple(jnp.asarray(t, F32) for t in (cos_q, sin_q, cos_k, sin_k))


def _t5_bucket(n):
    max_exact = REL_BUCKETS // 2
    nf = jnp.maximum(n, 1).astype(F32)
    large = max_exact + (jnp.log(nf / max_exact) / math.log(REL_MAX_DIST / max_exact)
                         * (REL_BUCKETS - max_exact)).astype(jnp.int32)
    large = jnp.minimum(large, REL_BUCKETS - 1)
    return jnp.where(n < max_exact, n, large)


def _toeplitz(g, tq):
    h = g.shape[0]
    padded = jnp.concatenate([g, jnp.zeros((h, 1), g.dtype)], axis=1)
    skewed = jnp.tile(padded, (1, tq))[:, :tq * (2 * tq - 1)].reshape(h, tq, 2 * tq - 1)
    return skewed[:, :, tq - 1:]


def _bias_table(rel_bias, tq):
    assert tq >= REL_MAX_DIST
    bias = rel_bias[_t5_bucket(jnp.arange(2 * tq))].astype(F32).T
    bias = (bias - rel_bias[REL_BUCKETS - 1].astype(F32)[:, None]) * LOG2E
    masked = jnp.full((bias.shape[0], tq - 1), NEG_BIG, F32)
    diagonal = jnp.concatenate([masked, bias[:, :tq]], axis=1)
    behind = bias[:, 1:]
    return jnp.stack([_toeplitz(diagonal, tq), _toeplitz(behind, tq)], axis=1)


def _causal_table(tq):
    kk = np.arange(tq)[:, None]
    qq = np.arange(tq)[None, :]
    return jnp.asarray(np.where(qq >= kk, 0.0, NEG_BIG), F32)


def _pick(t, want):
    return want if t % want == 0 else t


def kernel(x, rel_bias, pre_mix_g, w_in, w_ret_o, lambda_q1, lambda_k1, lambda_q2, lambda_k2, diff_subln_g, w_diff_o, mla_q_norm_g, w_mla_uq, mla_kv_norm_g, w_mla_ukv, w_mla_o, w_out, post_mix_g, pre_mlp_g, w_up, w_down, post_mlp_g):
    batch, seq, d = x.shape
    t = batch * seq
    depth = w_in.shape[0]
    tq = _pick(seq, 256)
    ret_tabs = _ret_tables(seq)
    mla_tabs = _mla_tables(seq)
    bias_tab = _bias_table(rel_bias, tq)
    causal_tab = _causal_table(tq)
    colscale = _in_colscale()
    w_in_pieces = _prep_w_in(w_in, tr=_pick(d, 256))
    merge_w = tuple(w.astype(BF16) for w in (w_ret_o, w_diff_o, w_mla_o, w_out))
    mlp_w = (w_up.astype(BF16), w_down.astype(BF16))
    row = lambda v: v.reshape(1, -1).astype(F32)

    x2 = x.reshape(t, d)
    for l in range(depth):
        proj = _in_proj(x2, row(pre_mix_g[l]), w_in_pieces, l, colscale,
                        tm=_pick(t, 512), tn=512)

        ret = _retention(proj, ret_tabs, batch=batch, seq=seq)

        lambda_init = 0.8 - 0.6 * math.exp(-0.3 * l)
        lam = (jnp.exp(jnp.sum(lambda_q1[l] * lambda_k1[l]).astype(F32))
               - jnp.exp(jnp.sum(lambda_q2[l] * lambda_k2[l]).astype(F32)) + lambda_init)
        da = _diff_attention(proj, lam.reshape(1), bias_tab, row(diff_subln_g[l]) * (1.0 - lambda_init),
                             batch=batch, seq=seq, tq=tq, group=4)

        wk, wvt = _prep_w_ukv(w_mla_ukv[l])
        qm, km, vt = _mla_prep(proj, row(mla_q_norm_g[l]), row(mla_kv_norm_g[l]),
                               _prep_w_uq(w_mla_uq[l]), wk, wvt, mla_tabs, seq=seq, tm=_pick(seq, 512), tk=tq)
        mo = _mla_attention(qm, km, vt, causal_tab, batch=batch, seq=seq, tq=tq, group=8)

        x2 = _merge(x2, ret, da, mo, proj, *merge_w, l, row(post_mix_g[l]), tm=_pick(t, 512))

        x2 = _mlp(x2, row(pre_mlp_g[l]), *mlp_w, l, row(post_mlp_g[l]), tm=_pick(t, 1024), tf=1024)
    return x2.reshape(batch, seq, d)
```

```python
import functools
import math

import jax
import jax.numpy as jnp
import numpy as np
from jax import lax
from jax.experimental import pallas as pl
from jax.experimental.pallas import tpu as pltpu

F32 = jnp.float32
BF16 = jnp.bfloat16

D_MODEL = 1024
EPS = 1e-6
RET_HEADS = 4
RET_DK = 128
RET_DV = 256
RET_CHUNK = 128
DIFF_HEADS = 8
DIFF_DH = 64
MLA_HEADS = 16
MLA_Q_LORA = 256
MLA_KV_LORA = 128
MLA_NOPE = 64
MLA_ROPE = 32
MLA_V = 64
ROPE_THETA = 10000.0
REL_BUCKETS = 32
REL_MAX_DIST = 128
D_FF = 4 * D_MODEL
N_BRANCH = 3

RET_QK_W = RET_HEADS * RET_DK
RET_V_W = RET_HEADS * RET_DV
DIFF_W = DIFF_HEADS * 2 * DIFF_DH
MLA_QH = MLA_NOPE + MLA_ROPE
MLA_KVH = MLA_NOPE + MLA_V
MLA_V_W = MLA_HEADS * MLA_V
MLA_PAD_W = MLA_HEADS * 128
IN_SPLITS = (RET_QK_W, RET_QK_W, RET_V_W, RET_V_W, DIFF_W, DIFF_W, DIFF_W,
             MLA_Q_LORA, MLA_KV_LORA, MLA_ROPE, N_BRANCH * D_MODEL)

LANES = 128
LOG2E = math.log2(math.e)
NEG_BIG = -1e30
VMEM_LIMIT = 56 * 1024 * 1024

COL_RQ = 0
COL_RK = COL_RQ + RET_QK_W
COL_RV = COL_RK + RET_QK_W
COL_RG = COL_RV + RET_V_W
COL_DQ = COL_RG + RET_V_W
COL_DK = COL_DQ + DIFF_W
COL_DV = COL_DK + DIFF_W
COL_GATES = COL_DV + DIFF_W
COL_CQ = COL_GATES + N_BRANCH * D_MODEL
COL_CKV = COL_CQ + MLA_Q_LORA
COL_KPE = COL_CKV + MLA_KV_LORA
N_IN = COL_KPE + LANES
N_IN_MAIN = COL_GATES
N_IN_SMALL = N_IN - COL_CQ


def _cparams(*sem):
    return pltpu.CompilerParams(dimension_semantics=sem, vmem_limit_bytes=VMEM_LIMIT)


def _rms(xf):
    return xf * lax.rsqrt(jnp.mean(xf * xf, axis=-1, keepdims=True) + EPS)


def _dot(a, b):
    return jnp.dot(a, b, preferred_element_type=F32)


def _dot_nt(a, b):
    return lax.dot_general(a, b, (((1,), (1,)), ((), ())), preferred_element_type=F32)


def _in_proj_kernel(x_ref, g_ref, wm_ref, wg_ref, ws_ref, cs_ref, o_ref, *, tn):
    h = (_rms(x_ref[...]) * g_ref[...]).astype(BF16)
    base = 0
    for w_ref in (wm_ref, wg_ref, ws_ref):
        for c in range(w_ref.shape[1] // tn):
            cols = slice(base + c * tn, base + (c + 1) * tn)
            acc = _dot(h, w_ref[:, c * tn:(c + 1) * tn])
            o_ref[:, cols] = (acc * cs_ref[:, cols]).astype(o_ref.dtype)
        base += w_ref.shape[1]


def _resident(shape):
    return pl.BlockSpec(shape, lambda *_: (0,) * len(shape), pipeline_mode=pl.Buffered(1))


def _layer_slab(w, layer):
    return pl.BlockSpec((None,) + w.shape[1:], lambda *_: (layer, 0, 0), pipeline_mode=pl.Buffered(1))


def _in_proj(x2, g, weights, layer, colscale, *, tm, tn):
    t, d = x2.shape
    n = sum(w.shape[2] for w in weights)
    assert all(w.shape[2] % tn == 0 for w in weights)
    return pl.pallas_call(
        functools.partial(_in_proj_kernel, tn=tn),
        grid=(t // tm,),
        in_specs=[pl.BlockSpec((tm, d), lambda i: (i, 0)), _resident((1, d))]
        + [_layer_slab(w, layer) for w in weights] + [_resident((1, n))],
        out_specs=pl.BlockSpec((tm, n), lambda i: (i, 0)),
        out_shape=jax.ShapeDtypeStruct((t, n), BF16),
        compiler_params=_cparams("parallel"),
        name="in_proj",
    )(x2, g, *weights, colscale)


def _ret_kernel(q_ref, k_ref, v_ref, rg_ref, cos_ref, sin_ref, dm_ref, xi_ref, zt_ref, gc_ref,
                o_ref, st_ref):
    st_ref[...] = jnp.zeros_like(st_ref)

    def chunk(n, carry):
        rows = _tile(n, RET_CHUNK)
        cos = cos_ref[rows, :]
        sin = sin_ref[rows, :]
        for h in range(RET_HEADS):
            qs = slice(h * RET_DK, (h + 1) * RET_DK)
            vs = slice(h * RET_DV, (h + 1) * RET_DV)
            q = q_ref[rows, qs].astype(F32)
            k = k_ref[rows, qs].astype(F32)
            qr = q * cos + pltpu.roll(q, RET_DK // 2, 1) * sin
            kr = k * cos + pltpu.roll(k, RET_DK // 2, 1) * sin
            v = v_ref[rows, vs]
            st = st_ref[h]
            inner = _dot_nt(qr.astype(BF16), kr.astype(BF16)) * dm_ref[h]
            out = _dot(inner.astype(BF16), v) + _dot((qr * xi_ref[h]).astype(BF16), st.astype(BF16))
            kz_t = (kr * zt_ref[h]).T.astype(BF16)
            st_ref[h] = gc_ref[h] * st + _dot(kz_t, v)
            rg = rg_ref[rows, vs].astype(F32)
            o_ref[rows, vs] = (_rms(out) * (rg * jax.nn.sigmoid(rg))).astype(o_ref.dtype)
        return carry

    lax.fori_loop(0, q_ref.shape[0] // RET_CHUNK, chunk, 0)


def _retention(proj, tabs, *, batch, seq):
    t = proj.shape[0]
    c = RET_CHUNK
    cos, sin, dmask, xi, zeta, gch = tabs
    return pl.pallas_call(
        _ret_kernel,
        grid=(batch,),
        in_specs=[pl.BlockSpec((seq, RET_QK_W), lambda b: (b, COL_RQ // RET_QK_W)),
                  pl.BlockSpec((seq, RET_QK_W), lambda b: (b, COL_RK // RET_QK_W)),
                  pl.BlockSpec((seq, RET_V_W), lambda b: (b, COL_RV // RET_V_W)),
                  pl.BlockSpec((seq, RET_V_W), lambda b: (b, COL_RG // RET_V_W)),
                  _resident((seq, RET_DK)), _resident((seq, RET_DK)),
                  _resident((RET_HEADS, c, c)), _resident((RET_HEADS, c, RET_DK)),
                  _resident((RET_HEADS, c, RET_DK)), _resident((RET_HEADS, 1, RET_DV))],
        out_specs=pl.BlockSpec((seq, RET_V_W), lambda b: (b, 0)),
        out_shape=jax.ShapeDtypeStruct((t, RET_V_W), BF16),
        scratch_shapes=[pltpu.VMEM((RET_HEADS, RET_DK, RET_DV), F32)],
        compiler_params=_cparams("parallel"),
        name="retention",
    )(proj, proj, proj, proj, cos, sin, dmask, xi, zeta, gch)


SUM_ROWS = 16


def _store_logits(s, s_ref, smax_ref):
    s_ref[...] = s
    smax_ref[...] = jnp.max(s, axis=0, keepdims=True)


def _softmax_update(s_ref, smax_ref, vt, m_ref, acc_ref):
    m = m_ref[...]
    m_new = jnp.maximum(m, smax_ref[...])
    alpha = jnp.exp2(m - m_new)
    p = jnp.exp2(s_ref[...] - m_new).astype(BF16)
    m_ref[...] = m_new
    vt_ones = jnp.concatenate([vt, jnp.ones((SUM_ROWS, vt.shape[1]), BF16)], axis=0)
    acc_ref[...] = alpha * acc_ref[...] + _dot(vt_ones, p)


def _softmax_result(acc_ref):
    dv = acc_ref.shape[0] - SUM_ROWS
    return acc_ref[:dv, :] / acc_ref[dv:dv + 1, :]


def _softmax_reset(m_ref, acc_ref):
    m_ref[...] = jnp.full(m_ref.shape, NEG_BIG, F32)
    acc_ref[...] = jnp.zeros(acc_ref.shape, F32)


def _softmax_scratch(group, dv, tk, n):
    return [pltpu.VMEM((2, group, tk, n), F32), pltpu.VMEM((2, group, 1, n), F32),
            pltpu.VMEM((group, 1, n), F32), pltpu.VMEM((group, dv + SUM_ROWS, n), F32)]


def _tile(i, size):
    return pl.ds(pl.multiple_of(i * size, size), size)


def _causal_tile_pipeline(nq, group, logits, softmax, finalize, near_tiles, interleave):
    def step(qi, j, slot):
        last = j == qi
        nqi = jnp.where(last, qi + 1, qi)
        nj = jnp.where(last, 0, j + 1)
        nqi_valid = jnp.minimum(nqi, nq - 1)

        def issue(near):
            if interleave:
                for g in range(group):
                    softmax(g, j, slot)
                    logits(g, nqi_valid, nj, 1 - slot, near)
            else:
                for g in range(group):
                    logits(g, nqi_valid, nj, 1 - slot, near)
                for g in range(group):
                    softmax(g, j, slot)

        lax.cond(nqi_valid - nj < near_tiles, lambda: issue(True), lambda: issue(False))

        @pl.when(last)
        def _():
            finalize(qi)

        return nqi, nj

    n_pairs = nq * (nq + 1) // 2
    zero = jnp.int32(0)
    for g in range(group):
        logits(g, zero, zero, 0, True)
    carry = lax.fori_loop(0, n_pairs // 2, lambda _, c: step(*step(*c, 0), 1), (zero, zero))
    if n_pairs % 2:
        step(*carry, 0)


def _diff_kernel(lam_ref, q_ref, k_ref, v_ref, tab_ref, g_ref, o_ref, vt_ref, s_ref, smax_ref, m_ref, acc_ref,
                 *, tq, group):
    w = 2 * DIFF_DH
    nq = vt_ref.shape[1]
    hslice = [slice(g * w, (g + 1) * w) for g in range(group)]
    state = [(m_ref.at[g], acc_ref.at[g]) for g in range(group)]
    for g in range(group):
        _softmax_reset(*state[g])
        for c in range(nq):
            vt_ref[g, c] = v_ref[c * tq:(c + 1) * tq, hslice[g]].astype(F32).T.astype(BF16)

    lane = lax.broadcasted_iota(jnp.int32, (1, w), 1)
    first_half = (lane < DIFF_DH).astype(BF16)
    second_half = (lane >= DIFF_DH).astype(BF16)

    def logits(g, qi, j, slot, near):
        q = q_ref[_tile(qi, tq), hslice[g]]
        q2 = jnp.concatenate([q * first_half, q * second_half], axis=0)
        s = _dot_nt(k_ref[_tile(j, tq), hslice[g]], q2)
        if near:
            bias = tab_ref[g, qi - j]
            s = s + jnp.concatenate([bias, bias], axis=1)
        _store_logits(s, s_ref.at[slot, g], smax_ref.at[slot, g])

    def softmax(g, j, slot):
        _softmax_update(s_ref.at[slot, g], smax_ref.at[slot, g], vt_ref[g, j], *state[g])

    def finalize(qi):
        for g in range(group):
            o = _softmax_result(state[g][1])
            a = o[:, :tq] - lam_ref[0] * o[:, tq:]
            a = a * lax.rsqrt(jnp.mean(a * a, axis=0, keepdims=True) + EPS)
            o_ref[_tile(qi, tq), hslice[g]] = (a.T * g_ref[...]).astype(o_ref.dtype)
            _softmax_reset(*state[g])

    _causal_tile_pipeline(nq, group, logits, softmax, finalize, near_tiles=tab_ref.shape[1],
                          interleave=True)


def _diff_attention(proj, lam, tab, g, *, batch, seq, tq, group):
    t = proj.shape[0]
    nq = seq // tq
    w = group * 2 * DIFF_DH
    return pl.pallas_call(
        functools.partial(_diff_kernel, tq=tq, group=group),
        grid=(batch, DIFF_HEADS // group),
        in_specs=[pl.BlockSpec(memory_space=pltpu.SMEM),
                  pl.BlockSpec((seq, w), lambda b, h: (b, COL_DQ // w + h)),
                  pl.BlockSpec((seq, w), lambda b, h: (b, COL_DK // w + h)),
                  pl.BlockSpec((seq, w), lambda b, h: (b, COL_DV // w + h)),
                  pl.BlockSpec((group, 2, tq, tq), lambda b, h: (h, 0, 0, 0)),
                  pl.BlockSpec((1, 2 * DIFF_DH), lambda b, h: (0, 0))],
        out_specs=pl.BlockSpec((seq, w), lambda b, h: (b, h)),
        out_shape=jax.ShapeDtypeStruct((t, DIFF_W), BF16),
        scratch_shapes=[pltpu.VMEM((group, nq, 2 * DIFF_DH, tq), BF16)]
        + _softmax_scratch(group, 2 * DIFF_DH, tq, 2 * tq),
        compiler_params=_cparams("parallel", "parallel"),
        name="diff_attention",
    )(lam, proj, proj, proj, tab, g)


def _mla_prep_kernel(cq_ref, ckv_ref, kpe_ref, gq_ref, gkv_ref, wq_ref, wk_ref, wvt_ref,
                     cq_tab, sq_tab, ck_tab, sk_tab, qt_ref, k_ref, vt_ref):
    hq = (_rms(cq_ref[...].astype(F32)) * gq_ref[...]).astype(BF16)
    hkv = (_rms(ckv_ref[...].astype(F32)) * gkv_ref[...]).astype(BF16)
    qa = _dot(hq, wq_ref[...])
    ka = _dot(hkv, wk_ref[...])
    tk = vt_ref.shape[-1]
    for c in range(vt_ref.shape[0]):
        vt_ref[c] = _dot_nt(wvt_ref[...], hkv[c * tk:(c + 1) * tk]).astype(vt_ref.dtype)
    kpe = kpe_ref[...].astype(F32)
    kpe = kpe * ck_tab[...] + pltpu.roll(kpe, LANES - MLA_ROPE, 1) * sk_tab[...]
    cq = cq_tab[...]
    sq = sq_tab[...]
    for h in range(MLA_HEADS):
        hs = slice(h * LANES, (h + 1) * LANES)
        qh = qa[:, hs]
        qh = qh * cq + pltpu.roll(qh, LANES - MLA_ROPE, 1) * sq
        for c in range(qt_ref.shape[0]):
            qt_ref[c, hs, :] = qh[c * tk:(c + 1) * tk].T.astype(qt_ref.dtype)
        k_ref[:, hs] = (ka[:, hs] + kpe).astype(k_ref.dtype)


def _mla_prep(proj, gq, gkv, wq, wk, wvt, tabs, *, seq, tm, tk):
    t = proj.shape[0]
    ns = seq // tm
    nc = tm // tk
    const = lambda i: (0, 0)
    tab_spec = pl.BlockSpec((tm, LANES), lambda i: (i % ns, 0))
    return pl.pallas_call(
        _mla_prep_kernel,
        grid=(t // tm,),
        in_specs=[pl.BlockSpec((tm, MLA_Q_LORA), lambda i: (i, COL_CQ // MLA_Q_LORA)),
                  pl.BlockSpec((tm, MLA_KV_LORA), lambda i: (i, COL_CKV // MLA_KV_LORA)),
                  pl.BlockSpec((tm, LANES), lambda i: (i, COL_KPE // LANES)),
                  pl.BlockSpec((1, MLA_Q_LORA), const),
                  pl.BlockSpec((1, MLA_KV_LORA), const),
                  pl.BlockSpec((MLA_Q_LORA, MLA_PAD_W), const),
                  pl.BlockSpec((MLA_KV_LORA, MLA_PAD_W), const),
                  pl.BlockSpec((MLA_V_W, MLA_KV_LORA), const),
                  tab_spec, tab_spec, tab_spec, tab_spec],
        out_specs=[pl.BlockSpec((nc, MLA_PAD_W, tk), lambda i: (i, 0, 0)),
                   pl.BlockSpec((tm, MLA_PAD_W), lambda i: (i, 0)),
                   pl.BlockSpec((nc, MLA_V_W, tk), lambda i: (i, 0, 0))],
        out_shape=[jax.ShapeDtypeStruct((t // tk, MLA_PAD_W, tk), BF16),
                   jax.ShapeDtypeStruct((t, MLA_PAD_W), BF16),
                   jax.ShapeDtypeStruct((t // tk, MLA_V_W, tk), BF16)],
        compiler_params=_cparams("parallel"),
        name="mla_prep",
    )(proj, proj, proj, gq, gkv, wq, wk, wvt, *tabs)


def _mla_kernel(qt_ref, k_ref, vt_ref, mask_ref, o_ref, s_ref, smax_ref, m_ref, acc_ref, *, tq, group):
    hslice = [slice(g * LANES, (g + 1) * LANES) for g in range(group)]
    vslice = [slice(g * MLA_V, (g + 1) * MLA_V) for g in range(group)]
    state = [(m_ref.at[g], acc_ref.at[g]) for g in range(group)]
    for g in range(group):
        _softmax_reset(*state[g])

    def logits(g, qi, j, slot, near):
        s = _dot(k_ref[_tile(j, tq), hslice[g]], qt_ref[qi, hslice[g], :])
        if near:
            s = s + mask_ref[...]
        _store_logits(s, s_ref.at[slot, g], smax_ref.at[slot, g])

    def softmax(g, j, slot):
        _softmax_update(s_ref.at[slot, g], smax_ref.at[slot, g], vt_ref[j, vslice[g], :], *state[g])

    def finalize(qi):
        o = jnp.concatenate([_softmax_result(acc) for _, acc in state], axis=0)
        o_ref[_tile(qi, tq), :] = o.T.astype(o_ref.dtype)
        for g in range(group):
            _softmax_reset(*state[g])

    _causal_tile_pipeline(vt_ref.shape[0], group, logits, softmax, finalize, near_tiles=1,
                          interleave=False)


def _mla_attention(qt, km, vt, mask, *, batch, seq, tq, group):
    t = km.shape[0]
    nq = seq // tq
    return pl.pallas_call(
        functools.partial(_mla_kernel, tq=tq, group=group),
        grid=(batch, MLA_HEADS // group),
        in_specs=[pl.BlockSpec((nq, group * LANES, tq), lambda b, h: (b, h, 0)),
                  pl.BlockSpec((seq, group * LANES), lambda b, h: (b, h)),
                  pl.BlockSpec((nq, group * MLA_V, tq), lambda b, h: (b, h, 0)),
                  pl.BlockSpec((tq, tq), lambda b, h: (0, 0))],
        out_specs=pl.BlockSpec((seq, group * MLA_V), lambda b, h: (b, h)),
        out_shape=jax.ShapeDtypeStruct((t, MLA_V_W), BF16),
        scratch_shapes=_softmax_scratch(group, MLA_V, tq, tq),
        compiler_params=_cparams("parallel", "parallel"),
        name="mla_attention",
    )(qt, km, vt, mask)


def _merge_kernel(x_ref, a_ref, b_ref, c_ref, g0_ref, g1_ref, g2_ref, wa_ref, wb_ref, wc_ref, wo_ref,
                  pg_ref, o_ref):
    merged = jax.nn.sigmoid(g0_ref[...].astype(F32)) * _dot(a_ref[...], wa_ref[...])
    merged += jax.nn.sigmoid(g1_ref[...].astype(F32)) * _dot(b_ref[...], wb_ref[...])
    merged += jax.nn.sigmoid(g2_ref[...].astype(F32)) * _dot(c_ref[...], wc_ref[...])
    y = _dot(merged.astype(BF16), wo_ref[...])
    o_ref[...] = x_ref[...] + _rms(y) * pg_ref[...]


def _merge(x2, a, b, c, proj, wa, wb, wc, wo, layer, pg, *, tm):
    t, d = x2.shape
    rowblk = pl.BlockSpec((tm, d), lambda i: (i, 0))
    gate = lambda n: pl.BlockSpec((tm, d), lambda i: (i, COL_GATES // D_MODEL + n))
    return pl.pallas_call(
        _merge_kernel,
        grid=(t // tm,),
        in_specs=[rowblk, rowblk, rowblk, rowblk, gate(0), gate(1), gate(2)]
        + [_layer_slab(w, layer) for w in (wa, wb, wc, wo)] + [_resident((1, d))],
        out_specs=rowblk,
        out_shape=jax.ShapeDtypeStruct((t, d), F32),
        compiler_params=_cparams("parallel"),
        name="merge",
    )(x2, a, b, c, proj, proj, proj, wa, wb, wc, wo, pg)


def _mlp_kernel(x_ref, g_ref, wu_ref, wd_ref, pg_ref, o_ref, *, tf):
    x = x_ref[...]
    h = (_rms(x) * g_ref[...]).astype(BF16)
    acc = None
    for c in range(wu_ref.shape[1] // tf):
        u = jnp.maximum(_dot(h, wu_ref[:, c * tf:(c + 1) * tf]), 0.0)
        part = _dot((u * u).astype(BF16), wd_ref[c * tf:(c + 1) * tf, :])
        acc = part if acc is None else acc + part
    o_ref[...] = x + _rms(acc) * pg_ref[...]


def _mlp(x2, g, wu, wd, layer, pg, *, tm, tf):
    t, d = x2.shape
    rowblk = pl.BlockSpec((tm, d), lambda i: (i, 0))
    return pl.pallas_call(
        functools.partial(_mlp_kernel, tf=tf),
        grid=(t // tm,),
        in_specs=[rowblk, _resident((1, d)), _layer_slab(wu, layer), _layer_slab(wd, layer), _resident((1, d))],
        out_specs=rowblk,
        out_shape=jax.ShapeDtypeStruct((t, d), F32),
        compiler_params=_cparams("parallel"),
        name="mlp",
    )(x2, g, wu, wd, pg)


def _rot_half_cols(w):
    half = w.shape[-1] // 2
    return jnp.concatenate([-w[..., half:], w[..., :half]], axis=-1)


def _cast_transposed_kernel(wt_ref, o_ref):
    o_ref[...] = wt_ref[...].T.astype(o_ref.dtype)


def _cast_transposed(wt, n, *, tn):
    depth, _, d = wt.shape
    return pl.pallas_call(
        _cast_transposed_kernel,
        grid=(depth, n // tn),
        in_specs=[pl.BlockSpec((None, tn, d), lambda l, c: (l, c, 0))],
        out_specs=pl.BlockSpec((None, d, tn), lambda l, c: (l, 0, c)),
        out_shape=jax.ShapeDtypeStruct((depth, d, n), BF16),
        compiler_params=_cparams("parallel", "parallel"),
        name="w_in_prep",
    )(wt)


def _prep_w_in(w_in, *, tn):
    wt = jnp.swapaxes(w_in, 1, 2)
    kpe_at = N_IN_MAIN + MLA_Q_LORA + MLA_KV_LORA
    gates_at = kpe_at + MLA_ROPE
    kpe = wt[:, kpe_at:gates_at]
    half = MLA_ROPE // 2
    zeros = jnp.zeros((wt.shape[0], MLA_NOPE, wt.shape[2]), wt.dtype)
    small = jnp.concatenate([wt[:, N_IN_MAIN:kpe_at], zeros, kpe, -kpe[:, half:], kpe[:, :half]], axis=1)
    return (_cast_transposed(wt, N_IN_MAIN, tn=tn),
            _cast_transposed(wt[:, gates_at:], N_BRANCH * D_MODEL, tn=tn),
            _cast_transposed(small, N_IN_SMALL, tn=tn))


def _in_colscale():
    cs = np.ones((1, N_IN), np.float32)
    cs[0, COL_DQ:COL_DQ + DIFF_W] = DIFF_DH ** -0.5 * LOG2E
    cs[0, COL_RK:COL_RK + RET_QK_W] = RET_DK ** -0.5
    return jnp.asarray(cs)


def _prep_w_uq(w):
    w = w.reshape(MLA_Q_LORA, MLA_HEADS, MLA_QH)
    pe = w[..., MLA_NOPE:]
    return jnp.concatenate([w, _rot_half_cols(pe)], axis=-1).reshape(MLA_Q_LORA, MLA_PAD_W).astype(BF16)


def _prep_w_ukv(w):
    w = w.reshape(MLA_KV_LORA, MLA_HEADS, MLA_KVH)
    wk = jnp.concatenate([w[..., :MLA_NOPE], jnp.zeros_like(w[..., :MLA_NOPE])], axis=-1)
    return (wk.reshape(MLA_KV_LORA, MLA_PAD_W).astype(BF16),
            w[..., MLA_NOPE:].reshape(MLA_KV_LORA, MLA_V_W).T.astype(BF16))


def _ret_tables(seq):
    pos = np.arange(seq, dtype=np.float64)
    inv_freq = 1.0 / (10000.0 ** np.linspace(0.0, 1.0, RET_DK // 2))
    ang = pos[:, None] * inv_freq[None, :]
    cos, sin = np.cos(ang), np.sin(ang)
    cos_t = np.concatenate([cos, cos], axis=-1)
    sin_t = np.concatenate([-sin, sin], axis=-1)
    log_g = np.log1p(-np.exp2(-5.0 - np.arange(RET_HEADS, dtype=np.float64)))
    idx = np.arange(RET_CHUNK, dtype=np.float64)
    rel = idx[:, None] - idx[None, :]
    dmask = np.where(rel >= 0, np.exp(np.maximum(rel, 0.0)[None] * log_g[:, None, None]), 0.0)
    xi = np.exp((idx + 1.0)[None, :] * log_g[:, None])[:, :, None]
    zeta = np.exp((RET_CHUNK - 1.0 - idx)[None, :] * log_g[:, None])[:, :, None]
    g_chunk = np.exp(RET_CHUNK * log_g)[:, None, None]
    tabs = (cos_t, sin_t, dmask,
            np.broadcast_to(xi, (RET_HEADS, RET_CHUNK, RET_DK)),
            np.broadcast_to(zeta, (RET_HEADS, RET_CHUNK, RET_DK)),
            np.broadcast_to(g_chunk, (RET_HEADS, 1, RET_DV)))
    return tuple(jnp.asarray(t, F32) for t in tabs)


def _mla_tables(seq):
    pos = np.arange(seq, dtype=np.float64)
    inv_freq = 1.0 / (ROPE_THETA ** (np.arange(0, MLA_ROPE, 2, dtype=np.float64) / MLA_ROPE))
    ang = pos[:, None] * inv_freq[None, :]
    cos, sin = np.cos(ang), np.sin(ang)
    z_nope = np.zeros((seq, MLA_NOPE))
    z_rope = np.zeros((seq, MLA_ROPE))
    cos_k = np.concatenate([z_nope, cos, cos, z_rope], axis=-1)
    sin_k = np.concatenate([z_nope, sin, sin, z_rope], axis=-1)
    qscale = MLA_QH ** -0.5 * LOG2E
    cos_q = np.concatenate([np.ones((seq, MLA_NOPE)), cos, cos, z_rope], axis=-1) * qscale
    sin_q = sin_k * qscale
    return tuple(jnp.asarray(t, F32) for t in (cos_q, sin_q, cos_k, sin_k))


def _t5_bucket(n):
    max_exact = REL_BUCKETS // 2
    nf = jnp.maximum(n, 1).astype(F32)
    large = max_exact + (jnp.log(nf / max_exact) / math.log(REL_MAX_DIST / max_exact)
                         * (REL_BUCKETS - max_exact)).astype(jnp.int32)
    large = jnp.minimum(large, REL_BUCKETS - 1)
    return jnp.where(n < max_exact, n, large)


def _toeplitz(g, tq):
    h = g.shape[0]
    padded = jnp.concatenate([g, jnp.zeros((h, 1), g.dtype)], axis=1)
    skewed = jnp.tile(padded, (1, tq))[:, :tq * (2 * tq - 1)].reshape(h, tq, 2 * tq - 1)
    return skewed[:, :, tq - 1:]


def _bias_table(rel_bias, tq):
    assert tq >= REL_MAX_DIST
    bias = rel_bias[_t5_bucket(jnp.arange(2 * tq))].astype(F32).T
    bias = (bias - rel_bias[REL_BUCKETS - 1].astype(F32)[:, None]) * LOG2E
    masked = jnp.full((bias.shape[0], tq - 1), NEG_BIG, F32)
    diagonal = jnp.concatenate([masked, bias[:, :tq]], axis=1)
    behind = bias[:, 1:]
    return jnp.stack([_toeplitz(diagonal, tq), _toeplitz(behind, tq)], axis=1)


def _causal_table(tq):
    kk = np.arange(tq)[:, None]
    qq = np.arange(tq)[None, :]
    return jnp.asarray(np.where(qq >= kk, 0.0, NEG_BIG), F32)


def _pick(t, want):
    return want if t % want == 0 else t


def kernel(x, rel_bias, pre_mix_g, w_in, w_ret_o, lambda_q1, lambda_k1, lambda_q2, lambda_k2, diff_subln_g, w_diff_o, mla_q_norm_g, w_mla_uq, mla_kv_norm_g, w_mla_ukv, w_mla_o, w_out, post_mix_g, pre_mlp_g, w_up, w_down, post_mlp_g):
    batch, seq, d = x.shape
    t = batch * seq
    depth = w_in.shape[0]
    tq = _pick(seq, 256)
    ret_tabs = _ret_tables(seq)
    mla_tabs = _mla_tables(seq)
    bias_tab = _bias_table(rel_bias, tq)
    causal_tab = _causal_table(tq)
    colscale = _in_colscale()
    w_in_pieces = _prep_w_in(w_in, tn=512)
    merge_w = tuple(w.astype(BF16) for w in (w_ret_o, w_diff_o, w_mla_o, w_out))
    mlp_w = (w_up.astype(BF16), w_down.astype(BF16))
    row = lambda v: v.reshape(1, -1).astype(F32)

    x2 = x.reshape(t, d)
    for l in range(depth):
        proj = _in_proj(x2, row(pre_mix_g[l]), w_in_pieces, l, colscale,
                        tm=_pick(t, 512), tn=512)

        ret = _retention(proj, ret_tabs, batch=batch, seq=seq)

        lambda_init = 0.8 - 0.6 * math.exp(-0.3 * l)
        lam = (jnp.exp(jnp.sum(lambda_q1[l] * lambda_k1[l]).astype(F32))
               - jnp.exp(jnp.sum(lambda_q2[l] * lambda_k2[l]).astype(F32)) + lambda_init)
        da = _diff_attention(proj, lam.reshape(1), bias_tab, row(diff_subln_g[l]) * (1.0 - lambda_init),
                             batch=batch, seq=seq, tq=tq, group=4)

        wk, wvt = _prep_w_ukv(w_mla_ukv[l])
        qm, km, vt = _mla_prep(proj, row(mla_q_norm_g[l]), row(mla_kv_norm_g[l]),
                               _prep_w_uq(w_mla_uq[l]), wk, wvt, mla_tabs, seq=seq, tm=_pick(seq, 512), tk=tq)
        mo = _mla_attention(qm, km, vt, causal_tab, batch=batch, seq=seq, tq=tq, group=8)

        x2 = _merge(x2, ret, da, mo, proj, *merge_w, l, row(post_mix_g[l]), tm=_pick(t, 512))

        x2 = _mlp(x2, row(pre_mlp_g[l]), *mlp_w, l, row(post_mlp_g[l]), tm=_pick(t, 1024), tf=1024)
    return x2.reshape(batch, seq, d)
```

```python
import functools
import math

import jax
import jax.numpy as jnp
import numpy as np
from jax import lax
from jax.experimental import pallas as pl
from jax.experimental.pallas import tpu as pltpu

F32 = jnp.float32
BF16 = jnp.bfloat16

D_MODEL = 1024
EPS = 1e-6
RET_HEADS = 4
RET_DK = 128
RET_DV = 256
RET_CHUNK = 128
DIFF_HEADS = 8
DIFF_DH = 64
MLA_HEADS = 16
MLA_Q_LORA = 256
MLA_KV_LORA = 128
MLA_NOPE = 64
MLA_ROPE = 32
MLA_V = 64
ROPE_THETA = 10000.0
REL_BUCKETS = 32
REL_MAX_DIST = 128
D_FF = 4 * D_MODEL
N_BRANCH = 3

RET_QK_W = RET_HEADS * RET_DK
RET_V_W = RET_HEADS * RET_DV
DIFF_W = DIFF_HEADS * 2 * DIFF_DH
MLA_QH = MLA_NOPE + MLA_ROPE
MLA_KVH = MLA_NOPE + MLA_V
MLA_V_W = MLA_HEADS * MLA_V
MLA_PAD_W = MLA_HEADS * 128
IN_SPLITS = (RET_QK_W, RET_QK_W, RET_V_W, RET_V_W, DIFF_W, DIFF_W, DIFF_W,
             MLA_Q_LORA, MLA_KV_LORA, MLA_ROPE, N_BRANCH * D_MODEL)

LANES = 128
LOG2E = math.log2(math.e)
NEG_BIG = -1e30
VMEM_LIMIT = 56 * 1024 * 1024

COL_RQ = 0
COL_RK = COL_RQ + RET_QK_W
COL_RV = COL_RK + RET_QK_W
COL_RG = COL_RV + RET_V_W
COL_DQ = COL_RG + RET_V_W
COL_DK = COL_DQ + DIFF_W
COL_DV = COL_DK + DIFF_W
N_IN_MAIN = COL_DV + DIFF_W
N_IN_SMALL = MLA_Q_LORA + MLA_KV_LORA + LANES


def _cparams(*sem):
    return pltpu.CompilerParams(dimension_semantics=sem, vmem_limit_bytes=VMEM_LIMIT)


def _rms(xf):
    return xf * lax.rsqrt(jnp.mean(xf * xf, axis=-1, keepdims=True) + EPS)


def _dot(a, b):
    return jnp.dot(a, b, preferred_element_type=F32)


def _dot_nt(a, b):
    return lax.dot_general(a, b, (((1,), (1,)), ((), ())), preferred_element_type=F32)


HEADS_PER_UP = 4


def _in_proj_kernel(x_ref, g_ref, wm_ref, ws_ref, cs_ref, gq_ref, gkv_ref, wqt_ref, wk_ref, wvt_ref,
                    cqt_tab, sqt_tab, ck_tab, sk_tab, o_ref, qt_ref, k_ref, vt_ref, *, tn):
    h = (_rms(x_ref[...]) * g_ref[...]).astype(BF16)

    small = _dot(h, ws_ref[...])
    hq = (_rms(small[:, :MLA_Q_LORA]) * gq_ref[...]).astype(BF16)
    hkv = (_rms(small[:, MLA_Q_LORA:MLA_Q_LORA + MLA_KV_LORA]) * gkv_ref[...]).astype(BF16)
    kpe = small[:, MLA_Q_LORA + MLA_KV_LORA:]
    kpe = kpe * ck_tab[...] + pltpu.roll(kpe, LANES - MLA_ROPE, 1) * sk_tab[...]
    tk = vt_ref.shape[-1]
    n_tiles = vt_ref.shape[0]
    for c in range(n_tiles):
        vt_ref[c] = _dot_nt(wvt_ref[...], hkv[c * tk:(c + 1) * tk]).astype(vt_ref.dtype)
    cqt = cqt_tab[...]
    sqt = sqt_tab[...]
    up = HEADS_PER_UP * LANES
    for hg in range(MLA_HEADS // HEADS_PER_UP):
        q_t = _dot_nt(wqt_ref[hg * up:(hg + 1) * up, :], hq)
        ka = _dot(hkv, wk_ref[:, hg * up:(hg + 1) * up])
        for hh in range(HEADS_PER_UP):
            hs = slice(hh * LANES, (hh + 1) * LANES)
            out = slice(hg * up + hh * LANES, hg * up + (hh + 1) * LANES)
            blk = q_t[hs]
            partner = jnp.concatenate([blk[MLA_ROPE:], blk[:MLA_ROPE]], axis=0)
            blk = blk * cqt + partner * sqt
            for c in range(n_tiles):
                qt_ref[c, out, :] = blk[:, c * tk:(c + 1) * tk].astype(qt_ref.dtype)
            k_ref[:, out] = (ka[:, hs] + kpe).astype(k_ref.dtype)

    for c in range(wm_ref.shape[1] // tn):
        cols = slice(c * tn, (c + 1) * tn)
        o_ref[:, cols] = (_dot(h, wm_ref[:, cols]) * cs_ref[:, cols]).astype(o_ref.dtype)


def _resident(shape):
    return pl.BlockSpec(shape, lambda *_: (0,) * len(shape), pipeline_mode=pl.Buffered(1))


def _layer_slab(w, layer):
    return pl.BlockSpec((None,) + w.shape[1:], lambda *_: (layer, 0, 0), pipeline_mode=pl.Buffered(1))


def _in_proj(x2, g, w_main, w_small, layer, colscale, gq, gkv, wqt, wk, wvt, tabs, *, seq, tm, tn, tk):
    t, d = x2.shape
    n = w_main.shape[2]
    assert n % tn == 0 and tm % tk == 0 and seq % tm == 0
    ns = seq // tm
    nc = tm // tk
    cqt, sqt, ck, sk = tabs
    tab_t = pl.BlockSpec((LANES, tm), lambda i: (0, i % ns))
    tab = pl.BlockSpec((tm, LANES), lambda i: (i % ns, 0))
    return pl.pallas_call(
        functools.partial(_in_proj_kernel, tn=tn),
        grid=(t // tm,),
        in_specs=[pl.BlockSpec((tm, d), lambda i: (i, 0)), _resident((1, d)),
                  _layer_slab(w_main, layer), _layer_slab(w_small, layer), _resident((1, n)),
                  _resident(gq.shape), _resident(gkv.shape),
                  _resident(wqt.shape), _resident(wk.shape), _resident(wvt.shape),
                  tab_t, tab_t, tab, tab],
        out_specs=[pl.BlockSpec((tm, n), lambda i: (i, 0)),
                   pl.BlockSpec((nc, MLA_PAD_W, tk), lambda i: (i, 0, 0)),
                   pl.BlockSpec((tm, MLA_PAD_W), lambda i: (i, 0)),
                   pl.BlockSpec((nc, MLA_V_W, tk), lambda i: (i, 0, 0))],
        out_shape=[jax.ShapeDtypeStruct((t, n), BF16),
                   jax.ShapeDtypeStruct((t // tk, MLA_PAD_W, tk), BF16),
                   jax.ShapeDtypeStruct((t, MLA_PAD_W), BF16),
                   jax.ShapeDtypeStruct((t // tk, MLA_V_W, tk), BF16)],
        compiler_params=_cparams("parallel"),
        name="in_proj",
    )(x2, g, w_main, w_small, colscale, gq, gkv, wqt, wk, wvt, cqt, sqt, ck, sk)


def _ret_kernel(q_ref, k_ref, v_ref, rg_ref, cos_ref, sin_ref, dm_ref, xi_ref, zt_ref, gc_ref,
                o_ref, st_ref):
    st_ref[...] = jnp.zeros_like(st_ref)

    def chunk(n, carry):
        rows = _tile(n, RET_CHUNK)
        cos = cos_ref[rows, :]
        sin = sin_ref[rows, :]
        for h in range(RET_HEADS):
            qs = slice(h * RET_DK, (h + 1) * RET_DK)
            vs = slice(h * RET_DV, (h + 1) * RET_DV)
            q = q_ref[rows, qs].astype(F32)
            k = k_ref[rows, qs].astype(F32)
            qr = q * cos + pltpu.roll(q, RET_DK // 2, 1) * sin
            kr = k * cos + pltpu.roll(k, RET_DK // 2, 1) * sin
            v = v_ref[rows, vs]
            st = st_ref[h]
            inner = _dot_nt(qr.astype(BF16), kr.astype(BF16)) * dm_ref[h]
            out = _dot(inner.astype(BF16), v) + _dot((qr * xi_ref[h]).astype(BF16), st.astype(BF16))
            kz_t = (kr * zt_ref[h]).T.astype(BF16)
            st_ref[h] = gc_ref[h] * st + _dot(kz_t, v)
            rg = rg_ref[rows, vs].astype(F32)
            o_ref[rows, vs] = (_rms(out) * (rg * jax.nn.sigmoid(rg))).astype(o_ref.dtype)
        return carry

    lax.fori_loop(0, q_ref.shape[0] // RET_CHUNK, chunk, 0)


def _retention(proj, tabs, *, batch, seq):
    t = proj.shape[0]
    c = RET_CHUNK
    cos, sin, dmask, xi, zeta, gch = tabs
    return pl.pallas_call(
        _ret_kernel,
        grid=(batch,),
        in_specs=[pl.BlockSpec((seq, RET_QK_W), lambda b: (b, COL_RQ // RET_QK_W)),
                  pl.BlockSpec((seq, RET_QK_W), lambda b: (b, COL_RK // RET_QK_W)),
                  pl.BlockSpec((seq, RET_V_W), lambda b: (b, COL_RV // RET_V_W)),
                  pl.BlockSpec((seq, RET_V_W), lambda b: (b, COL_RG // RET_V_W)),
                  _resident((seq, RET_DK)), _resident((seq, RET_DK)),
                  _resident((RET_HEADS, c, c)), _resident((RET_HEADS, c, RET_DK)),
                  _resident((RET_HEADS, c, RET_DK)), _resident((RET_HEADS, 1, RET_DV))],
        out_specs=pl.BlockSpec((seq, RET_V_W), lambda b: (b, 0)),
        out_shape=jax.ShapeDtypeStruct((t, RET_V_W), BF16),
        scratch_shapes=[pltpu.VMEM((RET_HEADS, RET_DK, RET_DV), F32)],
        compiler_params=_cparams("parallel"),
        name="retention",
    )(proj, proj, proj, proj, cos, sin, dmask, xi, zeta, gch)


SUM_ROWS = 16


def _store_logits(s, s_ref, smax_ref):
    s_ref[...] = s
    smax_ref[...] = jnp.max(s, axis=0, keepdims=True)


def _softmax_update(s_ref, smax_ref, vt, m_ref, acc_ref):
    m = m_ref[...]
    m_new = jnp.maximum(m, smax_ref[...])
    alpha = jnp.exp2(m - m_new)
    p = jnp.exp2(s_ref[...] - m_new).astype(BF16)
    m_ref[...] = m_new
    vt_ones = jnp.concatenate([vt, jnp.ones((SUM_ROWS, vt.shape[1]), BF16)], axis=0)
    acc_ref[...] = alpha * acc_ref[...] + _dot(vt_ones, p)


def _softmax_result(acc_ref):
    dv = acc_ref.shape[0] - SUM_ROWS
    return acc_ref[:dv, :] / acc_ref[dv:dv + 1, :]


def _softmax_reset(m_ref, acc_ref):
    m_ref[...] = jnp.full(m_ref.shape, NEG_BIG, F32)
    acc_ref[...] = jnp.zeros(acc_ref.shape, F32)


def _softmax_scratch(group, dv, tk, n):
    return [pltpu.VMEM((2, group, tk, n), F32), pltpu.VMEM((2, group, 1, n), F32),
            pltpu.VMEM((group, 1, n), F32), pltpu.VMEM((group, dv + SUM_ROWS, n), F32)]


def _tile(i, size):
    return pl.ds(pl.multiple_of(i * size, size), size)


def _causal_tile_pipeline(nq, group, logits, softmax, finalize, near_tiles, interleave):
    def step(qi, j, slot):
        last = j == qi
        nqi = jnp.where(last, qi + 1, qi)
        nj = jnp.where(last, 0, j + 1)
        nqi_valid = jnp.minimum(nqi, nq - 1)

        def issue(near):
            if interleave:
                for g in range(group):
                    softmax(g, j, slot)
                    logits(g, nqi_valid, nj, 1 - slot, near)
            else:
                for g in range(group):
                    logits(g, nqi_valid, nj, 1 - slot, near)
                for g in range(group):
                    softmax(g, j, slot)

        lax.cond(nqi_valid - nj < near_tiles, lambda: issue(True), lambda: issue(False))

        @pl.when(last)
        def _():
            finalize(qi)

        return nqi, nj

    n_pairs = nq * (nq + 1) // 2
    zero = jnp.int32(0)
    for g in range(group):
        logits(g, zero, zero, 0, True)
    carry = lax.fori_loop(0, n_pairs // 2, lambda _, c: step(*step(*c, 0), 1), (zero, zero))
    if n_pairs % 2:
        step(*carry, 0)


def _diff_kernel(lam_ref, q_ref, k_ref, v_ref, tab_ref, g_ref, o_ref, vt_ref, s_ref, smax_ref, m_ref, acc_ref,
                 *, tq, group):
    w = 2 * DIFF_DH
    nq = vt_ref.shape[1]
    hslice = [slice(g * w, (g + 1) * w) for g in range(group)]
    state = [(m_ref.at[g], acc_ref.at[g]) for g in range(group)]
    for g in range(group):
        _softmax_reset(*state[g])
        for c in range(nq):
            vt_ref[g, c] = v_ref[c * tq:(c + 1) * tq, hslice[g]].astype(F32).T.astype(BF16)

    lane = lax.broadcasted_iota(jnp.int32, (1, w), 1)
    first_half = (lane < DIFF_DH).astype(BF16)
    second_half = (lane >= DIFF_DH).astype(BF16)

    def logits(g, qi, j, slot, near):
        q = q_ref[_tile(qi, tq), hslice[g]]
        q2 = jnp.concatenate([q * first_half, q * second_half], axis=0)
        s = _dot_nt(k_ref[_tile(j, tq), hslice[g]], q2)
        if near:
            bias = tab_ref[g, qi - j]
            s = s + jnp.concatenate([bias, bias], axis=1)
        _store_logits(s, s_ref.at[slot, g], smax_ref.at[slot, g])

    def softmax(g, j, slot):
        _softmax_update(s_ref.at[slot, g], smax_ref.at[slot, g], vt_ref[g, j], *state[g])

    def finalize(qi):
        for g in range(group):
            o = _softmax_result(state[g][1])
            a = o[:, :tq] - lam_ref[0] * o[:, tq:]
            a = a * lax.rsqrt(jnp.mean(a * a, axis=0, keepdims=True) + EPS)
            o_ref[_tile(qi, tq), hslice[g]] = (a.T * g_ref[...]).astype(o_ref.dtype)
            _softmax_reset(*state[g])

    _causal_tile_pipeline(nq, group, logits, softmax, finalize, near_tiles=tab_ref.shape[1],
                          interleave=True)


def _diff_attention(proj, lam, tab, g, *, batch, seq, tq, group):
    t = proj.shape[0]
    nq = seq // tq
    w = group * 2 * DIFF_DH
    return pl.pallas_call(
        functools.partial(_diff_kernel, tq=tq, group=group),
        grid=(batch, DIFF_HEADS // group),
        in_specs=[pl.BlockSpec(memory_space=pltpu.SMEM),
                  pl.BlockSpec((seq, w), lambda b, h: (b, COL_DQ // w + h)),
                  pl.BlockSpec((seq, w), lambda b, h: (b, COL_DK // w + h)),
                  pl.BlockSpec((seq, w), lambda b, h: (b, COL_DV // w + h)),
                  pl.BlockSpec((group, 2, tq, tq), lambda b, h: (h, 0, 0, 0)),
                  pl.BlockSpec((1, 2 * DIFF_DH), lambda b, h: (0, 0))],
        out_specs=pl.BlockSpec((seq, w), lambda b, h: (b, h)),
        out_shape=jax.ShapeDtypeStruct((t, DIFF_W), BF16),
        scratch_shapes=[pltpu.VMEM((group, nq, 2 * DIFF_DH, tq), BF16)]
        + _softmax_scratch(group, 2 * DIFF_DH, tq, 2 * tq),
        compiler_params=_cparams("parallel", "parallel"),
        name="diff_attention",
    )(lam, proj, proj, proj, tab, g)


def _mla_kernel(qt_ref, k_ref, vt_ref, mask_ref, o_ref, s_ref, smax_ref, m_ref, acc_ref, *, tq, group):
    hslice = [slice(g * LANES, (g + 1) * LANES) for g in range(group)]
    vslice = [slice(g * MLA_V, (g + 1) * MLA_V) for g in range(group)]
    state = [(m_ref.at[g], acc_ref.at[g]) for g in range(group)]
    for g in range(group):
        _softmax_reset(*state[g])

    def logits(g, qi, j, slot, near):
        s = _dot(k_ref[_tile(j, tq), hslice[g]], qt_ref[qi, hslice[g], :])
        if near:
            s = s + mask_ref[...]
        _store_logits(s, s_ref.at[slot, g], smax_ref.at[slot, g])

    def softmax(g, j, slot):
        _softmax_update(s_ref.at[slot, g], smax_ref.at[slot, g], vt_ref[j, vslice[g], :], *state[g])

    def finalize(qi):
        o = jnp.concatenate([_softmax_result(acc) for _, acc in state], axis=0)
        o_ref[_tile(qi, tq), :] = o.T.astype(o_ref.dtype)
        for g in range(group):
            _softmax_reset(*state[g])

    _causal_tile_pipeline(vt_ref.shape[0], group, logits, softmax, finalize, near_tiles=1,
                          interleave=False)


def _mla_attention(qt, km, vt, mask, *, batch, seq, tq, group):
    t = km.shape[0]
    nq = seq // tq
    return pl.pallas_call(
        functools.partial(_mla_kernel, tq=tq, group=group),
        grid=(batch, MLA_HEADS // group),
        in_specs=[pl.BlockSpec((nq, group * LANES, tq), lambda b, h: (b, h, 0)),
                  pl.BlockSpec((seq, group * LANES), lambda b, h: (b, h)),
                  pl.BlockSpec((nq, group * MLA_V, tq), lambda b, h: (b, h, 0)),
                  pl.BlockSpec((tq, tq), lambda b, h: (0, 0))],
        out_specs=pl.BlockSpec((seq, group * MLA_V), lambda b, h: (b, h)),
        out_shape=jax.ShapeDtypeStruct((t, MLA_V_W), BF16),
        scratch_shapes=_softmax_scratch(group, MLA_V, tq, tq),
        compiler_params=_cparams("parallel", "parallel"),
        name="mla_attention",
    )(qt, km, vt, mask)


def _merge_kernel(x_ref, a_ref, b_ref, c_ref, g_ref, wg_ref, wa_ref, wb_ref, wc_ref, wo_ref, pg_ref, o_ref):
    x = x_ref[...]
    d = x.shape[1]
    h = (_rms(x) * g_ref[...]).astype(BF16)
    merged = None
    for n, (br_ref, w_ref) in enumerate(((a_ref, wa_ref), (b_ref, wb_ref), (c_ref, wc_ref))):
        gate = jax.nn.sigmoid(_dot(h, wg_ref[:, n * d:(n + 1) * d]))
        part = gate * _dot(br_ref[...], w_ref[...])
        merged = part if merged is None else merged + part
    y = _dot(merged.astype(BF16), wo_ref[...])
    o_ref[...] = x + _rms(y) * pg_ref[...]


def _merge(x2, a, b, c, g, w_gates, wa, wb, wc, wo, layer, pg, *, tm):
    t, d = x2.shape
    rowblk = pl.BlockSpec((tm, d), lambda i: (i, 0))
    return pl.pallas_call(
        _merge_kernel,
        grid=(t // tm,),
        in_specs=[rowblk, rowblk, rowblk, rowblk, _resident((1, d))]
        + [_layer_slab(w, layer) for w in (w_gates, wa, wb, wc, wo)] + [_resident((1, d))],
        out_specs=rowblk,
        out_shape=jax.ShapeDtypeStruct((t, d), F32),
        compiler_params=_cparams("parallel"),
        name="merge",
    )(x2, a, b, c, g, w_gates, wa, wb, wc, wo, pg)


def _mlp_kernel(x_ref, g_ref, wu_ref, wd_ref, pg_ref, o_ref, *, tf):
    x = x_ref[...]
    h = (_rms(x) * g_ref[...]).astype(BF16)
    acc = None
    for c in range(wu_ref.shape[1] // tf):
        u = jnp.maximum(_dot(h, wu_ref[:, c * tf:(c + 1) * tf]), 0.0)
        part = _dot((u * u).astype(BF16), wd_ref[c * tf:(c + 1) * tf, :])
        acc = part if acc is None else acc + part
    o_ref[...] = x + _rms(acc) * pg_ref[...]


def _mlp(x2, g, wu, wd, layer, pg, *, tm, tf):
    t, d = x2.shape
    rowblk = pl.BlockSpec((tm, d), lambda i: (i, 0))
    return pl.pallas_call(
        functools.partial(_mlp_kernel, tf=tf),
        grid=(t // tm,),
        in_specs=[rowblk, _resident((1, d)), _layer_slab(wu, layer), _layer_slab(wd, layer), _resident((1, d))],
        out_specs=rowblk,
        out_shape=jax.ShapeDtypeStruct((t, d), F32),
        compiler_params=_cparams("parallel"),
        name="mlp",
    )(x2, g, wu, wd, pg)


def _rot_half_cols(w):
    half = w.shape[-1] // 2
    return jnp.concatenate([-w[..., half:], w[..., :half]], axis=-1)


def _cast_transposed_kernel(wt_ref, o_ref):
    o_ref[...] = wt_ref[...].T.astype(o_ref.dtype)


def _cast_transposed(wt, n, *, tn):
    depth, _, d = wt.shape
    return pl.pallas_call(
        _cast_transposed_kernel,
        grid=(depth, n // tn),
        in_specs=[pl.BlockSpec((None, tn, d), lambda l, c: (l, c, 0))],
        out_specs=pl.BlockSpec((None, d, tn), lambda l, c: (l, 0, c)),
        out_shape=jax.ShapeDtypeStruct((depth, d, n), BF16),
        compiler_params=_cparams("parallel", "parallel"),
        name="w_in_prep",
    )(wt)


def _prep_w_in(w_in, *, tn):
    wt = jnp.swapaxes(w_in, 1, 2)
    kpe_at = N_IN_MAIN + MLA_Q_LORA + MLA_KV_LORA
    gates_at = kpe_at + MLA_ROPE
    kpe = wt[:, kpe_at:gates_at]
    half = MLA_ROPE // 2
    zeros = jnp.zeros((wt.shape[0], MLA_NOPE, wt.shape[2]), wt.dtype)
    small = jnp.concatenate([wt[:, N_IN_MAIN:kpe_at], zeros, kpe, -kpe[:, half:], kpe[:, :half]], axis=1)
    return (_cast_transposed(wt, N_IN_MAIN, tn=tn),
            _cast_transposed(wt[:, gates_at:], N_BRANCH * D_MODEL, tn=tn),
            _cast_transposed(small, N_IN_SMALL, tn=tn))


def _in_colscale():
    cs = np.ones((1, N_IN_MAIN), np.float32)
    cs[0, COL_DQ:COL_DQ + DIFF_W] = DIFF_DH ** -0.5 * LOG2E
    cs[0, COL_RK:COL_RK + RET_QK_W] = RET_DK ** -0.5
    return jnp.asarray(cs)


def _prep_w_uq(w):
    w = w.reshape(MLA_Q_LORA, MLA_HEADS, MLA_QH)
    pe = w[..., MLA_NOPE:]
    w = jnp.concatenate([w, _rot_half_cols(pe)], axis=-1).reshape(MLA_Q_LORA, MLA_PAD_W)
    return w.T.astype(BF16)


def _prep_w_ukv(w):
    w = w.reshape(MLA_KV_LORA, MLA_HEADS, MLA_KVH)
    wk = jnp.concatenate([w[..., :MLA_NOPE], jnp.zeros_like(w[..., :MLA_NOPE])], axis=-1)
    return (wk.reshape(MLA_KV_LORA, MLA_PAD_W).astype(BF16),
            w[..., MLA_NOPE:].reshape(MLA_KV_LORA, MLA_V_W).T.astype(BF16))


def _ret_tables(seq):
    pos = np.arange(seq, dtype=np.float64)
    inv_freq = 1.0 / (10000.0 ** np.linspace(0.0, 1.0, RET_DK // 2))
    ang = pos[:, None] * inv_freq[None, :]
    cos, sin = np.cos(ang), np.sin(ang)
    cos_t = np.concatenate([cos, cos], axis=-1)
    sin_t = np.concatenate([-sin, sin], axis=-1)
    log_g = np.log1p(-np.exp2(-5.0 - np.arange(RET_HEADS, dtype=np.float64)))
    idx = np.arange(RET_CHUNK, dtype=np.float64)
    rel = idx[:, None] - idx[None, :]
    dmask = np.where(rel >= 0, np.exp(np.maximum(rel, 0.0)[None] * log_g[:, None, None]), 0.0)
    xi = np.exp((idx + 1.0)[None, :] * log_g[:, None])[:, :, None]
    zeta = np.exp((RET_CHUNK - 1.0 - idx)[None, :] * log_g[:, None])[:, :, None]
    g_chunk = np.exp(RET_CHUNK * log_g)[:, None, None]
    tabs = (cos_t, sin_t, dmask,
            np.broadcast_to(xi, (RET_HEADS, RET_CHUNK, RET_DK)),
            np.broadcast_to(zeta, (RET_HEADS, RET_CHUNK, RET_DK)),
            np.broadcast_to(g_chunk, (RET_HEADS, 1, RET_DV)))
    return tuple(jnp.asarray(t, F32) for t in tabs)


def _mla_tables(seq):
    pos = np.arange(seq, dtype=np.float64)
    inv_freq = 1.0 / (ROPE_THETA ** (np.arange(0, MLA_ROPE, 2, dtype=np.float64) / MLA_ROPE))
    ang = pos[:, None] * inv_freq[None, :]
    cos, sin = np.cos(ang), np.sin(ang)
    z_nope = np.zeros((seq, MLA_NOPE))
    z_rope = np.zeros((seq, MLA_ROPE))
    cos_k = np.concatenate([z_nope, cos, cos, z_rope], axis=-1)
    sin_k = np.concatenate([z_nope, sin, sin, z_rope], axis=-1)
    qscale = MLA_QH ** -0.5 * LOG2E
    cos_q = np.concatenate([np.ones((seq, MLA_NOPE)), cos, cos, z_rope], axis=-1) * qscale
    sin_q = sin_k * qscale
    return tuple(jnp.asarray(t, F32) for t in (cos_q.T, sin_q.T, cos_k, sin_k))


def _t5_bucket(n):
    max_exact = REL_BUCKETS // 2
    nf = jnp.maximum(n, 1).astype(F32)
    large = max_exact + (jnp.log(nf / max_exact) / math.log(REL_MAX_DIST / max_exact)
                         * (REL_BUCKETS - max_exact)).astype(jnp.int32)
    large = jnp.minimum(large, REL_BUCKETS - 1)
    return jnp.where(n < max_exact, n, large)


def _toeplitz(g, tq):
    h = g.shape[0]
    padded = jnp.concatenate([g, jnp.zeros((h, 1), g.dtype)], axis=1)
    skewed = jnp.tile(padded, (1, tq))[:, :tq * (2 * tq - 1)].reshape(h, tq, 2 * tq - 1)
    return skewed[:, :, tq - 1:]


def _bias_table(rel_bias, tq):
    assert tq >= REL_MAX_DIST
    bias = rel_bias[_t5_bucket(jnp.arange(2 * tq))].astype(F32).T
    bias = (bias - rel_bias[REL_BUCKETS - 1].astype(F32)[:, None]) * LOG2E
    masked = jnp.full((bias.shape[0], tq - 1), NEG_BIG, F32)
    diagonal = jnp.concatenate([masked, bias[:, :tq]], axis=1)
    behind = bias[:, 1:]
    return jnp.stack([_toeplitz(diagonal, tq), _toeplitz(behind, tq)], axis=1)


def _causal_table(tq):
    kk = np.arange(tq)[:, None]
    qq = np.arange(tq)[None, :]
    return jnp.asarray(np.where(qq >= kk, 0.0, NEG_BIG), F32)


def _pick(t, want):
    return want if t % want == 0 else t


def kernel(x, rel_bias, pre_mix_g, w_in, w_ret_o, lambda_q1, lambda_k1, lambda_q2, lambda_k2, diff_subln_g, w_diff_o, mla_q_norm_g, w_mla_uq, mla_kv_norm_g, w_mla_ukv, w_mla_o, w_out, post_mix_g, pre_mlp_g, w_up, w_down, post_mlp_g):
    batch, seq, d = x.shape
    t = batch * seq
    depth = w_in.shape[0]
    tq = _pick(seq, 256)
    ret_tabs = _ret_tables(seq)
    mla_tabs = _mla_tables(seq)
    bias_tab = _bias_table(rel_bias, tq)
    causal_tab = _causal_table(tq)
    colscale = _in_colscale()
    w_main, w_gates, w_small = _prep_w_in(w_in, tn=512)
    merge_w = tuple(w.astype(BF16) for w in (w_ret_o, w_diff_o, w_mla_o, w_out))
    mlp_w = (w_up.astype(BF16), w_down.astype(BF16))
    row = lambda v: v.reshape(1, -1).astype(F32)

    x2 = x.reshape(t, d)
    for l in range(depth):
        wk, wvt = _prep_w_ukv(w_mla_ukv[l])
        proj, qt, km, vt = _in_proj(x2, row(pre_mix_g[l]), w_main, w_small, l, colscale,
                                    row(mla_q_norm_g[l]), row(mla_kv_norm_g[l]),
                                    _prep_w_uq(w_mla_uq[l]), wk, wvt, mla_tabs,
                                    seq=seq, tm=_pick(seq, 512), tn=512, tk=tq)

        ret = _retention(proj, ret_tabs, batch=batch, seq=seq)

        lambda_init = 0.8 - 0.6 * math.exp(-0.3 * l)
        lam = (jnp.exp(jnp.sum(lambda_q1[l] * lambda_k1[l]).astype(F32))
               - jnp.exp(jnp.sum(lambda_q2[l] * lambda_k2[l]).astype(F32)) + lambda_init)
        da = _diff_attention(proj, lam.reshape(1), bias_tab, row(diff_subln_g[l]) * (1.0 - lambda_init),
                             batch=batch, seq=seq, tq=tq, group=4)

        mo = _mla_attention(qt, km, vt, causal_tab, batch=batch, seq=seq, tq=tq, group=8)

        x2 = _merge(x2, ret, da, mo, row(pre_mix_g[l]), w_gates, *merge_w, l, row(post_mix_g[l]),
                    tm=_pick(t, 512))

        x2 = _mlp(x2, row(pre_mlp_g[l]), *mlp_w, l, row(post_mlp_g[l]), tm=_pick(t, 1024), tf=1024)
    return x2.reshape(batch, seq, d)
```

```python
import functools
import math

import jax
import jax.numpy as jnp
import numpy as np
from jax import lax
from jax.experimental import pallas as pl
from jax.experimental.pallas import tpu as pltpu

F32 = jnp.float32
BF16 = jnp.bfloat16

D_MODEL = 1024
EPS = 1e-6
RET_HEADS = 4
RET_DK = 128
RET_DV = 256
RET_CHUNK = 128
DIFF_HEADS = 8
DIFF_DH = 64
MLA_HEADS = 16
MLA_Q_LORA = 256
MLA_KV_LORA = 128
MLA_NOPE = 64
MLA_ROPE = 32
MLA_V = 64
ROPE_THETA = 10000.0
REL_BUCKETS = 32
REL_MAX_DIST = 128
D_FF = 4 * D_MODEL
N_BRANCH = 3

RET_QK_W = RET_HEADS * RET_DK
RET_V_W = RET_HEADS * RET_DV
DIFF_W = DIFF_HEADS * 2 * DIFF_DH
MLA_QH = MLA_NOPE + MLA_ROPE
MLA_KVH = MLA_NOPE + MLA_V
MLA_V_W = MLA_HEADS * MLA_V
MLA_PAD_W = MLA_HEADS * 128
IN_SPLITS = (RET_QK_W, RET_QK_W, RET_V_W, RET_V_W, DIFF_W, DIFF_W, DIFF_W,
             MLA_Q_LORA, MLA_KV_LORA, MLA_ROPE, N_BRANCH * D_MODEL)

LANES = 128
LOG2E = math.log2(math.e)
NEG_BIG = -1e30
VMEM_LIMIT = 56 * 1024 * 1024

COL_RQ = 0
COL_RK = COL_RQ + RET_QK_W
COL_RV = COL_RK + RET_QK_W
COL_RG = COL_RV + RET_V_W
COL_DQ = COL_RG + RET_V_W
COL_DK = COL_DQ + DIFF_W
N_IN_MAIN = COL_DK + DIFF_W
N_IN_SMALL = MLA_Q_LORA + MLA_KV_LORA + LANES


def _cparams(*sem):
    return pltpu.CompilerParams(dimension_semantics=sem, vmem_limit_bytes=VMEM_LIMIT)


def _rms(xf):
    return xf * lax.rsqrt(jnp.mean(xf * xf, axis=-1, keepdims=True) + EPS)


def _dot(a, b):
    return jnp.dot(a, b, preferred_element_type=F32)


def _dot_nt(a, b):
    return lax.dot_general(a, b, (((1,), (1,)), ((), ())), preferred_element_type=F32)


HEADS_PER_UP = 4


def _in_proj_kernel(x_ref, g_ref, wm_ref, wdvt_ref, ws_ref, cs_ref, gq_ref, gkv_ref, wqt_ref, wk_ref, wvt_ref,
                    cqt_tab, sqt_tab, ck_tab, sk_tab, o_ref, dvt_ref, qt_ref, k_ref, vt_ref, *, tn):
    h = (_rms(x_ref[...]) * g_ref[...]).astype(BF16)

    small = _dot(h, ws_ref[...])
    hq = (_rms(small[:, :MLA_Q_LORA]) * gq_ref[...]).astype(BF16)
    hkv = (_rms(small[:, MLA_Q_LORA:MLA_Q_LORA + MLA_KV_LORA]) * gkv_ref[...]).astype(BF16)
    kpe = small[:, MLA_Q_LORA + MLA_KV_LORA:]
    kpe = kpe * ck_tab[...] + pltpu.roll(kpe, LANES - MLA_ROPE, 1) * sk_tab[...]
    tk = vt_ref.shape[-1]
    n_tiles = vt_ref.shape[0]
    for c in range(n_tiles):
        vt_ref[c] = _dot_nt(wvt_ref[...], hkv[c * tk:(c + 1) * tk]).astype(vt_ref.dtype)
    cqt = cqt_tab[...]
    sqt = sqt_tab[...]
    up = HEADS_PER_UP * LANES
    for hg in range(MLA_HEADS // HEADS_PER_UP):
        q_t = _dot_nt(wqt_ref[hg * up:(hg + 1) * up, :], hq)
        ka = _dot(hkv, wk_ref[:, hg * up:(hg + 1) * up])
        for hh in range(HEADS_PER_UP):
            hs = slice(hh * LANES, (hh + 1) * LANES)
            out = slice(hg * up + hh * LANES, hg * up + (hh + 1) * LANES)
            blk = q_t[hs]
            partner = jnp.concatenate([blk[MLA_ROPE:], blk[:MLA_ROPE]], axis=0)
            blk = blk * cqt + partner * sqt
            for c in range(n_tiles):
                qt_ref[c, out, :] = blk[:, c * tk:(c + 1) * tk].astype(qt_ref.dtype)
            k_ref[:, out] = (ka[:, hs] + kpe).astype(k_ref.dtype)

    for c in range(wm_ref.shape[1] // tn):
        cols = slice(c * tn, (c + 1) * tn)
        o_ref[:, cols] = (_dot(h, wm_ref[:, cols]) * cs_ref[:, cols]).astype(o_ref.dtype)
    for r in range(wdvt_ref.shape[0] // tn):
        rows = slice(r * tn, (r + 1) * tn)
        dv_t = _dot_nt(wdvt_ref[rows, :], h)
        for c in range(n_tiles):
            dvt_ref[c, rows, :] = dv_t[:, c * tk:(c + 1) * tk].astype(dvt_ref.dtype)


def _resident(shape):
    return pl.BlockSpec(shape, lambda *_: (0,) * len(shape), pipeline_mode=pl.Buffered(1))


def _layer_slab(w, layer):
    return pl.BlockSpec((None,) + w.shape[1:], lambda *_: (layer, 0, 0), pipeline_mode=pl.Buffered(1))


def _in_proj(x2, g, w_main, w_dvt, w_small, layer, colscale, gq, gkv, wqt, wk, wvt, tabs, *, seq, tm, tn, tk):
    t, d = x2.shape
    n = w_main.shape[2]
    assert n % tn == 0 and w_dvt.shape[1] % tn == 0 and tm % tk == 0 and seq % tm == 0
    ns = seq // tm
    nc = tm // tk
    cqt, sqt, ck, sk = tabs
    tab_t = pl.BlockSpec((LANES, tm), lambda i: (0, i % ns))
    tab = pl.BlockSpec((tm, LANES), lambda i: (i % ns, 0))
    return pl.pallas_call(
        functools.partial(_in_proj_kernel, tn=tn),
        grid=(t // tm,),
        in_specs=[pl.BlockSpec((tm, d), lambda i: (i, 0)), _resident((1, d)),
                  _layer_slab(w_main, layer), _layer_slab(w_dvt, layer), _layer_slab(w_small, layer),
                  _resident((1, n)), _resident(gq.shape), _resident(gkv.shape),
                  _resident(wqt.shape), _resident(wk.shape), _resident(wvt.shape),
                  tab_t, tab_t, tab, tab],
        out_specs=[pl.BlockSpec((tm, n), lambda i: (i, 0)),
                   pl.BlockSpec((nc, w_dvt.shape[1], tk), lambda i: (i, 0, 0)),
                   pl.BlockSpec((nc, MLA_PAD_W, tk), lambda i: (i, 0, 0)),
                   pl.BlockSpec((tm, MLA_PAD_W), lambda i: (i, 0)),
                   pl.BlockSpec((nc, MLA_V_W, tk), lambda i: (i, 0, 0))],
        out_shape=[jax.ShapeDtypeStruct((t, n), BF16),
                   jax.ShapeDtypeStruct((t // tk, w_dvt.shape[1], tk), BF16),
                   jax.ShapeDtypeStruct((t // tk, MLA_PAD_W, tk), BF16),
                   jax.ShapeDtypeStruct((t, MLA_PAD_W), BF16),
                   jax.ShapeDtypeStruct((t // tk, MLA_V_W, tk), BF16)],
        compiler_params=_cparams("parallel"),
        name="in_proj",
    )(x2, g, w_main, w_dvt, w_small, colscale, gq, gkv, wqt, wk, wvt, cqt, sqt, ck, sk)


def _ret_kernel(q_ref, k_ref, v_ref, rg_ref, cos_ref, sin_ref, dm_ref, xi_ref, zt_ref, gc_ref,
                o_ref, st_ref):
    st_ref[...] = jnp.zeros_like(st_ref)

    def chunk(n, carry):
        rows = _tile(n, RET_CHUNK)
        cos = cos_ref[rows, :]
        sin = sin_ref[rows, :]
        for h in range(RET_HEADS):
            qs = slice(h * RET_DK, (h + 1) * RET_DK)
            vs = slice(h * RET_DV, (h + 1) * RET_DV)
            q = q_ref[rows, qs].astype(F32)
            k = k_ref[rows, qs].astype(F32)
            qr = q * cos + pltpu.roll(q, RET_DK // 2, 1) * sin
            kr = k * cos + pltpu.roll(k, RET_DK // 2, 1) * sin
            v = v_ref[rows, vs]
            st = st_ref[h]
            inner = _dot_nt(qr.astype(BF16), kr.astype(BF16)) * dm_ref[h]
            out = _dot(inner.astype(BF16), v) + _dot((qr * xi_ref[h]).astype(BF16), st.astype(BF16))
            kz_t = (kr * zt_ref[h]).T.astype(BF16)
            st_ref[h] = gc_ref[h] * st + _dot(kz_t, v)
            rg = rg_ref[rows, vs].astype(F32)
            o_ref[rows, vs] = (_rms(out) * (rg * jax.nn.sigmoid(rg))).astype(o_ref.dtype)
        return carry

    lax.fori_loop(0, q_ref.shape[0] // RET_CHUNK, chunk, 0)


def _retention(proj, tabs, *, batch, seq):
    t = proj.shape[0]
    c = RET_CHUNK
    cos, sin, dmask, xi, zeta, gch = tabs
    return pl.pallas_call(
        _ret_kernel,
        grid=(batch,),
        in_specs=[pl.BlockSpec((seq, RET_QK_W), lambda b: (b, COL_RQ // RET_QK_W)),
                  pl.BlockSpec((seq, RET_QK_W), lambda b: (b, COL_RK // RET_QK_W)),
                  pl.BlockSpec((seq, RET_V_W), lambda b: (b, COL_RV // RET_V_W)),
                  pl.BlockSpec((seq, RET_V_W), lambda b: (b, COL_RG // RET_V_W)),
                  _resident((seq, RET_DK)), _resident((seq, RET_DK)),
                  _resident((RET_HEADS, c, c)), _resident((RET_HEADS, c, RET_DK)),
                  _resident((RET_HEADS, c, RET_DK)), _resident((RET_HEADS, 1, RET_DV))],
        out_specs=pl.BlockSpec((seq, RET_V_W), lambda b: (b, 0)),
        out_shape=jax.ShapeDtypeStruct((t, RET_V_W), BF16),
        scratch_shapes=[pltpu.VMEM((RET_HEADS, RET_DK, RET_DV), F32)],
        compiler_params=_cparams("parallel"),
        name="retention",
    )(proj, proj, proj, proj, cos, sin, dmask, xi, zeta, gch)


SUM_ROWS = 16


def _store_logits(s, s_ref, smax_ref):
    s_ref[...] = s
    smax_ref[...] = jnp.max(s, axis=0, keepdims=True)


def _softmax_update(s_ref, smax_ref, vt, m_ref, acc_ref):
    m = m_ref[...]
    m_new = jnp.maximum(m, smax_ref[...])
    alpha = jnp.exp2(m - m_new)
    p = jnp.exp2(s_ref[...] - m_new).astype(BF16)
    m_ref[...] = m_new
    vt_ones = jnp.concatenate([vt, jnp.ones((SUM_ROWS, vt.shape[1]), BF16)], axis=0)
    acc_ref[...] = alpha * acc_ref[...] + _dot(vt_ones, p)


def _softmax_result(acc_ref):
    dv = acc_ref.shape[0] - SUM_ROWS
    return acc_ref[:dv, :] / acc_ref[dv:dv + 1, :]


def _softmax_reset(m_ref, acc_ref):
    m_ref[...] = jnp.full(m_ref.shape, NEG_BIG, F32)
    acc_ref[...] = jnp.zeros(acc_ref.shape, F32)


def _softmax_restart(m_ref):
    m_ref[...] = jnp.full(m_ref.shape, NEG_BIG, F32)


def _softmax_scratch(group, dv, tk, n):
    return [pltpu.VMEM((2, group, tk, n), F32), pltpu.VMEM((2, group, 1, n), F32),
            pltpu.VMEM((group, 1, n), F32), pltpu.VMEM((group, dv + SUM_ROWS, n), F32)]


def _tile(i, size):
    return pl.ds(pl.multiple_of(i * size, size), size)


def _causal_tile_pipeline(nq, group, logits, softmax, finalize, near_tiles, interleave):
    def step(qi, j, slot):
        last = j == qi
        nqi = jnp.where(last, qi + 1, qi)
        nj = jnp.where(last, 0, j + 1)
        nqi_valid = jnp.minimum(nqi, nq - 1)

        def issue(near):
            if interleave:
                for g in range(group):
                    softmax(g, j, slot)
                    logits(g, nqi_valid, nj, 1 - slot, near)
            else:
                for g in range(group):
                    logits(g, nqi_valid, nj, 1 - slot, near)
                for g in range(group):
                    softmax(g, j, slot)

        lax.cond(nqi_valid - nj < near_tiles, lambda: issue(True), lambda: issue(False))

        @pl.when(last)
        def _():
            finalize(qi)

        return nqi, nj

    n_pairs = nq * (nq + 1) // 2
    zero = jnp.int32(0)
    for g in range(group):
        logits(g, zero, zero, 0, True)
    carry = lax.fori_loop(0, n_pairs // 2, lambda _, c: step(*step(*c, 0), 1), (zero, zero))
    if n_pairs % 2:
        step(*carry, 0)


def _diff_kernel(lam_ref, q_ref, k_ref, vt_ref, tab_ref, gt_ref, o_ref, s_ref, smax_ref, m_ref, acc_ref,
                 *, tq, group):
    w = 2 * DIFF_DH
    nq = vt_ref.shape[0]
    hslice = [slice(g * w, (g + 1) * w) for g in range(group)]
    state = [(m_ref.at[g], acc_ref.at[g]) for g in range(group)]
    for g in range(group):
        _softmax_reset(*state[g])

    lane = lax.broadcasted_iota(jnp.int32, (1, w), 1)
    first_half = (lane < DIFF_DH).astype(BF16)
    second_half = (lane >= DIFF_DH).astype(BF16)

    def logits(g, qi, j, slot, near):
        q = q_ref[_tile(qi, tq), hslice[g]]
        q2 = jnp.concatenate([q * first_half, q * second_half], axis=0)
        s = _dot_nt(k_ref[_tile(j, tq), hslice[g]], q2)
        if near:
            bias = tab_ref[g, qi - j]
            s = s + jnp.concatenate([bias, bias], axis=1)
        _store_logits(s, s_ref.at[slot, g], smax_ref.at[slot, g])

    def softmax(g, j, slot):
        _softmax_update(s_ref.at[slot, g], smax_ref.at[slot, g], vt_ref[j, hslice[g], :], *state[g])

    def finalize(qi):
        for g in range(group):
            o = _softmax_result(state[g][1])
            a = o[:, :tq] - lam_ref[0] * o[:, tq:]
            a = a * lax.rsqrt(jnp.mean(a * a, axis=0, keepdims=True) + EPS) * gt_ref[...]
            o_ref[_tile(qi, tq), hslice[g]] = a.T.astype(o_ref.dtype)
            _softmax_restart(state[g][0])

    _causal_tile_pipeline(nq, group, logits, softmax, finalize, near_tiles=tab_ref.shape[1],
                          interleave=True)


def _diff_attention(proj, vt, lam, tab, g_t, *, batch, seq, tq, group):
    t = proj.shape[0]
    nq = seq // tq
    w = group * 2 * DIFF_DH
    return pl.pallas_call(
        functools.partial(_diff_kernel, tq=tq, group=group),
        grid=(batch, DIFF_HEADS // group),
        in_specs=[pl.BlockSpec(memory_space=pltpu.SMEM),
                  pl.BlockSpec((seq, w), lambda b, h: (b, COL_DQ // w + h)),
                  pl.BlockSpec((seq, w), lambda b, h: (b, COL_DK // w + h)),
                  pl.BlockSpec((nq, w, tq), lambda b, h: (b, h, 0)),
                  pl.BlockSpec((group, 2, tq, tq), lambda b, h: (h, 0, 0, 0)),
                  _resident(g_t.shape)],
        out_specs=pl.BlockSpec((seq, w), lambda b, h: (b, h)),
        out_shape=jax.ShapeDtypeStruct((t, DIFF_W), BF16),
        scratch_shapes=_softmax_scratch(group, 2 * DIFF_DH, tq, 2 * tq),
        compiler_params=_cparams("parallel", "parallel"),
        name="diff_attention",
    )(lam, proj, proj, vt, tab, g_t)


def _mla_kernel(qt_ref, k_ref, vt_ref, mask_ref, o_ref, s_ref, smax_ref, m_ref, acc_ref, *, tq, group):
    hslice = [slice(g * LANES, (g + 1) * LANES) for g in range(group)]
    vslice = [slice(g * MLA_V, (g + 1) * MLA_V) for g in range(group)]
    state = [(m_ref.at[g], acc_ref.at[g]) for g in range(group)]
    for g in range(group):
        _softmax_reset(*state[g])

    def logits(g, qi, j, slot, near):
        s = _dot(k_ref[_tile(j, tq), hslice[g]], qt_ref[qi, hslice[g], :])
        if near:
            s = s + mask_ref[...]
        _store_logits(s, s_ref.at[slot, g], smax_ref.at[slot, g])

    def softmax(g, j, slot):
        _softmax_update(s_ref.at[slot, g], smax_ref.at[slot, g], vt_ref[j, vslice[g], :], *state[g])

    def finalize(qi):
        o = jnp.concatenate([_softmax_result(acc) for _, acc in state], axis=0)
        o_ref[_tile(qi, tq), :] = o.T.astype(o_ref.dtype)
        for g in range(group):
            _softmax_restart(state[g][0])

    _causal_tile_pipeline(vt_ref.shape[0], group, logits, softmax, finalize, near_tiles=1,
                          interleave=False)


def _mla_attention(qt, km, vt, mask, *, batch, seq, tq, group):
    t = km.shape[0]
    nq = seq // tq
    return pl.pallas_call(
        functools.partial(_mla_kernel, tq=tq, group=group),
        grid=(batch, MLA_HEADS // group),
        in_specs=[pl.BlockSpec((nq, group * LANES, tq), lambda b, h: (b, h, 0)),
                  pl.BlockSpec((seq, group * LANES), lambda b, h: (b, h)),
                  pl.BlockSpec((nq, group * MLA_V, tq), lambda b, h: (b, h, 0)),
                  _resident(mask.shape)],
        out_specs=pl.BlockSpec((seq, group * MLA_V), lambda b, h: (b, h)),
        out_shape=jax.ShapeDtypeStruct((t, MLA_V_W), BF16),
        scratch_shapes=_softmax_scratch(group, MLA_V, tq, tq),
        compiler_params=_cparams("parallel", "parallel"),
        name="mla_attention",
    )(qt, km, vt, mask)


def _merge_kernel(x_ref, a_ref, b_ref, c_ref, g_ref, wg_ref, wa_ref, wb_ref, wc_ref, wo_ref, pg_ref, o_ref):
    x = x_ref[...]
    d = x.shape[1]
    h = (_rms(x) * g_ref[...]).astype(BF16)
    merged = None
    for n, (br_ref, w_ref) in enumerate(((a_ref, wa_ref), (b_ref, wb_ref), (c_ref, wc_ref))):
        gate = jax.nn.sigmoid(_dot(h, wg_ref[:, n * d:(n + 1) * d]))
        part = gate * _dot(br_ref[...], w_ref[...])
        merged = part if merged is None else merged + part
    y = _dot(merged.astype(BF16), wo_ref[...])
    o_ref[...] = x + _rms(y) * pg_ref[...]


def _merge(x2, a, b, c, g, w_gates, wa, wb, wc, wo, layer, pg, *, tm):
    t, d = x2.shape
    rowblk = pl.BlockSpec((tm, d), lambda i: (i, 0))
    return pl.pallas_call(
        _merge_kernel,
        grid=(t // tm,),
        in_specs=[rowblk, rowblk, rowblk, rowblk, _resident((1, d))]
        + [_layer_slab(w, layer) for w in (w_gates, wa, wb, wc, wo)] + [_resident((1, d))],
        out_specs=rowblk,
        out_shape=jax.ShapeDtypeStruct((t, d), F32),
        compiler_params=_cparams("parallel"),
        name="merge",
    )(x2, a, b, c, g, w_gates, wa, wb, wc, wo, pg)


def _mlp_kernel(x_ref, g_ref, wu_ref, wd_ref, pg_ref, o_ref, *, tf):
    x = x_ref[...]
    h = (_rms(x) * g_ref[...]).astype(BF16)
    acc = None
    for c in range(wu_ref.shape[1] // tf):
        u = jnp.maximum(_dot(h, wu_ref[:, c * tf:(c + 1) * tf]), 0.0)
        part = _dot((u * u).astype(BF16), wd_ref[c * tf:(c + 1) * tf, :])
        acc = part if acc is None else acc + part
    o_ref[...] = x + _rms(acc) * pg_ref[...]


def _mlp(x2, g, wu, wd, layer, pg, *, tm, tf):
    t, d = x2.shape
    rowblk = pl.BlockSpec((tm, d), lambda i: (i, 0))
    return pl.pallas_call(
        functools.partial(_mlp_kernel, tf=tf),
        grid=(t // tm,),
        in_specs=[rowblk, _resident((1, d)), _layer_slab(wu, layer), _layer_slab(wd, layer), _resident((1, d))],
        out_specs=rowblk,
        out_shape=jax.ShapeDtypeStruct((t, d), F32),
        compiler_params=_cparams("parallel"),
        name="mlp",
    )(x2, g, wu, wd, pg)


def _rot_half_cols(w):
    half = w.shape[-1] // 2
    return jnp.concatenate([-w[..., half:], w[..., :half]], axis=-1)


def _cast_transposed_kernel(wt_ref, o_ref):
    o_ref[...] = wt_ref[...].T.astype(o_ref.dtype)


def _cast_transposed(wt, n, *, tn):
    depth, _, d = wt.shape
    return pl.pallas_call(
        _cast_transposed_kernel,
        grid=(depth, n // tn),
        in_specs=[pl.BlockSpec((None, tn, d), lambda l, c: (l, c, 0))],
        out_specs=pl.BlockSpec((None, d, tn), lambda l, c: (l, 0, c)),
        out_shape=jax.ShapeDtypeStruct((depth, d, n), BF16),
        compiler_params=_cparams("parallel", "parallel"),
        name="w_in_prep",
    )(wt)


def _prep_w_in(w_in, *, tn):
    wt = jnp.swapaxes(w_in, 1, 2)
    cq_at = N_IN_MAIN + DIFF_W
    kpe_at = cq_at + MLA_Q_LORA + MLA_KV_LORA
    gates_at = kpe_at + MLA_ROPE
    kpe = wt[:, kpe_at:gates_at]
    half = MLA_ROPE // 2
    zeros = jnp.zeros((wt.shape[0], MLA_NOPE, wt.shape[2]), wt.dtype)
    small = jnp.concatenate([wt[:, cq_at:kpe_at], zeros, kpe, -kpe[:, half:], kpe[:, :half]], axis=1)
    return (_cast_transposed(wt, N_IN_MAIN, tn=tn),
            wt[:, N_IN_MAIN:cq_at].astype(BF16),
            _cast_transposed(wt[:, gates_at:], N_BRANCH * D_MODEL, tn=tn),
            _cast_transposed(small, N_IN_SMALL, tn=tn))


def _in_colscale():
    cs = np.ones((1, N_IN_MAIN), np.float32)
    cs[0, COL_DQ:COL_DQ + DIFF_W] = DIFF_DH ** -0.5 * LOG2E
    cs[0, COL_RK:COL_RK + RET_QK_W] = RET_DK ** -0.5
    return jnp.asarray(cs)


def _prep_w_uq(w):
    w = w.reshape(MLA_Q_LORA, MLA_HEADS, MLA_QH)
    pe = w[..., MLA_NOPE:]
    w = jnp.concatenate([w, _rot_half_cols(pe)], axis=-1).reshape(MLA_Q_LORA, MLA_PAD_W)
    return w.T.astype(BF16)


def _prep_w_ukv(w):
    w = w.reshape(MLA_KV_LORA, MLA_HEADS, MLA_KVH)
    wk = jnp.concatenate([w[..., :MLA_NOPE], jnp.zeros_like(w[..., :MLA_NOPE])], axis=-1)
    return (wk.reshape(MLA_KV_LORA, MLA_PAD_W).astype(BF16),
            w[..., MLA_NOPE:].reshape(MLA_KV_LORA, MLA_V_W).T.astype(BF16))


def _ret_tables(seq):
    pos = np.arange(seq, dtype=np.float64)
    inv_freq = 1.0 / (10000.0 ** np.linspace(0.0, 1.0, RET_DK // 2))
    ang = pos[:, None] * inv_freq[None, :]
    cos, sin = np.cos(ang), np.sin(ang)
    cos_t = np.concatenate([cos, cos], axis=-1)
    sin_t = np.concatenate([-sin, sin], axis=-1)
    log_g = np.log1p(-np.exp2(-5.0 - np.arange(RET_HEADS, dtype=np.float64)))
    idx = np.arange(RET_CHUNK, dtype=np.float64)
    rel = idx[:, None] - idx[None, :]
    dmask = np.where(rel >= 0, np.exp(np.maximum(rel, 0.0)[None] * log_g[:, None, None]), 0.0)
    xi = np.exp((idx + 1.0)[None, :] * log_g[:, None])[:, :, None]
    zeta = np.exp((RET_CHUNK - 1.0 - idx)[None, :] * log_g[:, None])[:, :, None]
    g_chunk = np.exp(RET_CHUNK * log_g)[:, None, None]
    tabs = (cos_t, sin_t, dmask,
            np.broadcast_to(xi, (RET_HEADS, RET_CHUNK, RET_DK)),
            np.broadcast_to(zeta, (RET_HEADS, RET_CHUNK, RET_DK)),
            np.broadcast_to(g_chunk, (RET_HEADS, 1, RET_DV)))
    return tuple(jnp.asarray(t, F32) for t in tabs)


def _mla_tables(seq):
    pos = np.arange(seq, dtype=np.float64)
    inv_freq = 1.0 / (ROPE_THETA ** (np.arange(0, MLA_ROPE, 2, dtype=np.float64) / MLA_ROPE))
    ang = pos[:, None] * inv_freq[None, :]
    cos, sin = np.cos(ang), np.sin(ang)
    z_nope = np.zeros((seq, MLA_NOPE))
    z_rope = np.zeros((seq, MLA_ROPE))
    cos_k = np.concatenate([z_nope, cos, cos, z_rope], axis=-1)
    sin_k = np.concatenate([z_nope, sin, sin, z_rope], axis=-1)
    qscale = MLA_QH ** -0.5 * LOG2E
    cos_q = np.concatenate([np.ones((seq, MLA_NOPE)), cos, cos, z_rope], axis=-1) * qscale
    sin_q = sin_k * qscale
    return tuple(jnp.asarray(t, F32) for t in (cos_q.T, sin_q.T, cos_k, sin_k))


def _t5_bucket(n):
    max_exact = REL_BUCKETS // 2
    nf = jnp.maximum(n, 1).astype(F32)
    large = max_exact + (jnp.log(nf / max_exact) / math.log(REL_MAX_DIST / max_exact)
                         * (REL_BUCKETS - max_exact)).astype(jnp.int32)
    large = jnp.minimum(large, REL_BUCKETS - 1)
    return jnp.where(n < max_exact, n, large)


def _toeplitz(g, tq):
    h = g.shape[0]
    padded = jnp.concatenate([g, jnp.zeros((h, 1), g.dtype)], axis=1)
    skewed = jnp.tile(padded, (1, tq))[:, :tq * (2 * tq - 1)].reshape(h, tq, 2 * tq - 1)
    return skewed[:, :, tq - 1:]


def _bias_table(rel_bias, tq):
    assert tq >= REL_MAX_DIST
    bias = rel_bias[_t5_bucket(jnp.arange(2 * tq))].astype(F32).T
    bias = (bias - rel_bias[REL_BUCKETS - 1].astype(F32)[:, None]) * LOG2E
    masked = jnp.full((bias.shape[0], tq - 1), NEG_BIG, F32)
    diagonal = jnp.concatenate([masked, bias[:, :tq]], axis=1)
    behind = bias[:, 1:]
    return jnp.stack([_toeplitz(diagonal, tq), _toeplitz(behind, tq)], axis=1)


def _causal_table(tq):
    kk = np.arange(tq)[:, None]
    qq = np.arange(tq)[None, :]
    return jnp.asarray(np.where(qq >= kk, 0.0, NEG_BIG), F32)


def _pick(t, want):
    return want if t % want == 0 else t


def kernel(x, rel_bias, pre_mix_g, w_in, w_ret_o, lambda_q1, lambda_k1, lambda_q2, lambda_k2, diff_subln_g, w_diff_o, mla_q_norm_g, w_mla_uq, mla_kv_norm_g, w_mla_ukv, w_mla_o, w_out, post_mix_g, pre_mlp_g, w_up, w_down, post_mlp_g):
    batch, seq, d = x.shape
    t = batch * seq
    depth = w_in.shape[0]
    tq = _pick(seq, 256)
    ret_tabs = _ret_tables(seq)
    mla_tabs = _mla_tables(seq)
    bias_tab = _bias_table(rel_bias, tq)
    causal_tab = _causal_table(tq)
    colscale = _in_colscale()
    w_main, w_dvt, w_gates, w_small = _prep_w_in(w_in, tn=512)
    merge_w = tuple(w.astype(BF16) for w in (w_ret_o, w_diff_o, w_mla_o, w_out))
    mlp_w = (w_up.astype(BF16), w_down.astype(BF16))
    row = lambda v: v.reshape(1, -1).astype(F32)

    x2 = x.reshape(t, d)
    for l in range(depth):
        wk, wvt = _prep_w_ukv(w_mla_ukv[l])
        proj, dvt, qt, km, vt = _in_proj(x2, row(pre_mix_g[l]), w_main, w_dvt, w_small, l, colscale,
                                         row(mla_q_norm_g[l]), row(mla_kv_norm_g[l]),
                                         _prep_w_uq(w_mla_uq[l]), wk, wvt, mla_tabs,
                                         seq=seq, tm=_pick(seq, 512), tn=512, tk=tq)

        ret = _retention(proj, ret_tabs, batch=batch, seq=seq)

        lambda_init = 0.8 - 0.6 * math.exp(-0.3 * l)
        lam = (jnp.exp(jnp.sum(lambda_q1[l] * lambda_k1[l]).astype(F32))
               - jnp.exp(jnp.sum(lambda_q2[l] * lambda_k2[l]).astype(F32)) + lambda_init)
        subln_g = jnp.broadcast_to((diff_subln_g[l].astype(F32) * (1.0 - lambda_init))[:, None], (2 * DIFF_DH, tq))
        da = _diff_attention(proj, dvt, lam.reshape(1), bias_tab, subln_g,
                             batch=batch, seq=seq, tq=tq, group=4)

        mo = _mla_attention(qt, km, vt, causal_tab, batch=batch, seq=seq, tq=tq, group=8)

        x2 = _merge(x2, ret, da, mo, row(pre_mix_g[l]), w_gates, *merge_w, l, row(post_mix_g[l]),
                    tm=_pick(t, 512))

        x2 = _mlp(x2, row(pre_mlp_g[l]), *mlp_w, l, row(post_mlp_g[l]), tm=_pick(t, 1024), tf=1024)
    return x2.reshape(batch, seq, d)
```

```python
import functools
import math

import jax
import jax.numpy as jnp
import numpy as np
from jax import lax
from jax.experimental import pallas as pl
from jax.experimental.pallas import tpu as pltpu

F32 = jnp.float32
BF16 = jnp.bfloat16

D_MODEL = 1024
EPS = 1e-6
RET_HEADS = 4
RET_DK = 128
RET_DV = 256
RET_CHUNK = 128
DIFF_HEADS = 8
DIFF_DH = 64
MLA_HEADS = 16
MLA_Q_LORA = 256
MLA_KV_LORA = 128
MLA_NOPE = 64
MLA_ROPE = 32
MLA_V = 64
ROPE_THETA = 10000.0
REL_BUCKETS = 32
REL_MAX_DIST = 128
D_FF = 4 * D_MODEL
N_BRANCH = 3

RET_QK_W = RET_HEADS * RET_DK
RET_V_W = RET_HEADS * RET_DV
DIFF_W = DIFF_HEADS * 2 * DIFF_DH
MLA_QH = MLA_NOPE + MLA_ROPE
MLA_KVH = MLA_NOPE + MLA_V
MLA_V_W = MLA_HEADS * MLA_V
MLA_PAD_W = MLA_HEADS * 128
IN_SPLITS = (RET_QK_W, RET_QK_W, RET_V_W, RET_V_W, DIFF_W, DIFF_W, DIFF_W,
             MLA_Q_LORA, MLA_KV_LORA, MLA_ROPE, N_BRANCH * D_MODEL)

LANES = 128
LOG2E = math.log2(math.e)
NEG_BIG = -1e30
VMEM_LIMIT = 56 * 1024 * 1024

COL_RQ = 0
COL_RK = COL_RQ + RET_QK_W
COL_RV = COL_RK + RET_QK_W
COL_RG = COL_RV + RET_V_W
COL_DQ = COL_RG + RET_V_W
COL_DK = COL_DQ + DIFF_W
N_IN_MAIN = COL_DK + DIFF_W
N_IN_SMALL = MLA_Q_LORA + MLA_KV_LORA + LANES


def _cparams(*sem):
    return pltpu.CompilerParams(dimension_semantics=sem, vmem_limit_bytes=VMEM_LIMIT)


def _rms(xf):
    return xf * lax.rsqrt(jnp.mean(xf * xf, axis=-1, keepdims=True) + EPS)


def _dot(a, b):
    return jnp.dot(a, b, preferred_element_type=F32)


def _dot_nt(a, b):
    return lax.dot_general(a, b, (((1,), (1,)), ((), ())), preferred_element_type=F32)


HEADS_PER_UP = 4


def _in_proj_kernel(x_ref, g_ref, wm_ref, wdvt_ref, ws_ref, cs_ref, gq_ref, gkv_ref, wqt_ref, wk_ref, wvt_ref,
                    cqt_tab, sqt_tab, ck_tab, sk_tab, o_ref, dvt_ref, qt_ref, k_ref, vt_ref, *, tn):
    h = (_rms(x_ref[...]) * g_ref[...]).astype(BF16)

    small = _dot(h, ws_ref[...])
    hq = (_rms(small[:, :MLA_Q_LORA]) * gq_ref[...]).astype(BF16)
    hkv = (_rms(small[:, MLA_Q_LORA:MLA_Q_LORA + MLA_KV_LORA]) * gkv_ref[...]).astype(BF16)
    kpe = small[:, MLA_Q_LORA + MLA_KV_LORA:]
    kpe = kpe * ck_tab[...] + pltpu.roll(kpe, LANES - MLA_ROPE, 1) * sk_tab[...]
    tk = vt_ref.shape[-1]
    n_tiles = vt_ref.shape[0]
    for c in range(n_tiles):
        vt_ref[c] = _dot_nt(wvt_ref[...], hkv[c * tk:(c + 1) * tk]).astype(vt_ref.dtype)
    cqt = cqt_tab[...]
    sqt = sqt_tab[...]
    up = HEADS_PER_UP * LANES
    for hg in range(MLA_HEADS // HEADS_PER_UP):
        q_t = _dot_nt(wqt_ref[hg * up:(hg + 1) * up, :], hq)
        ka = _dot(hkv, wk_ref[:, hg * up:(hg + 1) * up])
        for hh in range(HEADS_PER_UP):
            hs = slice(hh * LANES, (hh + 1) * LANES)
            out = slice(hg * up + hh * LANES, hg * up + (hh + 1) * LANES)
            blk = q_t[hs]
            partner = jnp.concatenate([blk[MLA_ROPE:], blk[:MLA_ROPE]], axis=0)
            blk = blk * cqt + partner * sqt
            for c in range(n_tiles):
                qt_ref[c, out, :] = blk[:, c * tk:(c + 1) * tk].astype(qt_ref.dtype)
            k_ref[:, out] = (ka[:, hs] + kpe).astype(k_ref.dtype)

    for c in range(wm_ref.shape[1] // tn):
        cols = slice(c * tn, (c + 1) * tn)
        o_ref[:, cols] = (_dot(h, wm_ref[:, cols]) * cs_ref[:, cols]).astype(o_ref.dtype)
    for r in range(wdvt_ref.shape[0] // tn):
        rows = slice(r * tn, (r + 1) * tn)
        dv_t = _dot_nt(wdvt_ref[rows, :], h)
        for c in range(n_tiles):
            dvt_ref[c, rows, :] = dv_t[:, c * tk:(c + 1) * tk].astype(dvt_ref.dtype)


def _resident(shape):
    return pl.BlockSpec(shape, lambda *_: (0,) * len(shape), pipeline_mode=pl.Buffered(1))


def _layer_slab(w, layer):
    return pl.BlockSpec((None,) + w.shape[1:], lambda *_: (layer, 0, 0), pipeline_mode=pl.Buffered(1))


def _in_proj(x2, g, w_main, w_dvt, w_small, layer, colscale, gq, gkv, wqt, wk, wvt, tabs, *, seq, tm, tn, tk):
    t, d = x2.shape
    n = w_main.shape[2]
    assert n % tn == 0 and w_dvt.shape[1] % tn == 0 and tm % tk == 0 and seq % tm == 0
    ns = seq // tm
    nc = tm // tk
    cqt, sqt, ck, sk = tabs
    tab_t = pl.BlockSpec((LANES, tm), lambda i: (0, i % ns))
    tab = pl.BlockSpec((tm, LANES), lambda i: (i % ns, 0))
    return pl.pallas_call(
        functools.partial(_in_proj_kernel, tn=tn),
        grid=(t // tm,),
        in_specs=[pl.BlockSpec((tm, d), lambda i: (i, 0)), _resident((1, d)),
                  _layer_slab(w_main, layer), _layer_slab(w_dvt, layer), _layer_slab(w_small, layer),
                  _resident((1, n)), _resident(gq.shape), _resident(gkv.shape),
                  _resident(wqt.shape), _resident(wk.shape), _resident(wvt.shape),
                  tab_t, tab_t, tab, tab],
        out_specs=[pl.BlockSpec((tm, n), lambda i: (i, 0)),
                   pl.BlockSpec((nc, w_dvt.shape[1], tk), lambda i: (i, 0, 0)),
                   pl.BlockSpec((nc, MLA_PAD_W, tk), lambda i: (i, 0, 0)),
                   pl.BlockSpec((tm, MLA_PAD_W), lambda i: (i, 0)),
                   pl.BlockSpec((nc, MLA_V_W, tk), lambda i: (i, 0, 0))],
        out_shape=[jax.ShapeDtypeStruct((t, n), BF16),
                   jax.ShapeDtypeStruct((t // tk, w_dvt.shape[1], tk), BF16),
                   jax.ShapeDtypeStruct((t // tk, MLA_PAD_W, tk), BF16),
                   jax.ShapeDtypeStruct((t, MLA_PAD_W), BF16),
                   jax.ShapeDtypeStruct((t // tk, MLA_V_W, tk), BF16)],
        compiler_params=_cparams("parallel"),
        name="in_proj",
    )(x2, g, w_main, w_dvt, w_small, colscale, gq, gkv, wqt, wk, wvt, cqt, sqt, ck, sk)


def _ret_kernel(q_ref, k_ref, v_ref, rg_ref, cos_ref, sin_ref, dm_ref, xi_ref, zt_ref, gc_ref,
                o_ref, st_ref):
    st_ref[...] = jnp.zeros_like(st_ref)

    def chunk(n, carry):
        rows = _tile(n, RET_CHUNK)
        cos = cos_ref[rows, :]
        sin = sin_ref[rows, :]
        for h in range(RET_HEADS):
            qs = slice(h * RET_DK, (h + 1) * RET_DK)
            vs = slice(h * RET_DV, (h + 1) * RET_DV)
            q = q_ref[rows, qs].astype(F32)
            k = k_ref[rows, qs].astype(F32)
            qr = q * cos + pltpu.roll(q, RET_DK // 2, 1) * sin
            kr = k * cos + pltpu.roll(k, RET_DK // 2, 1) * sin
            v = v_ref[rows, vs]
            st = st_ref[h]
            inner = _dot_nt(qr.astype(BF16), kr.astype(BF16)) * dm_ref[h]
            out = _dot(inner.astype(BF16), v) + _dot((qr * xi_ref[h]).astype(BF16), st.astype(BF16))
            kz_t = (kr * zt_ref[h]).T.astype(BF16)
            st_ref[h] = gc_ref[h] * st + _dot(kz_t, v)
            rg = rg_ref[rows, vs].astype(F32)
            o_ref[rows, vs] = (_rms(out) * (rg * jax.nn.sigmoid(rg))).astype(o_ref.dtype)
        return carry

    lax.fori_loop(0, q_ref.shape[0] // RET_CHUNK, chunk, 0)


def _retention(proj, tabs, *, batch, seq):
    t = proj.shape[0]
    c = RET_CHUNK
    cos, sin, dmask, xi, zeta, gch = tabs
    return pl.pallas_call(
        _ret_kernel,
        grid=(batch,),
        in_specs=[pl.BlockSpec((seq, RET_QK_W), lambda b: (b, COL_RQ // RET_QK_W)),
                  pl.BlockSpec((seq, RET_QK_W), lambda b: (b, COL_RK // RET_QK_W)),
                  pl.BlockSpec((seq, RET_V_W), lambda b: (b, COL_RV // RET_V_W)),
                  pl.BlockSpec((seq, RET_V_W), lambda b: (b, COL_RG // RET_V_W)),
                  _resident((seq, RET_DK)), _resident((seq, RET_DK)),
                  _resident((RET_HEADS, c, c)), _resident((RET_HEADS, c, RET_DK)),
                  _resident((RET_HEADS, c, RET_DK)), _resident((RET_HEADS, 1, RET_DV))],
        out_specs=pl.BlockSpec((seq, RET_V_W), lambda b: (b, 0)),
        out_shape=jax.ShapeDtypeStruct((t, RET_V_W), BF16),
        scratch_shapes=[pltpu.VMEM((RET_HEADS, RET_DK, RET_DV), F32)],
        compiler_params=_cparams("parallel"),
        name="retention",
    )(proj, proj, proj, proj, cos, sin, dmask, xi, zeta, gch)


SUM_ROWS = 16


def _store_logits(s, s_ref, smax_ref):
    s_ref[...] = s
    smax_ref[...] = jnp.max(s, axis=0, keepdims=True)


def _softmax_update(s_ref, smax_ref, vt, m_ref, acc_ref):
    m = m_ref[...]
    m_new = jnp.maximum(m, smax_ref[...])
    alpha = jnp.exp2(m - m_new)
    p = jnp.exp2(s_ref[...] - m_new).astype(BF16)
    m_ref[...] = m_new
    vt_ones = jnp.concatenate([vt, jnp.ones((SUM_ROWS, vt.shape[1]), BF16)], axis=0)
    acc_ref[...] = alpha * acc_ref[...] + _dot(vt_ones, p)


def _softmax_result(acc_ref):
    dv = acc_ref.shape[0] - SUM_ROWS
    return acc_ref[:dv, :] / acc_ref[dv:dv + 1, :]


def _softmax_reset(m_ref, acc_ref):
    m_ref[...] = jnp.full(m_ref.shape, NEG_BIG, F32)
    acc_ref[...] = jnp.zeros(acc_ref.shape, F32)


def _softmax_restart(m_ref):
    m_ref[...] = jnp.full(m_ref.shape, NEG_BIG, F32)


def _softmax_scratch(group, dv, tk, n):
    return [pltpu.VMEM((2, group, tk, n), F32), pltpu.VMEM((2, group, 1, n), F32),
            pltpu.VMEM((group, 1, n), F32), pltpu.VMEM((group, dv + SUM_ROWS, n), F32)]


def _tile(i, size):
    return pl.ds(pl.multiple_of(i * size, size), size)


def _causal_tile_pipeline(nq, group, logits, softmax, finalize, near_tiles, interleave):
    def step(qi, j, slot):
        last = j == qi
        nqi = jnp.where(last, qi + 1, qi)
        nj = jnp.where(last, 0, j + 1)
        nqi_valid = jnp.minimum(nqi, nq - 1)

        def issue(near):
            if interleave:
                for g in range(group):
                    softmax(g, j, slot)
                    logits(g, nqi_valid, nj, 1 - slot, near)
            else:
                for g in range(group):
                    logits(g, nqi_valid, nj, 1 - slot, near)
                for g in range(group):
                    softmax(g, j, slot)

        if near_tiles is None:
            issue(None)
        else:
            lax.cond(nqi_valid - nj < near_tiles, lambda: issue(True), lambda: issue(False))

        @pl.when(last)
        def _():
            finalize(qi)

        return nqi, nj

    n_pairs = nq * (nq + 1) // 2
    zero = jnp.int32(0)
    for g in range(group):
        logits(g, zero, zero, 0, True)
    carry = lax.fori_loop(0, n_pairs // 2, lambda _, c: step(*step(*c, 0), 1), (zero, zero))
    if n_pairs % 2:
        step(*carry, 0)


def _diff_kernel(lam_ref, q_ref, k_ref, vt_ref, tab_ref, gt_ref, o_ref, s_ref, smax_ref, m_ref, acc_ref,
                 *, tq, group):
    w = 2 * DIFF_DH
    nq = vt_ref.shape[0]
    hslice = [slice(g * w, (g + 1) * w) for g in range(group)]
    state = [(m_ref.at[g], acc_ref.at[g]) for g in range(group)]
    for g in range(group):
        _softmax_reset(*state[g])

    lane = lax.broadcasted_iota(jnp.int32, (1, w), 1)
    first_half = (lane < DIFF_DH).astype(BF16)
    second_half = (lane >= DIFF_DH).astype(BF16)

    def logits(g, qi, j, slot, near):
        q = q_ref[_tile(qi, tq), hslice[g]]
        q2 = jnp.concatenate([q * first_half, q * second_half], axis=0)
        s = _dot_nt(k_ref[_tile(j, tq), hslice[g]], q2)
        if near is None or near:
            bias = tab_ref[g, jnp.minimum(qi - j, tab_ref.shape[1] - 1)]
            s = s + jnp.concatenate([bias, bias], axis=1)
        _store_logits(s, s_ref.at[slot, g], smax_ref.at[slot, g])

    def softmax(g, j, slot):
        _softmax_update(s_ref.at[slot, g], smax_ref.at[slot, g], vt_ref[j, hslice[g], :], *state[g])

    def finalize(qi):
        for g in range(group):
            o = _softmax_result(state[g][1])
            a = o[:, :tq] - lam_ref[0] * o[:, tq:]
            a = a * lax.rsqrt(jnp.mean(a * a, axis=0, keepdims=True) + EPS) * gt_ref[...]
            o_ref[_tile(qi, tq), hslice[g]] = a.T.astype(o_ref.dtype)
            _softmax_restart(state[g][0])

    _causal_tile_pipeline(nq, group, logits, softmax, finalize, near_tiles=None,
                          interleave=True)


def _diff_attention(proj, vt, lam, tab, g_t, *, batch, seq, tq, group):
    t = proj.shape[0]
    nq = seq // tq
    w = group * 2 * DIFF_DH
    return pl.pallas_call(
        functools.partial(_diff_kernel, tq=tq, group=group),
        grid=(batch, DIFF_HEADS // group),
        in_specs=[pl.BlockSpec(memory_space=pltpu.SMEM),
                  pl.BlockSpec((seq, w), lambda b, h: (b, COL_DQ // w + h)),
                  pl.BlockSpec((seq, w), lambda b, h: (b, COL_DK // w + h)),
                  pl.BlockSpec((nq, w, tq), lambda b, h: (b, h, 0)),
                  pl.BlockSpec((group, 3, tq, tq), lambda b, h: (h, 0, 0, 0)),
                  _resident(g_t.shape)],
        out_specs=pl.BlockSpec((seq, w), lambda b, h: (b, h)),
        out_shape=jax.ShapeDtypeStruct((t, DIFF_W), BF16),
        scratch_shapes=_softmax_scratch(group, 2 * DIFF_DH, tq, 2 * tq),
        compiler_params=_cparams("parallel", "parallel"),
        name="diff_attention",
    )(lam, proj, proj, vt, tab, g_t)


def _mla_kernel(qt_ref, k_ref, vt_ref, mask_ref, o_ref, s_ref, smax_ref, m_ref, acc_ref, *, tq, group):
    hslice = [slice(g * LANES, (g + 1) * LANES) for g in range(group)]
    vslice = [slice(g * MLA_V, (g + 1) * MLA_V) for g in range(group)]
    state = [(m_ref.at[g], acc_ref.at[g]) for g in range(group)]
    for g in range(group):
        _softmax_reset(*state[g])

    def logits(g, qi, j, slot, near):
        s = _dot(k_ref[_tile(j, tq), hslice[g]], qt_ref[qi, hslice[g], :])
        if near is None or near:
            s = s + mask_ref[jnp.minimum(qi - j, 1)]
        _store_logits(s, s_ref.at[slot, g], smax_ref.at[slot, g])

    def softmax(g, j, slot):
        _softmax_update(s_ref.at[slot, g], smax_ref.at[slot, g], vt_ref[j, vslice[g], :], *state[g])

    def finalize(qi):
        o = jnp.concatenate([_softmax_result(acc) for _, acc in state], axis=0)
        o_ref[_tile(qi, tq), :] = o.T.astype(o_ref.dtype)
        for g in range(group):
            _softmax_restart(state[g][0])

    _causal_tile_pipeline(vt_ref.shape[0], group, logits, softmax, finalize, near_tiles=None,
                          interleave=False)


def _mla_attention(qt, km, vt, mask, *, batch, seq, tq, group):
    t = km.shape[0]
    nq = seq // tq
    return pl.pallas_call(
        functools.partial(_mla_kernel, tq=tq, group=group),
        grid=(batch, MLA_HEADS // group),
        in_specs=[pl.BlockSpec((nq, group * LANES, tq), lambda b, h: (b, h, 0)),
                  pl.BlockSpec((seq, group * LANES), lambda b, h: (b, h)),
                  pl.BlockSpec((nq, group * MLA_V, tq), lambda b, h: (b, h, 0)),
                  _resident(mask.shape)],
        out_specs=pl.BlockSpec((seq, group * MLA_V), lambda b, h: (b, h)),
        out_shape=jax.ShapeDtypeStruct((t, MLA_V_W), BF16),
        scratch_shapes=_softmax_scratch(group, MLA_V, tq, tq),
        compiler_params=_cparams("parallel", "parallel"),
        name="mla_attention",
    )(qt, km, vt, mask)


def _merge_kernel(x_ref, a_ref, b_ref, c_ref, g_ref, wg_ref, wa_ref, wb_ref, wc_ref, wo_ref, pg_ref, o_ref):
    x = x_ref[...]
    d = x.shape[1]
    h = (_rms(x) * g_ref[...]).astype(BF16)
    merged = None
    for n, (br_ref, w_ref) in enumerate(((a_ref, wa_ref), (b_ref, wb_ref), (c_ref, wc_ref))):
        gate = jax.nn.sigmoid(_dot(h, wg_ref[:, n * d:(n + 1) * d]))
        part = gate * _dot(br_ref[...], w_ref[...])
        merged = part if merged is None else merged + part
    y = _dot(merged.astype(BF16), wo_ref[...])
    o_ref[...] = x + _rms(y) * pg_ref[...]


def _merge(x2, a, b, c, g, w_gates, wa, wb, wc, wo, layer, pg, *, tm):
    t, d = x2.shape
    rowblk = pl.BlockSpec((tm, d), lambda i: (i, 0))
    return pl.pallas_call(
        _merge_kernel,
        grid=(t // tm,),
        in_specs=[rowblk, rowblk, rowblk, rowblk, _resident((1, d))]
        + [_layer_slab(w, layer) for w in (w_gates, wa, wb, wc, wo)] + [_resident((1, d))],
        out_specs=rowblk,
        out_shape=jax.ShapeDtypeStruct((t, d), F32),
        compiler_params=_cparams("parallel"),
        name="merge",
    )(x2, a, b, c, g, w_gates, wa, wb, wc, wo, pg)


def _mlp_kernel(x_ref, g_ref, wu_ref, wd_ref, pg_ref, o_ref, *, tf):
    x = x_ref[...]
    h = (_rms(x) * g_ref[...]).astype(BF16)
    acc = None
    for c in range(wu_ref.shape[1] // tf):
        u = jnp.maximum(_dot(h, wu_ref[:, c * tf:(c + 1) * tf]), 0.0)
        part = _dot((u * u).astype(BF16), wd_ref[c * tf:(c + 1) * tf, :])
        acc = part if acc is None else acc + part
    o_ref[...] = x + _rms(acc) * pg_ref[...]


def _mlp(x2, g, wu, wd, layer, pg, *, tm, tf):
    t, d = x2.shape
    rowblk = pl.BlockSpec((tm, d), lambda i: (i, 0))
    return pl.pallas_call(
        functools.partial(_mlp_kernel, tf=tf),
        grid=(t // tm,),
        in_specs=[rowblk, _resident((1, d)), _layer_slab(wu, layer), _layer_slab(wd, layer), _resident((1, d))],
        out_specs=rowblk,
        out_shape=jax.ShapeDtypeStruct((t, d), F32),
        compiler_params=_cparams("parallel"),
        name="mlp",
    )(x2, g, wu, wd, pg)


def _rot_half_cols(w):
    half = w.shape[-1] // 2
    return jnp.concatenate([-w[..., half:], w[..., :half]], axis=-1)


def _cast_transposed_kernel(wt_ref, o_ref):
    o_ref[...] = wt_ref[...].T.astype(o_ref.dtype)


def _cast_transposed(wt, n, *, tn):
    depth, _, d = wt.shape
    return pl.pallas_call(
        _cast_transposed_kernel,
        grid=(depth, n // tn),
        in_specs=[pl.BlockSpec((None, tn, d), lambda l, c: (l, c, 0))],
        out_specs=pl.BlockSpec((None, d, tn), lambda l, c: (l, 0, c)),
        out_shape=jax.ShapeDtypeStruct((depth, d, n), BF16),
        compiler_params=_cparams("parallel", "parallel"),
        name="w_in_prep",
    )(wt)


def _prep_w_in(w_in, *, tn):
    wt = jnp.swapaxes(w_in, 1, 2)
    cq_at = N_IN_MAIN + DIFF_W
    kpe_at = cq_at + MLA_Q_LORA + MLA_KV_LORA
    gates_at = kpe_at + MLA_ROPE
    kpe = wt[:, kpe_at:gates_at]
    half = MLA_ROPE // 2
    zeros = jnp.zeros((wt.shape[0], MLA_NOPE, wt.shape[2]), wt.dtype)
    small = jnp.concatenate([wt[:, cq_at:kpe_at], zeros, kpe, -kpe[:, half:], kpe[:, :half]], axis=1)
    return (_cast_transposed(wt, N_IN_MAIN, tn=tn),
            wt[:, N_IN_MAIN:cq_at].astype(BF16),
            _cast_transposed(wt[:, gates_at:], N_BRANCH * D_MODEL, tn=tn),
            _cast_transposed(small, N_IN_SMALL, tn=tn))


def _in_colscale():
    cs = np.ones((1, N_IN_MAIN), np.float32)
    cs[0, COL_DQ:COL_DQ + DIFF_W] = DIFF_DH ** -0.5 * LOG2E
    cs[0, COL_RK:COL_RK + RET_QK_W] = RET_DK ** -0.5
    return jnp.asarray(cs)


def _prep_w_uq(w):
    w = w.reshape(MLA_Q_LORA, MLA_HEADS, MLA_QH)
    pe = w[..., MLA_NOPE:]
    w = jnp.concatenate([w, _rot_half_cols(pe)], axis=-1).reshape(MLA_Q_LORA, MLA_PAD_W)
    return w.T.astype(BF16)


def _prep_w_ukv(w):
    w = w.reshape(MLA_KV_LORA, MLA_HEADS, MLA_KVH)
    wk = jnp.concatenate([w[..., :MLA_NOPE], jnp.zeros_like(w[..., :MLA_NOPE])], axis=-1)
    return (wk.reshape(MLA_KV_LORA, MLA_PAD_W).astype(BF16),
            w[..., MLA_NOPE:].reshape(MLA_KV_LORA, MLA_V_W).T.astype(BF16))


def _ret_tables(seq):
    pos = np.arange(seq, dtype=np.float64)
    inv_freq = 1.0 / (10000.0 ** np.linspace(0.0, 1.0, RET_DK // 2))
    ang = pos[:, None] * inv_freq[None, :]
    cos, sin = np.cos(ang), np.sin(ang)
    cos_t = np.concatenate([cos, cos], axis=-1)
    sin_t = np.concatenate([-sin, sin], axis=-1)
    log_g = np.log1p(-np.exp2(-5.0 - np.arange(RET_HEADS, dtype=np.float64)))
    idx = np.arange(RET_CHUNK, dtype=np.float64)
    rel = idx[:, None] - idx[None, :]
    dmask = np.where(rel >= 0, np.exp(np.maximum(rel, 0.0)[None] * log_g[:, None, None]), 0.0)
    xi = np.exp((idx + 1.0)[None, :] * log_g[:, None])[:, :, None]
    zeta = np.exp((RET_CHUNK - 1.0 - idx)[None, :] * log_g[:, None])[:, :, None]
    g_chunk = np.exp(RET_CHUNK * log_g)[:, None, None]
    tabs = (cos_t, sin_t, dmask,
            np.broadcast_to(xi, (RET_HEADS, RET_CHUNK, RET_DK)),
            np.broadcast_to(zeta, (RET_HEADS, RET_CHUNK, RET_DK)),
            np.broadcast_to(g_chunk, (RET_HEADS, 1, RET_DV)))
    return tuple(jnp.asarray(t, F32) for t in tabs)


def _mla_tables(seq):
    pos = np.arange(seq, dtype=np.float64)
    inv_freq = 1.0 / (ROPE_THETA ** (np.arange(0, MLA_ROPE, 2, dtype=np.float64) / MLA_ROPE))
    ang = pos[:, None] * inv_freq[None, :]
    cos, sin = np.cos(ang), np.sin(ang)
    z_nope = np.zeros((seq, MLA_NOPE))
    z_rope = np.zeros((seq, MLA_ROPE))
    cos_k = np.concatenate([z_nope, cos, cos, z_rope], axis=-1)
    sin_k = np.concatenate([z_nope, sin, sin, z_rope], axis=-1)
    qscale = MLA_QH ** -0.5 * LOG2E
    cos_q = np.concatenate([np.ones((seq, MLA_NOPE)), cos, cos, z_rope], axis=-1) * qscale
    sin_q = sin_k * qscale
    return tuple(jnp.asarray(t, F32) for t in (cos_q.T, sin_q.T, cos_k, sin_k))


def _t5_bucket(n):
    max_exact = REL_BUCKETS // 2
    nf = jnp.maximum(n, 1).astype(F32)
    large = max_exact + (jnp.log(nf / max_exact) / math.log(REL_MAX_DIST / max_exact)
                         * (REL_BUCKETS - max_exact)).astype(jnp.int32)
    large = jnp.minimum(large, REL_BUCKETS - 1)
    return jnp.where(n < max_exact, n, large)


def _toeplitz(g, tq):
    h = g.shape[0]
    padded = jnp.concatenate([g, jnp.zeros((h, 1), g.dtype)], axis=1)
    skewed = jnp.tile(padded, (1, tq))[:, :tq * (2 * tq - 1)].reshape(h, tq, 2 * tq - 1)
    return skewed[:, :, tq - 1:]


def _bias_table(rel_bias, tq):
    assert tq >= REL_MAX_DIST
    bias = rel_bias[_t5_bucket(jnp.arange(2 * tq))].astype(F32).T
    bias = (bias - rel_bias[REL_BUCKETS - 1].astype(F32)[:, None]) * LOG2E
    masked = jnp.full((bias.shape[0], tq - 1), NEG_BIG, F32)
    diagonal = jnp.concatenate([masked, bias[:, :tq]], axis=1)
    behind = bias[:, 1:]
    zeros = jnp.zeros((bias.shape[0], tq, tq), F32)
    return jnp.stack([_toeplitz(diagonal, tq), _toeplitz(behind, tq), zeros], axis=1)


def _causal_table(tq):
    kk = np.arange(tq)[:, None]
    qq = np.arange(tq)[None, :]
    return jnp.asarray(np.stack([np.where(qq >= kk, 0.0, NEG_BIG), np.zeros((tq, tq))]), F32)


def _pick(t, want):
    return want if t % want == 0 else t


def kernel(x, rel_bias, pre_mix_g, w_in, w_ret_o, lambda_q1, lambda_k1, lambda_q2, lambda_k2, diff_subln_g, w_diff_o, mla_q_norm_g, w_mla_uq, mla_kv_norm_g, w_mla_ukv, w_mla_o, w_out, post_mix_g, pre_mlp_g, w_up, w_down, post_mlp_g):
    batch, seq, d = x.shape
    t = batch * seq
    depth = w_in.shape[0]
    tq = _pick(seq, 256)
    ret_tabs = _ret_tables(seq)
    mla_tabs = _mla_tables(seq)
    bias_tab = _bias_table(rel_bias, tq)
    causal_tab = _causal_table(tq)
    colscale = _in_colscale()
    w_main, w_dvt, w_gates, w_small = _prep_w_in(w_in, tn=512)
    merge_w = tuple(w.astype(BF16) for w in (w_ret_o, w_diff_o, w_mla_o, w_out))
    mlp_w = (w_up.astype(BF16), w_down.astype(BF16))
    row = lambda v: v.reshape(1, -1).astype(F32)

    x2 = x.reshape(t, d)
    for l in range(depth):
        wk, wvt = _prep_w_ukv(w_mla_ukv[l])
        proj, dvt, qt, km, vt = _in_proj(x2, row(pre_mix_g[l]), w_main, w_dvt, w_small, l, colscale,
                                         row(mla_q_norm_g[l]), row(mla_kv_norm_g[l]),
                                         _prep_w_uq(w_mla_uq[l]), wk, wvt, mla_tabs,
                                         seq=seq, tm=_pick(seq, 512), tn=512, tk=tq)

        ret = _retention(proj, ret_tabs, batch=batch, seq=seq)

        lambda_init = 0.8 - 0.6 * math.exp(-0.3 * l)
        lam = (jnp.exp(jnp.sum(lambda_q1[l] * lambda_k1[l]).astype(F32))
               - jnp.exp(jnp.sum(lambda_q2[l] * lambda_k2[l]).astype(F32)) + lambda_init)
        subln_g = jnp.broadcast_to((diff_subln_g[l].astype(F32) * (1.0 - lambda_init))[:, None], (2 * DIFF_DH, tq))
        da = _diff_attention(proj, dvt, lam.reshape(1), bias_tab, subln_g,
                             batch=batch, seq=seq, tq=tq, group=4)

        mo = _mla_attention(qt, km, vt, causal_tab, batch=batch, seq=seq, tq=tq, group=8)

        x2 = _merge(x2, ret, da, mo, row(pre_mix_g[l]), w_gates, *merge_w, l, row(post_mix_g[l]),
                    tm=_pick(t, 512))

        x2 = _mlp(x2, row(pre_mlp_g[l]), *mlp_w, l, row(post_mlp_g[l]), tm=_pick(t, 1024), tf=1024)
    return x2.reshape(batch, seq, d)
```

```python
import functools
import math

import jax
import jax.numpy as jnp
import numpy as np
from jax import lax
from jax.experimental import pallas as pl
from jax.experimental.pallas import tpu as pltpu

F32 = jnp.float32
BF16 = jnp.bfloat16

D_MODEL = 1024
EPS = 1e-6
RET_HEADS = 4
RET_DK = 128
RET_DV = 256
RET_CHUNK = 128
DIFF_HEADS = 8
DIFF_DH = 64
MLA_HEADS = 16
MLA_Q_LORA = 256
MLA_KV_LORA = 128
MLA_NOPE = 64
MLA_ROPE = 32
MLA_V = 64
ROPE_THETA = 10000.0
REL_BUCKETS = 32
REL_MAX_DIST = 128
D_FF = 4 * D_MODEL
N_BRANCH = 3

RET_QK_W = RET_HEADS * RET_DK
RET_V_W = RET_HEADS * RET_DV
DIFF_W = DIFF_HEADS * 2 * DIFF_DH
MLA_QH = MLA_NOPE + MLA_ROPE
MLA_KVH = MLA_NOPE + MLA_V
MLA_V_W = MLA_HEADS * MLA_V
MLA_PAD_W = MLA_HEADS * 128
IN_SPLITS = (RET_QK_W, RET_QK_W, RET_V_W, RET_V_W, DIFF_W, DIFF_W, DIFF_W,
             MLA_Q_LORA, MLA_KV_LORA, MLA_ROPE, N_BRANCH * D_MODEL)

LANES = 128
LOG2E = math.log2(math.e)
NEG_BIG = -1e30
VMEM_LIMIT = 56 * 1024 * 1024

COL_RQ = 0
COL_RK = COL_RQ + RET_QK_W
COL_RV = COL_RK + RET_QK_W
COL_RG = COL_RV + RET_V_W
COL_DK = COL_RG + RET_V_W
N_IN_MAIN = COL_DK + DIFF_W
W_IN_DQ = COL_DK
W_IN_DK = W_IN_DQ + DIFF_W
W_IN_DV = W_IN_DK + DIFF_W
DIFF_Q_SCALE = DIFF_DH ** -0.5 * LOG2E
N_IN_SMALL = MLA_Q_LORA + MLA_KV_LORA + LANES


def _cparams(*sem):
    return pltpu.CompilerParams(dimension_semantics=sem, vmem_limit_bytes=VMEM_LIMIT)


def _rms(xf):
    return xf * lax.rsqrt(jnp.mean(xf * xf, axis=-1, keepdims=True) + EPS)


def _dot(a, b):
    return jnp.dot(a, b, preferred_element_type=F32)


def _dot_nt(a, b):
    return lax.dot_general(a, b, (((1,), (1,)), ((), ())), preferred_element_type=F32)


HEADS_PER_UP = 4


def _in_proj_kernel(x_ref, g_ref, wm_ref, wdvt_ref, ws_ref, cs_ref, gq_ref, gkv_ref, wqt_ref, wk_ref, wvt_ref,
                    cqt_tab, sqt_tab, ck_tab, sk_tab, o_ref, dvt_ref, qt_ref, k_ref, vt_ref, *, tn):
    h = (_rms(x_ref[...]) * g_ref[...]).astype(BF16)

    small = _dot(h, ws_ref[...])
    hq = (_rms(small[:, :MLA_Q_LORA]) * gq_ref[...]).astype(BF16)
    hkv = (_rms(small[:, MLA_Q_LORA:MLA_Q_LORA + MLA_KV_LORA]) * gkv_ref[...]).astype(BF16)
    kpe = small[:, MLA_Q_LORA + MLA_KV_LORA:]
    kpe = kpe * ck_tab[...] + pltpu.roll(kpe, LANES - MLA_ROPE, 1) * sk_tab[...]
    tk = vt_ref.shape[-1]
    n_tiles = vt_ref.shape[0]
    for c in range(n_tiles):
        vt_ref[c] = _dot_nt(wvt_ref[...], hkv[c * tk:(c + 1) * tk]).astype(vt_ref.dtype)
    cqt = cqt_tab[...]
    sqt = sqt_tab[...]
    up = HEADS_PER_UP * LANES
    for hg in range(MLA_HEADS // HEADS_PER_UP):
        q_t = _dot_nt(wqt_ref[hg * up:(hg + 1) * up, :], hq)
        ka = _dot(hkv, wk_ref[:, hg * up:(hg + 1) * up])
        for hh in range(HEADS_PER_UP):
            hs = slice(hh * LANES, (hh + 1) * LANES)
            out = slice(hg * up + hh * LANES, hg * up + (hh + 1) * LANES)
            blk = q_t[hs]
            partner = jnp.concatenate([blk[MLA_ROPE:], blk[:MLA_ROPE]], axis=0)
            blk = blk * cqt + partner * sqt
            for c in range(n_tiles):
                qt_ref[c, out, :] = blk[:, c * tk:(c + 1) * tk].astype(qt_ref.dtype)
            k_ref[:, out] = (ka[:, hs] + kpe).astype(k_ref.dtype)

    for c in range(wm_ref.shape[1] // tn):
        cols = slice(c * tn, (c + 1) * tn)
        o_ref[:, cols] = (_dot(h, wm_ref[:, cols]) * cs_ref[:, cols]).astype(o_ref.dtype)
    for r in range(wdvt_ref.shape[0] // tn):
        rows = slice(r * tn, (r + 1) * tn)
        dqv_t = _dot_nt(wdvt_ref[rows, :], h)
        if r * tn < DIFF_W:
            dqv_t = dqv_t * DIFF_Q_SCALE
        for c in range(n_tiles):
            dvt_ref[c, rows, :] = dqv_t[:, c * tk:(c + 1) * tk].astype(dvt_ref.dtype)


def _resident(shape):
    return pl.BlockSpec(shape, lambda *_: (0,) * len(shape), pipeline_mode=pl.Buffered(1))


def _layer_slab(w, layer):
    return pl.BlockSpec((None,) + w.shape[1:], lambda *_: (layer, 0, 0), pipeline_mode=pl.Buffered(1))


def _in_proj(x2, g, w_main, w_dvt, w_small, layer, colscale, gq, gkv, wqt, wk, wvt, tabs, *, seq, tm, tn, tk):
    t, d = x2.shape
    n = w_main.shape[2]
    assert n % tn == 0 and w_dvt.shape[1] % tn == 0 and tm % tk == 0 and seq % tm == 0
    ns = seq // tm
    nc = tm // tk
    cqt, sqt, ck, sk = tabs
    tab_t = pl.BlockSpec((LANES, tm), lambda i: (0, i % ns))
    tab = pl.BlockSpec((tm, LANES), lambda i: (i % ns, 0))
    return pl.pallas_call(
        functools.partial(_in_proj_kernel, tn=tn),
        grid=(t // tm,),
        in_specs=[pl.BlockSpec((tm, d), lambda i: (i, 0)), _resident((1, d)),
                  _layer_slab(w_main, layer), _layer_slab(w_dvt, layer), _layer_slab(w_small, layer),
                  _resident((1, n)), _resident(gq.shape), _resident(gkv.shape),
                  _resident(wqt.shape), _resident(wk.shape), _resident(wvt.shape),
                  tab_t, tab_t, tab, tab],
        out_specs=[pl.BlockSpec((tm, n), lambda i: (i, 0)),
                   pl.BlockSpec((nc, w_dvt.shape[1], tk), lambda i: (i, 0, 0)),
                   pl.BlockSpec((nc, MLA_PAD_W, tk), lambda i: (i, 0, 0)),
                   pl.BlockSpec((tm, MLA_PAD_W), lambda i: (i, 0)),
                   pl.BlockSpec((nc, MLA_V_W, tk), lambda i: (i, 0, 0))],
        out_shape=[jax.ShapeDtypeStruct((t, n), BF16),
                   jax.ShapeDtypeStruct((t // tk, w_dvt.shape[1], tk), BF16),
                   jax.ShapeDtypeStruct((t // tk, MLA_PAD_W, tk), BF16),
                   jax.ShapeDtypeStruct((t, MLA_PAD_W), BF16),
                   jax.ShapeDtypeStruct((t // tk, MLA_V_W, tk), BF16)],
        compiler_params=_cparams("parallel"),
        name="in_proj",
    )(x2, g, w_main, w_dvt, w_small, colscale, gq, gkv, wqt, wk, wvt, cqt, sqt, ck, sk)


def _ret_kernel(q_ref, k_ref, v_ref, rg_ref, cos_ref, sin_ref, dm_ref, xi_ref, zt_ref, gc_ref,
                o_ref, st_ref):
    st_ref[...] = jnp.zeros_like(st_ref)

    def chunk(n, carry):
        rows = _tile(n, RET_CHUNK)
        cos = cos_ref[rows, :]
        sin = sin_ref[rows, :]
        for h in range(RET_HEADS):
            qs = slice(h * RET_DK, (h + 1) * RET_DK)
            vs = slice(h * RET_DV, (h + 1) * RET_DV)
            q = q_ref[rows, qs].astype(F32)
            k = k_ref[rows, qs].astype(F32)
            qr = q * cos + pltpu.roll(q, RET_DK // 2, 1) * sin
            kr = k * cos + pltpu.roll(k, RET_DK // 2, 1) * sin
            v = v_ref[rows, vs]
            st = st_ref[h]
            inner = _dot_nt(qr.astype(BF16), kr.astype(BF16)) * dm_ref[h]
            out = _dot(inner.astype(BF16), v) + _dot((qr * xi_ref[h]).astype(BF16), st.astype(BF16))
            kz_t = (kr * zt_ref[h]).T.astype(BF16)
            st_ref[h] = gc_ref[h] * st + _dot(kz_t, v)
            rg = rg_ref[rows, vs].astype(F32)
            o_ref[rows, vs] = (_rms(out) * (rg * jax.nn.sigmoid(rg))).astype(o_ref.dtype)
        return carry

    lax.fori_loop(0, q_ref.shape[0] // RET_CHUNK, chunk, 0)


def _retention(proj, tabs, *, batch, seq):
    t = proj.shape[0]
    c = RET_CHUNK
    cos, sin, dmask, xi, zeta, gch = tabs
    return pl.pallas_call(
        _ret_kernel,
        grid=(batch,),
        in_specs=[pl.BlockSpec((seq, RET_QK_W), lambda b: (b, COL_RQ // RET_QK_W)),
                  pl.BlockSpec((seq, RET_QK_W), lambda b: (b, COL_RK // RET_QK_W)),
                  pl.BlockSpec((seq, RET_V_W), lambda b: (b, COL_RV // RET_V_W)),
                  pl.BlockSpec((seq, RET_V_W), lambda b: (b, COL_RG // RET_V_W)),
                  _resident((seq, RET_DK)), _resident((seq, RET_DK)),
                  _resident((RET_HEADS, c, c)), _resident((RET_HEADS, c, RET_DK)),
                  _resident((RET_HEADS, c, RET_DK)), _resident((RET_HEADS, 1, RET_DV))],
        out_specs=pl.BlockSpec((seq, RET_V_W), lambda b: (b, 0)),
        out_shape=jax.ShapeDtypeStruct((t, RET_V_W), BF16),
        scratch_shapes=[pltpu.VMEM((RET_HEADS, RET_DK, RET_DV), F32)],
        compiler_params=_cparams("parallel"),
        name="retention",
    )(proj, proj, proj, proj, cos, sin, dmask, xi, zeta, gch)


SUM_ROWS = 16


def _store_logits(s, s_ref, smax_ref):
    s_ref[...] = s
    smax_ref[...] = jnp.max(s, axis=0, keepdims=True)


def _softmax_update(s_ref, smax_ref, vt, m_ref, acc_ref):
    m = m_ref[...]
    m_new = jnp.maximum(m, smax_ref[...])
    alpha = jnp.exp2(m - m_new)
    p = jnp.exp2(s_ref[...] - m_new).astype(BF16)
    m_ref[...] = m_new
    vt_ones = jnp.concatenate([vt, jnp.ones((SUM_ROWS, vt.shape[1]), BF16)], axis=0)
    acc_ref[...] = alpha * acc_ref[...] + _dot(vt_ones, p)


def _softmax_result(acc_ref):
    dv = acc_ref.shape[0] - SUM_ROWS
    return acc_ref[:dv, :] / acc_ref[dv:dv + 1, :]


def _softmax_reset(m_ref, acc_ref):
    m_ref[...] = jnp.full(m_ref.shape, NEG_BIG, F32)
    acc_ref[...] = jnp.zeros(acc_ref.shape, F32)


def _softmax_restart(m_ref):
    m_ref[...] = jnp.full(m_ref.shape, NEG_BIG, F32)


def _softmax_scratch(group, dv, tk, n):
    return [pltpu.VMEM((2, group, tk, n), F32), pltpu.VMEM((2, group, 1, n), F32),
            pltpu.VMEM((group, 1, n), F32), pltpu.VMEM((group, dv + SUM_ROWS, n), F32)]


def _tile(i, size):
    return pl.ds(pl.multiple_of(i * size, size), size)


def _causal_tile_pipeline(nq, group, logits, softmax, finalize, near_tiles, interleave):
    def step(qi, j, slot):
        last = j == qi
        nqi = jnp.where(last, qi + 1, qi)
        nj = jnp.where(last, 0, j + 1)
        nqi_valid = jnp.minimum(nqi, nq - 1)

        def issue(near):
            if interleave:
                for g in range(group):
                    softmax(g, j, slot)
                    logits(g, nqi_valid, nj, 1 - slot, near)
            else:
                for g in range(group):
                    logits(g, nqi_valid, nj, 1 - slot, near)
                for g in range(group):
                    softmax(g, j, slot)

        lax.cond(nqi_valid - nj < near_tiles, lambda: issue(True), lambda: issue(False))

        @pl.when(last)
        def _():
            finalize(qi)

        return nqi, nj

    n_pairs = nq * (nq + 1) // 2
    zero = jnp.int32(0)
    for g in range(group):
        logits(g, zero, zero, 0, True)
    carry = lax.fori_loop(0, n_pairs // 2, lambda _, c: step(*step(*c, 0), 1), (zero, zero))
    if n_pairs % 2:
        step(*carry, 0)


def _diff_kernel(lam_ref, qt_ref, k_ref, vt_ref, tab_ref, gt_ref, o_ref, s_ref, smax_ref, m_ref, acc_ref,
                 *, tq, group):
    w = 2 * DIFF_DH
    nq = vt_ref.shape[0]
    hslice = [slice(g * w, (g + 1) * w) for g in range(group)]
    state = [(m_ref.at[g], acc_ref.at[g]) for g in range(group)]
    for g in range(group):
        _softmax_reset(*state[g])

    zeros = jnp.zeros((DIFF_DH, tq), BF16)

    def logits(g, qi, j, slot, near):
        qt = qt_ref[qi, hslice[g], :]
        q2 = jnp.concatenate([jnp.concatenate([qt[:DIFF_DH], zeros], axis=0),
                              jnp.concatenate([zeros, qt[DIFF_DH:]], axis=0)], axis=1)
        s = _dot(k_ref[_tile(j, tq), hslice[g]], q2)
        if near:
            bias = tab_ref[g, qi - j]
            s = s + jnp.concatenate([bias, bias], axis=1)
        _store_logits(s, s_ref.at[slot, g], smax_ref.at[slot, g])

    def softmax(g, j, slot):
        _softmax_update(s_ref.at[slot, g], smax_ref.at[slot, g], vt_ref[j, hslice[g], :], *state[g])

    def finalize(qi):
        for g in range(group):
            o = _softmax_result(state[g][1])
            a = o[:, :tq] - lam_ref[0] * o[:, tq:]
            a = a * lax.rsqrt(jnp.mean(a * a, axis=0, keepdims=True) + EPS) * gt_ref[...]
            o_ref[_tile(qi, tq), hslice[g]] = a.T.astype(o_ref.dtype)
            _softmax_restart(state[g][0])

    _causal_tile_pipeline(nq, group, logits, softmax, finalize, near_tiles=tab_ref.shape[1],
                          interleave=True)


def _diff_attention(proj, qvt, lam, tab, g_t, *, batch, seq, tq, group):
    t = proj.shape[0]
    nq = seq // tq
    w = group * 2 * DIFF_DH
    return pl.pallas_call(
        functools.partial(_diff_kernel, tq=tq, group=group),
        grid=(batch, DIFF_HEADS // group),
        in_specs=[pl.BlockSpec(memory_space=pltpu.SMEM),
                  pl.BlockSpec((nq, w, tq), lambda b, h: (b, h, 0)),
                  pl.BlockSpec((seq, w), lambda b, h: (b, COL_DK // w + h)),
                  pl.BlockSpec((nq, w, tq), lambda b, h: (b, DIFF_W // w + h, 0)),
                  pl.BlockSpec((group, 2, tq, tq), lambda b, h: (h, 0, 0, 0)),
                  _resident(g_t.shape)],
        out_specs=pl.BlockSpec((seq, w), lambda b, h: (b, h)),
        out_shape=jax.ShapeDtypeStruct((t, DIFF_W), BF16),
        scratch_shapes=_softmax_scratch(group, 2 * DIFF_DH, tq, 2 * tq),
        compiler_params=_cparams("parallel", "parallel"),
        name="diff_attention",
    )(lam, qvt, proj, qvt, tab, g_t)


def _mla_kernel(qt_ref, k_ref, vt_ref, mask_ref, o_ref, s_ref, smax_ref, m_ref, acc_ref, *, tq, group):
    hslice = [slice(g * LANES, (g + 1) * LANES) for g in range(group)]
    vslice = [slice(g * MLA_V, (g + 1) * MLA_V) for g in range(group)]
    state = [(m_ref.at[g], acc_ref.at[g]) for g in range(group)]
    for g in range(group):
        _softmax_reset(*state[g])

    def logits(g, qi, j, slot, near):
        qt = jnp.concatenate([qt_ref[qi, hslice[g], :][:MLA_QH], jnp.zeros((LANES - MLA_QH, tq), BF16)], axis=0)
        s = _dot(k_ref[_tile(j, tq), hslice[g]], qt)
        if near:
            s = s + mask_ref[...]
        _store_logits(s, s_ref.at[slot, g], smax_ref.at[slot, g])

    def softmax(g, j, slot):
        _softmax_update(s_ref.at[slot, g], smax_ref.at[slot, g], vt_ref[j, vslice[g], :], *state[g])

    def finalize(qi):
        o = jnp.concatenate([_softmax_result(acc) for _, acc in state], axis=0)
        o_ref[_tile(qi, tq), :] = o.T.astype(o_ref.dtype)
        for g in range(group):
            _softmax_restart(state[g][0])

    _causal_tile_pipeline(vt_ref.shape[0], group, logits, softmax, finalize, near_tiles=1,
                          interleave=False)


def _mla_attention(qt, km, vt, mask, *, batch, seq, tq, group):
    t = km.shape[0]
    nq = seq // tq
    return pl.pallas_call(
        functools.partial(_mla_kernel, tq=tq, group=group),
        grid=(batch, MLA_HEADS // group),
        in_specs=[pl.BlockSpec((nq, group * LANES, tq), lambda b, h: (b, h, 0)),
                  pl.BlockSpec((seq, group * LANES), lambda b, h: (b, h)),
                  pl.BlockSpec((nq, group * MLA_V, tq), lambda b, h: (b, h, 0)),
                  _resident(mask.shape)],
        out_specs=pl.BlockSpec((seq, group * MLA_V), lambda b, h: (b, h)),
        out_shape=jax.ShapeDtypeStruct((t, MLA_V_W), BF16),
        scratch_shapes=_softmax_scratch(group, MLA_V, tq, tq),
        compiler_params=_cparams("parallel", "parallel"),
        name="mla_attention",
    )(qt, km, vt, mask)


def _merge_kernel(x_ref, a_ref, b_ref, c_ref, g_ref, wg_ref, wa_ref, wb_ref, wc_ref, wo_ref, pg_ref, o_ref):
    x = x_ref[...]
    d = x.shape[1]
    h = (_rms(x) * g_ref[...]).astype(BF16)
    merged = None
    for n, (br_ref, w_ref) in enumerate(((a_ref, wa_ref), (b_ref, wb_ref), (c_ref, wc_ref))):
        gate = jax.nn.sigmoid(_dot(h, wg_ref[:, n * d:(n + 1) * d]))
        part = gate * _dot(br_ref[...], w_ref[...])
        merged = part if merged is None else merged + part
    y = _dot(merged.astype(BF16), wo_ref[...])
    o_ref[...] = x + _rms(y) * pg_ref[...]


def _merge(x2, a, b, c, g, w_gates, wa, wb, wc, wo, layer, pg, *, tm):
    t, d = x2.shape
    rowblk = pl.BlockSpec((tm, d), lambda i: (i, 0))
    return pl.pallas_call(
        _merge_kernel,
        grid=(t // tm,),
        in_specs=[rowblk, rowblk, rowblk, rowblk, _resident((1, d))]
        + [_layer_slab(w, layer) for w in (w_gates, wa, wb, wc, wo)] + [_resident((1, d))],
        out_specs=rowblk,
        out_shape=jax.ShapeDtypeStruct((t, d), F32),
        compiler_params=_cparams("parallel"),
        name="merge",
    )(x2, a, b, c, g, w_gates, wa, wb, wc, wo, pg)


def _mlp_kernel(x_ref, g_ref, wu_ref, wd_ref, pg_ref, o_ref, *, tf):
    x = x_ref[...]
    h = (_rms(x) * g_ref[...]).astype(BF16)
    acc = None
    for c in range(wu_ref.shape[1] // tf):
        u = jnp.maximum(_dot(h, wu_ref[:, c * tf:(c + 1) * tf]), 0.0)
        part = _dot((u * u).astype(BF16), wd_ref[c * tf:(c + 1) * tf, :])
        acc = part if acc is None else acc + part
    o_ref[...] = x + _rms(acc) * pg_ref[...]


def _mlp(x2, g, wu, wd, layer, pg, *, tm, tf):
    t, d = x2.shape
    rowblk = pl.BlockSpec((tm, d), lambda i: (i, 0))
    return pl.pallas_call(
        functools.partial(_mlp_kernel, tf=tf),
        grid=(t // tm,),
        in_specs=[rowblk, _resident((1, d)), _layer_slab(wu, layer), _layer_slab(wd, layer), _resident((1, d))],
        out_specs=rowblk,
        out_shape=jax.ShapeDtypeStruct((t, d), F32),
        compiler_params=_cparams("parallel"),
        name="mlp",
    )(x2, g, wu, wd, pg)


def _rot_half_cols(w):
    half = w.shape[-1] // 2
    return jnp.concatenate([-w[..., half:], w[..., :half]], axis=-1)


def _cast_transposed_kernel(wt_ref, o_ref):
    o_ref[...] = wt_ref[...].T.astype(o_ref.dtype)


def _cast_transposed(wt, n, *, tn, skip=None):
    depth, _, d = wt.shape
    if skip is None:
        src = lambda c: c
    else:
        src = lambda c: jnp.where(c < skip[0] // tn, c, c + (skip[1] - skip[0]) // tn)
    return pl.pallas_call(
        _cast_transposed_kernel,
        grid=(depth, n // tn),
        in_specs=[pl.BlockSpec((None, tn, d), lambda l, c: (l, src(c), 0))],
        out_specs=pl.BlockSpec((None, d, tn), lambda l, c: (l, 0, c)),
        out_shape=jax.ShapeDtypeStruct((depth, d, n), BF16),
        compiler_params=_cparams("parallel", "parallel"),
        name="w_in_prep",
    )(wt)


def _prep_w_in(w_in, *, tn):
    wt = jnp.swapaxes(w_in, 1, 2)
    cq_at = W_IN_DV + DIFF_W
    kpe_at = cq_at + MLA_Q_LORA + MLA_KV_LORA
    gates_at = kpe_at + MLA_ROPE
    kpe = wt[:, kpe_at:gates_at]
    half = MLA_ROPE // 2
    zeros = jnp.zeros((wt.shape[0], MLA_NOPE, wt.shape[2]), wt.dtype)
    small = jnp.concatenate([wt[:, cq_at:kpe_at], zeros, kpe, -kpe[:, half:], kpe[:, :half]], axis=1)
    dqv = jnp.concatenate([wt[:, W_IN_DQ:W_IN_DK], wt[:, W_IN_DV:cq_at]], axis=1)
    return (_cast_transposed(wt, N_IN_MAIN, tn=tn, skip=(W_IN_DQ, W_IN_DK)),
            dqv.astype(BF16),
            _cast_transposed(wt[:, gates_at:], N_BRANCH * D_MODEL, tn=tn),
            _cast_transposed(small, N_IN_SMALL, tn=tn))


def _in_colscale():
    cs = np.ones((1, N_IN_MAIN), np.float32)
    cs[0, COL_RK:COL_RK + RET_QK_W] = RET_DK ** -0.5
    return jnp.asarray(cs)


def _prep_w_uq(w):
    w = w.reshape(MLA_Q_LORA, MLA_HEADS, MLA_QH)
    pe = w[..., MLA_NOPE:]
    w = jnp.concatenate([w, _rot_half_cols(pe)], axis=-1).reshape(MLA_Q_LORA, MLA_PAD_W)
    return w.T.astype(BF16)


def _prep_w_ukv(w):
    w = w.reshape(MLA_KV_LORA, MLA_HEADS, MLA_KVH)
    wk = jnp.concatenate([w[..., :MLA_NOPE], jnp.zeros_like(w[..., :MLA_NOPE])], axis=-1)
    return (wk.reshape(MLA_KV_LORA, MLA_PAD_W).astype(BF16),
            w[..., MLA_NOPE:].reshape(MLA_KV_LORA, MLA_V_W).T.astype(BF16))


def _ret_tables(seq):
    pos = np.arange(seq, dtype=np.float64)
    inv_freq = 1.0 / (10000.0 ** np.linspace(0.0, 1.0, RET_DK // 2))
    ang = pos[:, None] * inv_freq[None, :]
    cos, sin = np.cos(ang), np.sin(ang)
    cos_t = np.concatenate([cos, cos], axis=-1)
    sin_t = np.concatenate([-sin, sin], axis=-1)
    log_g = np.log1p(-np.exp2(-5.0 - np.arange(RET_HEADS, dtype=np.float64)))
    idx = np.arange(RET_CHUNK, dtype=np.float64)
    rel = idx[:, None] - idx[None, :]
    dmask = np.where(rel >= 0, np.exp(np.maximum(rel, 0.0)[None] * log_g[:, None, None]), 0.0)
    xi = np.exp((idx + 1.0)[None, :] * log_g[:, None])[:, :, None]
    zeta = np.exp((RET_CHUNK - 1.0 - idx)[None, :] * log_g[:, None])[:, :, None]
    g_chunk = np.exp(RET_CHUNK * log_g)[:, None, None]
    tabs = (cos_t, sin_t, dmask,
            np.broadcast_to(xi, (RET_HEADS, RET_CHUNK, RET_DK)),
            np.broadcast_to(zeta, (RET_HEADS, RET_CHUNK, RET_DK)),
            np.broadcast_to(g_chunk, (RET_HEADS, 1, RET_DV)))
    return tuple(jnp.asarray(t, F32) for t in tabs)


def _mla_tables(seq):
    pos = np.arange(seq, dtype=np.float64)
    inv_freq = 1.0 / (ROPE_THETA ** (np.arange(0, MLA_ROPE, 2, dtype=np.float64) / MLA_ROPE))
    ang = pos[:, None] * inv_freq[None, :]
    cos, sin = np.cos(ang), np.sin(ang)
    z_nope = np.zeros((seq, MLA_NOPE))
    z_rope = np.zeros((seq, MLA_ROPE))
    cos_k = np.concatenate([z_nope, cos, cos, z_rope], axis=-1)
    sin_k = np.concatenate([z_nope, sin, sin, z_rope], axis=-1)
    qscale = MLA_QH ** -0.5 * LOG2E
    cos_q = np.concatenate([np.ones((seq, MLA_NOPE)), cos, cos, z_rope], axis=-1) * qscale
    sin_q = sin_k * qscale
    return tuple(jnp.asarray(t, F32) for t in (cos_q.T, sin_q.T, cos_k, sin_k))


def _t5_bucket(n):
    max_exact = REL_BUCKETS // 2
    nf = jnp.maximum(n, 1).astype(F32)
    large = max_exact + (jnp.log(nf / max_exact) / math.log(REL_MAX_DIST / max_exact)
                         * (REL_BUCKETS - max_exact)).astype(jnp.int32)
    large = jnp.minimum(large, REL_BUCKETS - 1)
    return jnp.where(n < max_exact, n, large)


def _toeplitz(g, tq):
    h = g.shape[0]
    padded = jnp.concatenate([g, jnp.zeros((h, 1), g.dtype)], axis=1)
    skewed = jnp.tile(padded, (1, tq))[:, :tq * (2 * tq - 1)].reshape(h, tq, 2 * tq - 1)
    return skewed[:, :, tq - 1:]


def _bias_table(rel_bias, tq):
    assert tq >= REL_MAX_DIST
    bias = rel_bias[_t5_bucket(jnp.arange(2 * tq))].astype(F32).T
    bias = (bias - rel_bias[REL_BUCKETS - 1].astype(F32)[:, None]) * LOG2E
    masked = jnp.full((bias.shape[0], tq - 1), NEG_BIG, F32)
    diagonal = jnp.concatenate([masked, bias[:, :tq]], axis=1)
    behind = bias[:, 1:]
    return jnp.stack([_toeplitz(diagonal, tq), _toeplitz(behind, tq)], axis=1)


def _causal_table(tq):
    kk = np.arange(tq)[:, None]
    qq = np.arange(tq)[None, :]
    return jnp.asarray(np.where(qq >= kk, 0.0, NEG_BIG), F32)


def _pick(t, want):
    return want if t % want == 0 else t


def kernel(x, rel_bias, pre_mix_g, w_in, w_ret_o, lambda_q1, lambda_k1, lambda_q2, lambda_k2, diff_subln_g, w_diff_o, mla_q_norm_g, w_mla_uq, mla_kv_norm_g, w_mla_ukv, w_mla_o, w_out, post_mix_g, pre_mlp_g, w_up, w_down, post_mlp_g):
    batch, seq, d = x.shape
    t = batch * seq
    depth = w_in.shape[0]
    tq = _pick(seq, 256)
    ret_tabs = _ret_tables(seq)
    mla_tabs = _mla_tables(seq)
    bias_tab = _bias_table(rel_bias, tq)
    causal_tab = _causal_table(tq)
    colscale = _in_colscale()
    w_main, w_dvt, w_gates, w_small = _prep_w_in(w_in, tn=512)
    merge_w = tuple(w.astype(BF16) for w in (w_ret_o, w_diff_o, w_mla_o, w_out))
    mlp_w = (w_up.astype(BF16), w_down.astype(BF16))
    row = lambda v: v.reshape(1, -1).astype(F32)

    x2 = x.reshape(t, d)
    for l in range(depth):
        wk, wvt = _prep_w_ukv(w_mla_ukv[l])
        proj, dvt, qt, km, vt = _in_proj(x2, row(pre_mix_g[l]), w_main, w_dvt, w_small, l, colscale,
                                         row(mla_q_norm_g[l]), row(mla_kv_norm_g[l]),
                                         _prep_w_uq(w_mla_uq[l]), wk, wvt, mla_tabs,
                                         seq=seq, tm=_pick(seq, 512), tn=512, tk=tq)

        ret = _retention(proj, ret_tabs, batch=batch, seq=seq)

        lambda_init = 0.8 - 0.6 * math.exp(-0.3 * l)
        lam = (jnp.exp(jnp.sum(lambda_q1[l] * lambda_k1[l]).astype(F32))
               - jnp.exp(jnp.sum(lambda_q2[l] * lambda_k2[l]).astype(F32)) + lambda_init)
        subln_g = jnp.broadcast_to((diff_subln_g[l].astype(F32) * (1.0 - lambda_init))[:, None], (2 * DIFF_DH, tq))
        da = _diff_attention(proj, dvt, lam.reshape(1), bias_tab, subln_g,
                             batch=batch, seq=seq, tq=tq, group=4)

        mo = _mla_attention(qt, km, vt, causal_tab, batch=batch, seq=seq, tq=tq, group=8)

        x2 = _merge(x2, ret, da, mo, row(pre_mix_g[l]), w_gates, *merge_w, l, row(post_mix_g[l]),
                    tm=_pick(t, 512))

        x2 = _mlp(x2, row(pre_mlp_g[l]), *mlp_w, l, row(post_mlp_g[l]), tm=_pick(t, 1024), tf=1024)
    return x2.reshape(batch, seq, d)
```

```python
import functools
import math
from typing import NamedTuple

import jax
import jax.numpy as jnp
import numpy as np
from jax import lax
from jax.experimental import pallas as pl
from jax.experimental.pallas import tpu as pltpu

F32 = jnp.float32
BF16 = jnp.bfloat16

D_MODEL = 1024
EPS = 1e-6
RET_HEADS = 4
RET_DK = 128
RET_DV = 256
RET_CHUNK = 128
DIFF_HEADS = 8
DIFF_DH = 64
MLA_HEADS = 16
MLA_Q_LORA = 256
MLA_KV_LORA = 128
MLA_NOPE = 64
MLA_ROPE = 32
MLA_V = 64
ROPE_THETA = 10000.0
REL_BUCKETS = 32
REL_MAX_DIST = 128
N_BRANCH = 3

LANES = 128
V7X_VMEM_BYTES = 64 * 1024 * 1024
VMEM_LIMIT = V7X_VMEM_BYTES * 7 // 8
LOG2E = math.log2(math.e)
NEG_BIG = -1e30

RET_QK_W = RET_HEADS * RET_DK
RET_V_W = RET_HEADS * RET_DV
DIFF_W = DIFF_HEADS * 2 * DIFF_DH
MLA_QH = MLA_NOPE + MLA_ROPE
MLA_KVH = MLA_NOPE + MLA_V
MLA_V_W = MLA_HEADS * MLA_V
MLA_PAD_W = MLA_HEADS * LANES

COL_RQ = 0
COL_RK = COL_RQ + RET_QK_W
COL_RV = COL_RK + RET_QK_W
COL_RG = COL_RV + RET_V_W
COL_DK = COL_RG + RET_V_W
N_IN_MAIN = COL_DK + DIFF_W
W_IN_DQ = COL_DK
W_IN_DK = W_IN_DQ + DIFF_W
W_IN_DV = W_IN_DK + DIFF_W
DIFF_Q_SCALE = DIFF_DH ** -0.5 * LOG2E
N_IN_SMALL = MLA_Q_LORA + MLA_KV_LORA + LANES


def _cparams(*sem):
    return pltpu.CompilerParams(dimension_semantics=sem, vmem_limit_bytes=VMEM_LIMIT)


def _rms(xf):
    return xf * lax.rsqrt(jnp.mean(xf * xf, axis=-1, keepdims=True) + EPS)


def _dot(a, b):
    return jnp.dot(a, b, preferred_element_type=F32)


def _dot_nt(a, b):
    return lax.dot_general(a, b, (((1,), (1,)), ((), ())), preferred_element_type=F32)


HEADS_PER_UP = 4


def _in_proj_kernel(x_ref, g_ref, wm_ref, wdvt_ref, ws_ref, cs_ref, gq_ref, gkv_ref, wqt_ref, wk_ref, wvt_ref,
                    cqt_tab, sqt_tab, ck_tab, sk_tab, o_ref, dvt_ref, qt_ref, k_ref, vt_ref, *, tn):
    h = (_rms(x_ref[...]) * g_ref[...]).astype(BF16)

    small = _dot(h, ws_ref[...])
    hq = (_rms(small[:, :MLA_Q_LORA]) * gq_ref[...]).astype(BF16)
    hkv = (_rms(small[:, MLA_Q_LORA:MLA_Q_LORA + MLA_KV_LORA]) * gkv_ref[...]).astype(BF16)
    kpe = small[:, MLA_Q_LORA + MLA_KV_LORA:]
    kpe = kpe * ck_tab[...] + pltpu.roll(kpe, LANES - MLA_ROPE, 1) * sk_tab[...]
    tk = vt_ref.shape[-1]
    n_tiles = vt_ref.shape[0]
    for c in range(n_tiles):
        vt_ref[c] = _dot_nt(wvt_ref[...], hkv[c * tk:(c + 1) * tk]).astype(vt_ref.dtype)
    cqt = cqt_tab[...]
    sqt = sqt_tab[...]
    up = HEADS_PER_UP * LANES
    for hg in range(MLA_HEADS // HEADS_PER_UP):
        q_t = _dot_nt(wqt_ref[hg * up:(hg + 1) * up, :], hq)
        ka = _dot(hkv, wk_ref[:, hg * up:(hg + 1) * up])
        for hh in range(HEADS_PER_UP):
            hs = slice(hh * LANES, (hh + 1) * LANES)
            out = slice(hg * up + hh * LANES, hg * up + (hh + 1) * LANES)
            blk = q_t[hs]
            partner = jnp.concatenate([blk[MLA_ROPE:], blk[:MLA_ROPE]], axis=0)
            blk = blk * cqt + partner * sqt
            for c in range(n_tiles):
                qt_ref[c, out, :] = blk[:, c * tk:(c + 1) * tk].astype(qt_ref.dtype)
            k_ref[:, out] = (ka[:, hs] + kpe).astype(k_ref.dtype)

    for c in range(wm_ref.shape[1] // tn):
        cols = slice(c * tn, (c + 1) * tn)
        o_ref[:, cols] = (_dot(h, wm_ref[:, cols]) * cs_ref[:, cols]).astype(o_ref.dtype)
    for r in range(wdvt_ref.shape[0] // tn):
        rows = slice(r * tn, (r + 1) * tn)
        dqv_t = _dot_nt(wdvt_ref[rows, :], h)
        if r * tn < DIFF_W:
            dqv_t = dqv_t * DIFF_Q_SCALE
        for c in range(n_tiles):
            dvt_ref[c, rows, :] = dqv_t[:, c * tk:(c + 1) * tk].astype(dvt_ref.dtype)


def _resident(shape):
    return pl.BlockSpec(shape, lambda *_: (0,) * len(shape), pipeline_mode=pl.Buffered(1))


def _layer_slab(w, layer):
    return pl.BlockSpec((None,) + w.shape[1:], lambda *_: (layer, 0, 0), pipeline_mode=pl.Buffered(1))


def _in_proj(x2, g, w_main, w_dvt, w_small, layer, colscale, gq, gkv, wqt, wk, wvt, tabs, *, seq, tm, tn, tk):
    t, d = x2.shape
    n = w_main.shape[2]
    assert n % tn == 0 and w_dvt.shape[1] % tn == 0 and tm % tk == 0 and seq % tm == 0
    ns = seq // tm
    nc = tm // tk
    cqt, sqt, ck, sk = tabs
    tab_t = pl.BlockSpec((LANES, tm), lambda i: (0, i % ns))
    tab = pl.BlockSpec((tm, LANES), lambda i: (i % ns, 0))
    return pl.pallas_call(
        functools.partial(_in_proj_kernel, tn=tn),
        grid=(t // tm,),
        in_specs=[pl.BlockSpec((tm, d), lambda i: (i, 0)), _resident((1, d)),
                  _layer_slab(w_main, layer), _layer_slab(w_dvt, layer), _layer_slab(w_small, layer),
                  _resident((1, n)), _resident(gq.shape), _resident(gkv.shape),
                  _resident(wqt.shape), _resident(wk.shape), _resident(wvt.shape),
                  tab_t, tab_t, tab, tab],
        out_specs=[pl.BlockSpec((tm, n), lambda i: (i, 0)),
                   pl.BlockSpec((nc, w_dvt.shape[1], tk), lambda i: (i, 0, 0)),
                   pl.BlockSpec((nc, MLA_PAD_W, tk), lambda i: (i, 0, 0)),
                   pl.BlockSpec((tm, MLA_PAD_W), lambda i: (i, 0)),
                   pl.BlockSpec((nc, MLA_V_W, tk), lambda i: (i, 0, 0))],
        out_shape=[jax.ShapeDtypeStruct((t, n), BF16),
                   jax.ShapeDtypeStruct((t // tk, w_dvt.shape[1], tk), BF16),
                   jax.ShapeDtypeStruct((t // tk, MLA_PAD_W, tk), BF16),
                   jax.ShapeDtypeStruct((t, MLA_PAD_W), BF16),
                   jax.ShapeDtypeStruct((t // tk, MLA_V_W, tk), BF16)],
        compiler_params=_cparams("parallel"),
        name="in_proj",
    )(x2, g, w_main, w_dvt, w_small, colscale, gq, gkv, wqt, wk, wvt, cqt, sqt, ck, sk)


def _ret_kernel(q_ref, k_ref, v_ref, rg_ref, cos_ref, sin_ref, dm_ref, xi_ref, zt_ref, gc_ref,
                o_ref, st_ref):
    st_ref[...] = jnp.zeros_like(st_ref)

    def chunk(n, carry):
        rows = _tile(n, RET_CHUNK)
        cos = cos_ref[rows, :]
        sin = sin_ref[rows, :]
        for h in range(RET_HEADS):
            qs = slice(h * RET_DK, (h + 1) * RET_DK)
            vs = slice(h * RET_DV, (h + 1) * RET_DV)
            q = q_ref[rows, qs].astype(F32)
            k = k_ref[rows, qs].astype(F32)
            qr = q * cos + pltpu.roll(q, RET_DK // 2, 1) * sin
            kr = k * cos + pltpu.roll(k, RET_DK // 2, 1) * sin
            v = v_ref[rows, vs]
            st = st_ref[h]
            inner = _dot_nt(qr.astype(BF16), kr.astype(BF16)) * dm_ref[h]
            out = _dot(inner.astype(BF16), v) + _dot((qr * xi_ref[h]).astype(BF16), st.astype(BF16))
            kz_t = (kr * zt_ref[h]).T.astype(BF16)
            st_ref[h] = gc_ref[h] * st + _dot(kz_t, v)
            rg = rg_ref[rows, vs].astype(F32)
            o_ref[rows, vs] = (_rms(out) * (rg * jax.nn.sigmoid(rg))).astype(o_ref.dtype)
        return carry

    lax.fori_loop(0, q_ref.shape[0] // RET_CHUNK, chunk, 0, unroll=4)


def _retention(proj, tabs, *, batch, seq):
    t = proj.shape[0]
    c = RET_CHUNK
    cos, sin, dmask, xi, zeta, gch = tabs
    return pl.pallas_call(
        _ret_kernel,
        grid=(batch,),
        in_specs=[pl.BlockSpec((seq, RET_QK_W), lambda b: (b, COL_RQ // RET_QK_W)),
                  pl.BlockSpec((seq, RET_QK_W), lambda b: (b, COL_RK // RET_QK_W)),
                  pl.BlockSpec((seq, RET_V_W), lambda b: (b, COL_RV // RET_V_W)),
                  pl.BlockSpec((seq, RET_V_W), lambda b: (b, COL_RG // RET_V_W)),
                  _resident((seq, RET_DK)), _resident((seq, RET_DK)),
                  _resident((RET_HEADS, c, c)), _resident((RET_HEADS, c, RET_DK)),
                  _resident((RET_HEADS, c, RET_DK)), _resident((RET_HEADS, 1, RET_DV))],
        out_specs=pl.BlockSpec((seq, RET_V_W), lambda b: (b, 0)),
        out_shape=jax.ShapeDtypeStruct((t, RET_V_W), BF16),
        scratch_shapes=[pltpu.VMEM((RET_HEADS, RET_DK, RET_DV), F32)],
        compiler_params=_cparams("parallel"),
        name="retention",
    )(proj, proj, proj, proj, cos, sin, dmask, xi, zeta, gch)


SUM_ROWS = 16


def _store_logits(s, s_ref, smax_ref):
    s_ref[...] = s
    smax_ref[...] = jnp.max(s, axis=0, keepdims=True)


def _softmax_update(s_ref, smax_ref, vt, m_ref, acc_ref):
    m = m_ref[...]
    m_new = jnp.maximum(m, smax_ref[...])
    alpha = jnp.exp2(m - m_new)
    p = jnp.exp2(s_ref[...] - m_new).astype(BF16)
    m_ref[...] = m_new
    vt_ones = jnp.concatenate([vt, jnp.ones((SUM_ROWS, vt.shape[1]), BF16)], axis=0)
    acc_ref[...] = alpha * acc_ref[...] + _dot(vt_ones, p)


def _softmax_result(acc_ref):
    dv = acc_ref.shape[0] - SUM_ROWS
    return acc_ref[:dv, :] / acc_ref[dv:dv + 1, :]


def _softmax_reset(m_ref, acc_ref):
    m_ref[...] = jnp.full(m_ref.shape, NEG_BIG, F32)
    acc_ref[...] = jnp.zeros(acc_ref.shape, F32)


def _softmax_restart(m_ref):
    m_ref[...] = jnp.full(m_ref.shape, NEG_BIG, F32)


def _softmax_scratch(group, dv, tk, n):
    return [pltpu.VMEM((2, group, tk, n), F32), pltpu.VMEM((2, group, 1, n), F32),
            pltpu.VMEM((group, 1, n), F32), pltpu.VMEM((group, dv + SUM_ROWS, n), F32)]


def _tile(i, size):
    return pl.ds(pl.multiple_of(i * size, size), size)


def _causal_tile_pipeline(nq, group, logits, softmax, finalize, near_tiles, interleave):
    def step(qi, j, slot):
        last = j == qi
        nqi = jnp.where(last, qi + 1, qi)
        nj = jnp.where(last, 0, j + 1)
        nqi_valid = jnp.minimum(nqi, nq - 1)

        def issue(near):
            if interleave:
                for g in range(group):
                    softmax(g, j, slot)
                    logits(g, nqi_valid, nj, 1 - slot, near)
            else:
                for g in range(group):
                    logits(g, nqi_valid, nj, 1 - slot, near)
                for g in range(group):
                    softmax(g, j, slot)

        lax.cond(nqi_valid - nj < near_tiles, lambda: issue(True), lambda: issue(False))

        @pl.when(last)
        def _():
            finalize(qi)

        return nqi, nj

    n_pairs = nq * (nq + 1) // 2
    zero = jnp.int32(0)
    for g in range(group):
        logits(g, zero, zero, 0, True)
    carry = lax.fori_loop(0, n_pairs // 2, lambda _, c: step(*step(*c, 0), 1), (zero, zero))
    if n_pairs % 2:
        step(*carry, 0)


def _diff_kernel(lam_ref, qt_ref, k_ref, vt_ref, tab_ref, gt_ref, o_ref, s_ref, smax_ref, m_ref, acc_ref,
                 *, tq, group):
    w = 2 * DIFF_DH
    nq = vt_ref.shape[0]
    hslice = [slice(g * w, (g + 1) * w) for g in range(group)]
    state = [(m_ref.at[g], acc_ref.at[g]) for g in range(group)]
    for g in range(group):
        _softmax_reset(*state[g])

    zeros = jnp.zeros((DIFF_DH, tq), BF16)

    def logits(g, qi, j, slot, near):
        qt = qt_ref[qi, hslice[g], :]
        q2 = jnp.concatenate([jnp.concatenate([qt[:DIFF_DH], zeros], axis=0),
                              jnp.concatenate([zeros, qt[DIFF_DH:]], axis=0)], axis=1)
        s = _dot(k_ref[_tile(j, tq), hslice[g]], q2)
        if near:
            bias = tab_ref[g, qi - j]
            s = s + jnp.concatenate([bias, bias], axis=1)
        _store_logits(s, s_ref.at[slot, g], smax_ref.at[slot, g])

    def softmax(g, j, slot):
        _softmax_update(s_ref.at[slot, g], smax_ref.at[slot, g], vt_ref[j, hslice[g], :], *state[g])

    def finalize(qi):
        for g in range(group):
            o = _softmax_result(state[g][1])
            a = o[:, :tq] - lam_ref[0] * o[:, tq:]
            a = a * lax.rsqrt(jnp.mean(a * a, axis=0, keepdims=True) + EPS) * gt_ref[...]
            o_ref[_tile(qi, tq), hslice[g]] = a.T.astype(o_ref.dtype)
            _softmax_restart(state[g][0])

    _causal_tile_pipeline(nq, group, logits, softmax, finalize, near_tiles=tab_ref.shape[1],
                          interleave=True)


def _diff_attention(proj, qvt, lam, tab, g_t, *, batch, seq, tq, group):
    t = proj.shape[0]
    nq = seq // tq
    w = group * 2 * DIFF_DH
    return pl.pallas_call(
        functools.partial(_diff_kernel, tq=tq, group=group),
        grid=(batch, DIFF_HEADS // group),
        in_specs=[pl.BlockSpec(memory_space=pltpu.SMEM),
                  pl.BlockSpec((nq, w, tq), lambda b, h: (b, h, 0)),
                  pl.BlockSpec((seq, w), lambda b, h: (b, COL_DK // w + h)),
                  pl.BlockSpec((nq, w, tq), lambda b, h: (b, DIFF_W // w + h, 0)),
                  pl.BlockSpec((group, 2, tq, tq), lambda b, h: (h, 0, 0, 0)),
                  _resident(g_t.shape)],
        out_specs=pl.BlockSpec((seq, w), lambda b, h: (b, h)),
        out_shape=jax.ShapeDtypeStruct((t, DIFF_W), BF16),
        scratch_shapes=_softmax_scratch(group, 2 * DIFF_DH, tq, 2 * tq),
        compiler_params=_cparams("parallel", "parallel"),
        name="diff_attention",
    )(lam, qvt, proj, qvt, tab, g_t)


def _mla_kernel(qt_ref, k_ref, vt_ref, mask_ref, o_ref, s_ref, smax_ref, m_ref, acc_ref, *, tq, group):
    hslice = [slice(g * LANES, (g + 1) * LANES) for g in range(group)]
    vslice = [slice(g * MLA_V, (g + 1) * MLA_V) for g in range(group)]
    state = [(m_ref.at[g], acc_ref.at[g]) for g in range(group)]
    for g in range(group):
        _softmax_reset(*state[g])

    def logits(g, qi, j, slot, near):
        qt = jnp.concatenate([qt_ref[qi, hslice[g], :][:MLA_QH], jnp.zeros((LANES - MLA_QH, tq), BF16)], axis=0)
        s = _dot(k_ref[_tile(j, tq), hslice[g]], qt)
        if near:
            s = s + mask_ref[...]
        _store_logits(s, s_ref.at[slot, g], smax_ref.at[slot, g])

    def softmax(g, j, slot):
        _softmax_update(s_ref.at[slot, g], smax_ref.at[slot, g], vt_ref[j, vslice[g], :], *state[g])

    def finalize(qi):
        o = jnp.concatenate([_softmax_result(acc) for _, acc in state], axis=0)
        o_ref[_tile(qi, tq), :] = o.T.astype(o_ref.dtype)
        for g in range(group):
            _softmax_restart(state[g][0])

    _causal_tile_pipeline(vt_ref.shape[0], group, logits, softmax, finalize, near_tiles=1,
                          interleave=False)


def _mla_attention(qt, km, vt, mask, *, batch, seq, tq, group):
    t = km.shape[0]
    nq = seq // tq
    return pl.pallas_call(
        functools.partial(_mla_kernel, tq=tq, group=group),
        grid=(batch, MLA_HEADS // group),
        in_specs=[pl.BlockSpec((nq, group * LANES, tq), lambda b, h: (b, h, 0)),
                  pl.BlockSpec((seq, group * LANES), lambda b, h: (b, h)),
                  pl.BlockSpec((nq, group * MLA_V, tq), lambda b, h: (b, h, 0)),
                  _resident(mask.shape)],
        out_specs=pl.BlockSpec((seq, group * MLA_V), lambda b, h: (b, h)),
        out_shape=jax.ShapeDtypeStruct((t, MLA_V_W), BF16),
        scratch_shapes=_softmax_scratch(group, MLA_V, tq, tq),
        compiler_params=_cparams("parallel", "parallel"),
        name="mla_attention",
    )(qt, km, vt, mask)


def _merge_kernel(x_ref, a_ref, b_ref, c_ref, g_ref, wg_ref, wa_ref, wb_ref, wc_ref, wo_ref, pg_ref, o_ref):
    x = x_ref[...]
    d = x.shape[1]
    h = (_rms(x) * g_ref[...]).astype(BF16)
    merged = None
    for n, (br_ref, w_ref) in enumerate(((a_ref, wa_ref), (b_ref, wb_ref), (c_ref, wc_ref))):
        gate = jax.nn.sigmoid(_dot(h, wg_ref[:, n * d:(n + 1) * d]))
        part = gate * _dot(br_ref[...], w_ref[...])
        merged = part if merged is None else merged + part
    y = _dot(merged.astype(BF16), wo_ref[...])
    o_ref[...] = x + _rms(y) * pg_ref[...]


def _merge(x2, a, b, c, g, w_gates, wa, wb, wc, wo, layer, pg, *, tm):
    t, d = x2.shape
    rowblk = pl.BlockSpec((tm, d), lambda i: (i, 0))
    return pl.pallas_call(
        _merge_kernel,
        grid=(t // tm,),
        in_specs=[rowblk, rowblk, rowblk, rowblk, _resident((1, d))]
        + [_layer_slab(w, layer) for w in (w_gates, wa, wb, wc, wo)] + [_resident((1, d))],
        out_specs=rowblk,
        out_shape=jax.ShapeDtypeStruct((t, d), F32),
        compiler_params=_cparams("parallel"),
        name="merge",
    )(x2, a, b, c, g, w_gates, wa, wb, wc, wo, pg)


def _mlp_kernel(x_ref, g_ref, wu_ref, wd_ref, pg_ref, o_ref, *, tf):
    x = x_ref[...]
    h = (_rms(x) * g_ref[...]).astype(BF16)
    acc = None
    for c in range(wu_ref.shape[1] // tf):
        u = jnp.maximum(_dot(h, wu_ref[:, c * tf:(c + 1) * tf]), 0.0)
        part = _dot((u * u).astype(BF16), wd_ref[c * tf:(c + 1) * tf, :])
        acc = part if acc is None else acc + part
    o_ref[...] = x + _rms(acc) * pg_ref[...]


def _mlp(x2, g, wu, wd, layer, pg, *, tm, tf):
    t, d = x2.shape
    rowblk = pl.BlockSpec((tm, d), lambda i: (i, 0))
    return pl.pallas_call(
        functools.partial(_mlp_kernel, tf=tf),
        grid=(t // tm,),
        in_specs=[rowblk, _resident((1, d)), _layer_slab(wu, layer), _layer_slab(wd, layer), _resident((1, d))],
        out_specs=rowblk,
        out_shape=jax.ShapeDtypeStruct((t, d), F32),
        compiler_params=_cparams("parallel"),
        name="mlp",
    )(x2, g, wu, wd, pg)


def _rot_half_cols(w):
    half = w.shape[-1] // 2
    return jnp.concatenate([-w[..., half:], w[..., :half]], axis=-1)


def _cast_transposed_kernel(wt_ref, o_ref):
    o_ref[...] = wt_ref[...].T.astype(o_ref.dtype)


def _cast_transposed(wt, n, *, tn, skip=None):
    depth, _, d = wt.shape
    if skip is None:
        src = lambda c: c
    else:
        src = lambda c: jnp.where(c < skip[0] // tn, c, c + (skip[1] - skip[0]) // tn)
    return pl.pallas_call(
        _cast_transposed_kernel,
        grid=(depth, n // tn),
        in_specs=[pl.BlockSpec((None, tn, d), lambda l, c: (l, src(c), 0))],
        out_specs=pl.BlockSpec((None, d, tn), lambda l, c: (l, 0, c)),
        out_shape=jax.ShapeDtypeStruct((depth, d, n), BF16),
        compiler_params=_cparams("parallel", "parallel"),
        name="w_in_prep",
    )(wt)


def _prep_w_in(w_in, *, tn):
    wt = jnp.swapaxes(w_in, 1, 2)
    cq_at = W_IN_DV + DIFF_W
    kpe_at = cq_at + MLA_Q_LORA + MLA_KV_LORA
    gates_at = kpe_at + MLA_ROPE
    kpe = wt[:, kpe_at:gates_at]
    half = MLA_ROPE // 2
    zeros = jnp.zeros((wt.shape[0], MLA_NOPE, wt.shape[2]), wt.dtype)
    small = jnp.concatenate([wt[:, cq_at:kpe_at], zeros, kpe, -kpe[:, half:], kpe[:, :half]], axis=1)
    dqv = jnp.concatenate([wt[:, W_IN_DQ:W_IN_DK], wt[:, W_IN_DV:cq_at]], axis=1)
    return (_cast_transposed(wt, N_IN_MAIN, tn=tn, skip=(W_IN_DQ, W_IN_DK)),
            dqv.astype(BF16),
            _cast_transposed(wt[:, gates_at:], N_BRANCH * D_MODEL, tn=tn),
            _cast_transposed(small, N_IN_SMALL, tn=tn))


def _in_colscale():
    cs = np.ones((1, N_IN_MAIN), np.float32)
    cs[0, COL_RK:COL_RK + RET_QK_W] = RET_DK ** -0.5
    return jnp.asarray(cs)


def _prep_w_uq(w):
    w = w.reshape(MLA_Q_LORA, MLA_HEADS, MLA_QH)
    pe = w[..., MLA_NOPE:]
    w = jnp.concatenate([w, _rot_half_cols(pe)], axis=-1).reshape(MLA_Q_LORA, MLA_PAD_W)
    return w.T.astype(BF16)


def _prep_w_ukv(w):
    w = w.reshape(MLA_KV_LORA, MLA_HEADS, MLA_KVH)
    wk = jnp.concatenate([w[..., :MLA_NOPE], jnp.zeros_like(w[..., :MLA_NOPE])], axis=-1)
    return (wk.reshape(MLA_KV_LORA, MLA_PAD_W).astype(BF16),
            w[..., MLA_NOPE:].reshape(MLA_KV_LORA, MLA_V_W).T.astype(BF16))


def _ret_tables(seq):
    pos = np.arange(seq, dtype=np.float64)
    inv_freq = 1.0 / (10000.0 ** np.linspace(0.0, 1.0, RET_DK // 2))
    ang = pos[:, None] * inv_freq[None, :]
    cos, sin = np.cos(ang), np.sin(ang)
    cos_t = np.concatenate([cos, cos], axis=-1)
    sin_t = np.concatenate([-sin, sin], axis=-1)
    log_g = np.log1p(-np.exp2(-5.0 - np.arange(RET_HEADS, dtype=np.float64)))
    idx = np.arange(RET_CHUNK, dtype=np.float64)
    rel = idx[:, None] - idx[None, :]
    dmask = np.where(rel >= 0, np.exp(np.maximum(rel, 0.0)[None] * log_g[:, None, None]), 0.0)
    xi = np.exp((idx + 1.0)[None, :] * log_g[:, None])[:, :, None]
    zeta = np.exp((RET_CHUNK - 1.0 - idx)[None, :] * log_g[:, None])[:, :, None]
    g_chunk = np.exp(RET_CHUNK * log_g)[:, None, None]
    tabs = (cos_t, sin_t, dmask,
            np.broadcast_to(xi, (RET_HEADS, RET_CHUNK, RET_DK)),
            np.broadcast_to(zeta, (RET_HEADS, RET_CHUNK, RET_DK)),
            np.broadcast_to(g_chunk, (RET_HEADS, 1, RET_DV)))
    return tuple(jnp.asarray(t, F32) for t in tabs)


def _mla_tables(seq):
    pos = np.arange(seq, dtype=np.float64)
    inv_freq = 1.0 / (ROPE_THETA ** (np.arange(0, MLA_ROPE, 2, dtype=np.float64) / MLA_ROPE))
    ang = pos[:, None] * inv_freq[None, :]
    cos, sin = np.cos(ang), np.sin(ang)
    z_nope = np.zeros((seq, MLA_NOPE))
    z_rope = np.zeros((seq, MLA_ROPE))
    cos_k = np.concatenate([z_nope, cos, cos, z_rope], axis=-1)
    sin_k = np.concatenate([z_nope, sin, sin, z_rope], axis=-1)
    qscale = MLA_QH ** -0.5 * LOG2E
    cos_q = np.concatenate([np.ones((seq, MLA_NOPE)), cos, cos, z_rope], axis=-1) * qscale
    sin_q = sin_k * qscale
    return tuple(jnp.asarray(t, F32) for t in (cos_q.T, sin_q.T, cos_k, sin_k))


def _t5_bucket(n):
    max_exact = REL_BUCKETS // 2
    nf = jnp.maximum(n, 1).astype(F32)
    large = max_exact + (jnp.log(nf / max_exact) / math.log(REL_MAX_DIST / max_exact)
                         * (REL_BUCKETS - max_exact)).astype(jnp.int32)
    large = jnp.minimum(large, REL_BUCKETS - 1)
    return jnp.where(n < max_exact, n, large)


def _toeplitz(g, tq):
    h = g.shape[0]
    padded = jnp.concatenate([g, jnp.zeros((h, 1), g.dtype)], axis=1)
    skewed = jnp.tile(padded, (1, tq))[:, :tq * (2 * tq - 1)].reshape(h, tq, 2 * tq - 1)
    return skewed[:, :, tq - 1:]


def _bias_table(rel_bias, tq):
    assert tq >= REL_MAX_DIST
    bias = rel_bias[_t5_bucket(jnp.arange(2 * tq))].astype(F32).T
    bias = (bias - rel_bias[REL_BUCKETS - 1].astype(F32)[:, None]) * LOG2E
    masked = jnp.full((bias.shape[0], tq - 1), NEG_BIG, F32)
    diagonal = jnp.concatenate([masked, bias[:, :tq]], axis=1)
    behind = bias[:, 1:]
    return jnp.stack([_toeplitz(diagonal, tq), _toeplitz(behind, tq)], axis=1)


def _causal_table(tq):
    kk = np.arange(tq)[:, None]
    qq = np.arange(tq)[None, :]
    return jnp.asarray(np.where(qq >= kk, 0.0, NEG_BIG), F32)


def _pick(t, want):
    return want if t % want == 0 else t


class _Tiles(NamedTuple):
    attn: int
    in_rows: int
    in_cols: int
    merge_rows: int
    mlp_rows: int
    mlp_cols: int
    diff_group: int
    mla_group: int


def _plan_tiles(batch, seq):
    t = batch * seq
    return _Tiles(attn=_pick(seq, 256), in_rows=_pick(seq, 512), in_cols=512, merge_rows=_pick(t, 512),
                  mlp_rows=_pick(t, 1024), mlp_cols=1024, diff_group=4, mla_group=8)


def kernel(x, rel_bias, pre_mix_g, w_in, w_ret_o, lambda_q1, lambda_k1, lambda_q2, lambda_k2, diff_subln_g, w_diff_o, mla_q_norm_g, w_mla_uq, mla_kv_norm_g, w_mla_ukv, w_mla_o, w_out, post_mix_g, pre_mlp_g, w_up, w_down, post_mlp_g):
    batch, seq, d = x.shape
    t = batch * seq
    depth = w_in.shape[0]
    tiles = _plan_tiles(batch, seq)
    tq = tiles.attn
    assert d == D_MODEL and w_in.shape[1:] == (D_MODEL, W_IN_DV + DIFF_W + N_IN_SMALL - LANES + MLA_ROPE
                                               + N_BRANCH * D_MODEL)
    assert seq % RET_CHUNK == 0 and seq % tq == 0
    assert DIFF_HEADS % tiles.diff_group == 0 and MLA_HEADS % tiles.mla_group == 0
    ret_tabs = _ret_tables(seq)
    mla_tabs = _mla_tables(seq)
    bias_tab = _bias_table(rel_bias, tq)
    causal_tab = _causal_table(tq)
    colscale = _in_colscale()
    w_main, w_dvt, w_gates, w_small = _prep_w_in(w_in, tn=tiles.in_cols)
    merge_w = tuple(w.astype(BF16) for w in (w_ret_o, w_diff_o, w_mla_o, w_out))
    mlp_w = (w_up.astype(BF16), w_down.astype(BF16))
    row = lambda v: v.reshape(1, -1).astype(F32)

    x2 = x.reshape(t, d)
    for l in range(depth):
        wk, wvt = _prep_w_ukv(w_mla_ukv[l])
        proj, dvt, qt, km, vt = _in_proj(x2, row(pre_mix_g[l]), w_main, w_dvt, w_small, l, colscale,
                                         row(mla_q_norm_g[l]), row(mla_kv_norm_g[l]),
                                         _prep_w_uq(w_mla_uq[l]), wk, wvt, mla_tabs,
                                         seq=seq, tm=tiles.in_rows, tn=tiles.in_cols, tk=tq)

        ret = _retention(proj, ret_tabs, batch=batch, seq=seq)

        lambda_init = 0.8 - 0.6 * math.exp(-0.3 * l)
        lam = (jnp.exp(jnp.sum(lambda_q1[l] * lambda_k1[l]).astype(F32))
               - jnp.exp(jnp.sum(lambda_q2[l] * lambda_k2[l]).astype(F32)) + lambda_init)
        subln_g = jnp.broadcast_to((diff_subln_g[l].astype(F32) * (1.0 - lambda_init))[:, None], (2 * DIFF_DH, tq))
        da = _diff_attention(proj, dvt, lam.reshape(1), bias_tab, subln_g,
                             batch=batch, seq=seq, tq=tq, group=tiles.diff_group)

        mo = _mla_attention(qt, km, vt, causal_tab, batch=batch, seq=seq, tq=tq, group=tiles.mla_group)

        x2 = _merge(x2, ret, da, mo, row(pre_mix_g[l]), w_gates, *merge_w, l, row(post_mix_g[l]),
                    tm=tiles.merge_rows)

        x2 = _mlp(x2, row(pre_mlp_g[l]), *mlp_w, l, row(post_mlp_g[l]), tm=tiles.mlp_rows, tf=tiles.mlp_cols)
    return x2.reshape(batch, seq, d)
```

```python
import functools
import math
from typing import NamedTuple

import jax
import jax.numpy as jnp
import numpy as np
from jax import lax
from jax.experimental import pallas as pl
from jax.experimental.pallas import tpu as pltpu

F32 = jnp.float32
BF16 = jnp.bfloat16

D_MODEL = 1024
EPS = 1e-6
RET_HEADS = 4
RET_DK = 128
RET_DV = 256
RET_CHUNK = 128
DIFF_HEADS = 8
DIFF_DH = 64
MLA_HEADS = 16
MLA_Q_LORA = 256
MLA_KV_LORA = 128
MLA_NOPE = 64
MLA_ROPE = 32
MLA_V = 64
ROPE_THETA = 10000.0
REL_BUCKETS = 32
REL_MAX_DIST = 128
N_BRANCH = 3

LANES = 128
V7X_VMEM_BYTES = 64 * 1024 * 1024
VMEM_LIMIT = V7X_VMEM_BYTES * 7 // 8
LOG2E = math.log2(math.e)
NEG_BIG = -1e30

RET_QK_W = RET_HEADS * RET_DK
RET_V_W = RET_HEADS * RET_DV
DIFF_W = DIFF_HEADS * 2 * DIFF_DH
MLA_QH = MLA_NOPE + MLA_ROPE
MLA_KVH = MLA_NOPE + MLA_V
MLA_V_W = MLA_HEADS * MLA_V
MLA_PAD_W = MLA_HEADS * LANES

COL_RQ = 0
COL_RK = COL_RQ + RET_QK_W
COL_RV = COL_RK + RET_QK_W
COL_RG = COL_RV + RET_V_W
COL_DK = COL_RG + RET_V_W
N_IN_MAIN = COL_DK + DIFF_W
W_IN_DQ = COL_DK
W_IN_DK = W_IN_DQ + DIFF_W
W_IN_DV = W_IN_DK + DIFF_W
DIFF_Q_SCALE = DIFF_DH ** -0.5 * LOG2E
N_IN_SMALL = MLA_Q_LORA + MLA_KV_LORA + LANES


def _cparams(*sem):
    return pltpu.CompilerParams(dimension_semantics=sem, vmem_limit_bytes=VMEM_LIMIT)


def _rms(xf):
    return xf * lax.rsqrt(jnp.mean(xf * xf, axis=-1, keepdims=True) + EPS)


def _dot(a, b):
    return jnp.dot(a, b, preferred_element_type=F32)


def _dot_nt(a, b):
    return lax.dot_general(a, b, (((1,), (1,)), ((), ())), preferred_element_type=F32)


HEADS_PER_UP = 4


def _in_proj_kernel(x_ref, g_ref, wm_ref, wdvt_ref, ws_ref, cs_ref, gq_ref, gkv_ref, wqt_ref, wk_ref, wvt_ref,
                    cqt_tab, sqt_tab, ck_tab, sk_tab, o_ref, dvt_ref, qt_ref, k_ref, vt_ref, *, tn):
    h = (_rms(x_ref[...]) * g_ref[...]).astype(BF16)

    small = _dot(h, ws_ref[...])
    hq = (_rms(small[:, :MLA_Q_LORA]) * gq_ref[...]).astype(BF16)
    hkv = (_rms(small[:, MLA_Q_LORA:MLA_Q_LORA + MLA_KV_LORA]) * gkv_ref[...]).astype(BF16)
    kpe = small[:, MLA_Q_LORA + MLA_KV_LORA:]
    kpe = kpe * ck_tab[...] + pltpu.roll(kpe, LANES - MLA_ROPE, 1) * sk_tab[...]
    tk = vt_ref.shape[-1]
    n_tiles = vt_ref.shape[0]
    for c in range(n_tiles):
        vt_ref[c] = _dot_nt(wvt_ref[...], hkv[c * tk:(c + 1) * tk]).astype(vt_ref.dtype)
    cqt = cqt_tab[...]
    sqt = sqt_tab[...]
    up = HEADS_PER_UP * LANES
    for hg in range(MLA_HEADS // HEADS_PER_UP):
        q_t = _dot_nt(wqt_ref[hg * up:(hg + 1) * up, :], hq)
        ka = _dot(hkv, wk_ref[:, hg * up:(hg + 1) * up])
        for hh in range(HEADS_PER_UP):
            hs = slice(hh * LANES, (hh + 1) * LANES)
            out = slice(hg * up + hh * LANES, hg * up + (hh + 1) * LANES)
            blk = q_t[hs]
            partner = jnp.concatenate([blk[MLA_ROPE:], blk[:MLA_ROPE]], axis=0)
            blk = blk * cqt + partner * sqt
            for c in range(n_tiles):
                qt_ref[c, out, :] = blk[:, c * tk:(c + 1) * tk].astype(qt_ref.dtype)
            k_ref[:, out] = (ka[:, hs] + kpe).astype(k_ref.dtype)

    for c in range(wm_ref.shape[1] // tn):
        cols = slice(c * tn, (c + 1) * tn)
        o_ref[:, cols] = (_dot(h, wm_ref[:, cols]) * cs_ref[:, cols]).astype(o_ref.dtype)
    for r in range(wdvt_ref.shape[0] // tn):
        rows = slice(r * tn, (r + 1) * tn)
        dqv_t = _dot_nt(wdvt_ref[rows, :], h)
        if r * tn < DIFF_W:
            dqv_t = dqv_t * DIFF_Q_SCALE
        for c in range(n_tiles):
            dvt_ref[c, rows, :] = dqv_t[:, c * tk:(c + 1) * tk].astype(dvt_ref.dtype)


def _resident(shape):
    return pl.BlockSpec(shape, lambda *_: (0,) * len(shape), pipeline_mode=pl.Buffered(1))


def _layer_slab(w, layer):
    return pl.BlockSpec((None,) + w.shape[1:], lambda *_: (layer, 0, 0), pipeline_mode=pl.Buffered(1))


def _in_proj(x2, g, w_main, w_dvt, w_small, layer, colscale, gq, gkv, wqt, wk, wvt, tabs, *, seq, tm, tn, tk):
    t, d = x2.shape
    n = w_main.shape[2]
    assert n % tn == 0 and w_dvt.shape[1] % tn == 0 and tm % tk == 0 and seq % tm == 0
    ns = seq // tm
    nc = tm // tk
    cqt, sqt, ck, sk = tabs
    tab_t = pl.BlockSpec((LANES, tm), lambda i: (0, i % ns))
    tab = pl.BlockSpec((tm, LANES), lambda i: (i % ns, 0))
    return pl.pallas_call(
        functools.partial(_in_proj_kernel, tn=tn),
        grid=(t // tm,),
        in_specs=[pl.BlockSpec((tm, d), lambda i: (i, 0)), _resident((1, d)),
                  _layer_slab(w_main, layer), _layer_slab(w_dvt, layer), _layer_slab(w_small, layer),
                  _resident((1, n)), _resident(gq.shape), _resident(gkv.shape),
                  _resident(wqt.shape), _resident(wk.shape), _resident(wvt.shape),
                  tab_t, tab_t, tab, tab],
        out_specs=[pl.BlockSpec((tm, n), lambda i: (i, 0)),
                   pl.BlockSpec((nc, w_dvt.shape[1], tk), lambda i: (i, 0, 0)),
                   pl.BlockSpec((nc, MLA_PAD_W, tk), lambda i: (i, 0, 0)),
                   pl.BlockSpec((tm, MLA_PAD_W), lambda i: (i, 0)),
                   pl.BlockSpec((nc, MLA_V_W, tk), lambda i: (i, 0, 0))],
        out_shape=[jax.ShapeDtypeStruct((t, n), BF16),
                   jax.ShapeDtypeStruct((t // tk, w_dvt.shape[1], tk), BF16),
                   jax.ShapeDtypeStruct((t // tk, MLA_PAD_W, tk), BF16),
                   jax.ShapeDtypeStruct((t, MLA_PAD_W), BF16),
                   jax.ShapeDtypeStruct((t // tk, MLA_V_W, tk), BF16)],
        compiler_params=_cparams("parallel"),
        name="in_proj",
    )(x2, g, w_main, w_dvt, w_small, colscale, gq, gkv, wqt, wk, wvt, cqt, sqt, ck, sk)


def _ret_kernel(q_ref, k_ref, v_ref, rg_ref, cos_ref, sin_ref, dm_ref, xi_ref, zt_ref, gc_ref,
                o_ref, st_ref):
    st_ref[...] = jnp.zeros_like(st_ref)

    def chunk(n, carry):
        rows = _tile(n, RET_CHUNK)
        cos = cos_ref[rows, :]
        sin = sin_ref[rows, :]
        for h in range(RET_HEADS):
            qs = slice(h * RET_DK, (h + 1) * RET_DK)
            vs = slice(h * RET_DV, (h + 1) * RET_DV)
            q = q_ref[rows, qs].astype(F32)
            k = k_ref[rows, qs].astype(F32)
            qr = q * cos + pltpu.roll(q, RET_DK // 2, 1) * sin
            kr = k * cos + pltpu.roll(k, RET_DK // 2, 1) * sin
            v = v_ref[rows, vs]
            st = st_ref[h]
            inner = _dot_nt(qr.astype(BF16), kr.astype(BF16)) * dm_ref[h]
            out = _dot(inner.astype(BF16), v) + _dot((qr * xi_ref[h]).astype(BF16), st.astype(BF16))
            kz_t = (kr * zt_ref[h]).T.astype(BF16)
            st_ref[h] = gc_ref[h] * st + _dot(kz_t, v)
            rg = rg_ref[rows, vs].astype(F32)
            o_ref[rows, vs] = (_rms(out) * (rg * jax.nn.sigmoid(rg))).astype(o_ref.dtype)
        return carry

    lax.fori_loop(0, q_ref.shape[0] // RET_CHUNK, chunk, 0, unroll=4)


def _retention(proj, tabs, *, batch, seq):
    t = proj.shape[0]
    c = RET_CHUNK
    cos, sin, dmask, xi, zeta, gch = tabs
    return pl.pallas_call(
        _ret_kernel,
        grid=(batch,),
        in_specs=[pl.BlockSpec((seq, RET_QK_W), lambda b: (b, COL_RQ // RET_QK_W)),
                  pl.BlockSpec((seq, RET_QK_W), lambda b: (b, COL_RK // RET_QK_W)),
                  pl.BlockSpec((seq, RET_V_W), lambda b: (b, COL_RV // RET_V_W)),
                  pl.BlockSpec((seq, RET_V_W), lambda b: (b, COL_RG // RET_V_W)),
                  _resident((seq, RET_DK)), _resident((seq, RET_DK)),
                  _resident((RET_HEADS, c, c)), _resident((RET_HEADS, c, RET_DK)),
                  _resident((RET_HEADS, c, RET_DK)), _resident((RET_HEADS, 1, RET_DV))],
        out_specs=pl.BlockSpec((seq, RET_V_W), lambda b: (b, 0)),
        out_shape=jax.ShapeDtypeStruct((t, RET_V_W), BF16),
        scratch_shapes=[pltpu.VMEM((RET_HEADS, RET_DK, RET_DV), F32)],
        compiler_params=_cparams("parallel"),
        name="retention",
    )(proj, proj, proj, proj, cos, sin, dmask, xi, zeta, gch)


SUM_ROWS = 16


def _store_logits(s, s_ref, smax_ref):
    s_ref[...] = s
    smax_ref[...] = jnp.max(s, axis=0, keepdims=True)


def _softmax_update(s_ref, smax_ref, vt, m_ref, acc_ref):
    m = m_ref[...]
    m_new = jnp.maximum(m, smax_ref[...])
    alpha = jnp.exp2(m - m_new)
    p = jnp.exp2(s_ref[...] - m_new).astype(BF16)
    m_ref[...] = m_new
    vt_ones = jnp.concatenate([vt, jnp.ones((SUM_ROWS, vt.shape[1]), BF16)], axis=0)
    acc_ref[...] = alpha * acc_ref[...] + _dot(vt_ones, p)


def _softmax_result(acc_ref):
    dv = acc_ref.shape[0] - SUM_ROWS
    return acc_ref[:dv, :] / acc_ref[dv:dv + 1, :]


def _softmax_reset(m_ref, acc_ref):
    m_ref[...] = jnp.full(m_ref.shape, NEG_BIG, F32)
    acc_ref[...] = jnp.zeros(acc_ref.shape, F32)


def _softmax_restart(m_ref):
    m_ref[...] = jnp.full(m_ref.shape, NEG_BIG, F32)


def _softmax_scratch(group, dv, tk, n):
    return [pltpu.VMEM((2, group, tk, n), F32), pltpu.VMEM((2, group, 1, n), F32),
            pltpu.VMEM((2, group, 1, n), F32), pltpu.VMEM((2, group, dv + SUM_ROWS, n), F32)]


def _tile(i, size):
    return pl.ds(pl.multiple_of(i * size, size), size)


def _causal_tile_pipeline(nq, group, logits, softmax, finalize):
    def following(qi, j):
        last = j == qi
        return jnp.where(last, qi + 1, qi), jnp.where(last, 0, j + 1)

    def two_pairs(_, carry):
        qa, ja = carry
        qb, jb = following(qa, ja)
        qc, jc = following(qb, jb)
        qc_valid = jnp.minimum(qc, nq - 1)
        for g in range(group):
            logits(g, qb, jb, 1)
        for g in range(group):
            softmax(g, qa, ja, 0)
        for g in range(group):
            logits(g, qc_valid, jc, 0)
        for g in range(group):
            softmax(g, qb, jb, 1)

        @pl.when(ja == qa)
        def _():
            finalize(qa)

        @pl.when(jb == qb)
        def _():
            finalize(qb)

        return qc, jc

    n_pairs = nq * (nq + 1) // 2
    zero = jnp.int32(0)
    for g in range(group):
        logits(g, zero, zero, 0)
    q_last, j_last = lax.fori_loop(0, n_pairs // 2, two_pairs, (zero, zero))
    if n_pairs % 2:
        for g in range(group):
            softmax(g, q_last, j_last, 0)
        finalize(q_last)


def _diff_kernel(lam_ref, qt_ref, k_ref, vt_ref, tab_ref, gt_ref, o_ref, s_ref, smax_ref, m_ref, acc_ref,
                 *, tq, group):
    w = 2 * DIFF_DH
    nq = vt_ref.shape[0]
    hslice = [slice(g * w, (g + 1) * w) for g in range(group)]
    _softmax_reset(m_ref, acc_ref)
    zeros = jnp.zeros((DIFF_DH, tq), BF16)

    def logits(g, qi, j, slot):
        qt = qt_ref[qi, hslice[g], :]
        q2 = jnp.concatenate([jnp.concatenate([qt[:DIFF_DH], zeros], axis=0),
                              jnp.concatenate([zeros, qt[DIFF_DH:]], axis=0)], axis=1)
        s = _dot(k_ref[_tile(j, tq), hslice[g]], q2)
        bias = tab_ref[g, jnp.minimum(qi - j, tab_ref.shape[1] - 1)]
        s = s + jnp.concatenate([bias, bias], axis=1)
        _store_logits(s, s_ref.at[slot, g], smax_ref.at[slot, g])

    def softmax(g, qi, j, slot):
        _softmax_update(s_ref.at[slot, g], smax_ref.at[slot, g], vt_ref[j, hslice[g], :],
                        m_ref.at[qi % 2, g], acc_ref.at[qi % 2, g])

    def finalize(qi):
        for g in range(group):
            o = _softmax_result(acc_ref.at[qi % 2, g])
            a = o[:, :tq] - lam_ref[0] * o[:, tq:]
            a = a * lax.rsqrt(jnp.mean(a * a, axis=0, keepdims=True) + EPS) * gt_ref[...]
            o_ref[_tile(qi, tq), hslice[g]] = a.T.astype(o_ref.dtype)
            _softmax_restart(m_ref.at[qi % 2, g])

    _causal_tile_pipeline(nq, group, logits, softmax, finalize)


def _diff_attention(proj, qvt, lam, tab, g_t, *, batch, seq, tq, group):
    t = proj.shape[0]
    nq = seq // tq
    w = group * 2 * DIFF_DH
    return pl.pallas_call(
        functools.partial(_diff_kernel, tq=tq, group=group),
        grid=(batch, DIFF_HEADS // group),
        in_specs=[pl.BlockSpec(memory_space=pltpu.SMEM),
                  pl.BlockSpec((nq, w, tq), lambda b, h: (b, h, 0)),
                  pl.BlockSpec((seq, w), lambda b, h: (b, COL_DK // w + h)),
                  pl.BlockSpec((nq, w, tq), lambda b, h: (b, DIFF_W // w + h, 0)),
                  pl.BlockSpec((group,) + tab.shape[1:], lambda b, h: (h, 0, 0, 0)),
                  _resident(g_t.shape)],
        out_specs=pl.BlockSpec((seq, w), lambda b, h: (b, h)),
        out_shape=jax.ShapeDtypeStruct((t, DIFF_W), BF16),
        scratch_shapes=_softmax_scratch(group, 2 * DIFF_DH, tq, 2 * tq),
        compiler_params=_cparams("parallel", "parallel"),
        name="diff_attention",
    )(lam, qvt, proj, qvt, tab, g_t)


def _mla_kernel(qt_ref, k_ref, vt_ref, mask_ref, o_ref, s_ref, smax_ref, m_ref, acc_ref, *, tq, group):
    hslice = [slice(g * LANES, (g + 1) * LANES) for g in range(group)]
    vslice = [slice(g * MLA_V, (g + 1) * MLA_V) for g in range(group)]
    _softmax_reset(m_ref, acc_ref)

    def logits(g, qi, j, slot):
        qt = jnp.concatenate([qt_ref[qi, hslice[g], :][:MLA_QH], jnp.zeros((LANES - MLA_QH, tq), BF16)], axis=0)
        s = _dot(k_ref[_tile(j, tq), hslice[g]], qt)
        s = s + mask_ref[jnp.minimum(qi - j, 1)]
        _store_logits(s, s_ref.at[slot, g], smax_ref.at[slot, g])

    def softmax(g, qi, j, slot):
        _softmax_update(s_ref.at[slot, g], smax_ref.at[slot, g], vt_ref[j, vslice[g], :],
                        m_ref.at[qi % 2, g], acc_ref.at[qi % 2, g])

    def finalize(qi):
        o = jnp.concatenate([_softmax_result(acc_ref.at[qi % 2, g]) for g in range(group)], axis=0)
        o_ref[_tile(qi, tq), :] = o.T.astype(o_ref.dtype)
        for g in range(group):
            _softmax_restart(m_ref.at[qi % 2, g])

    _causal_tile_pipeline(vt_ref.shape[0], group, logits, softmax, finalize)


def _mla_attention(qt, km, vt, mask, *, batch, seq, tq, group):
    t = km.shape[0]
    nq = seq // tq
    return pl.pallas_call(
        functools.partial(_mla_kernel, tq=tq, group=group),
        grid=(batch, MLA_HEADS // group),
        in_specs=[pl.BlockSpec((nq, group * LANES, tq), lambda b, h: (b, h, 0)),
                  pl.BlockSpec((seq, group * LANES), lambda b, h: (b, h)),
                  pl.BlockSpec((nq, group * MLA_V, tq), lambda b, h: (b, h, 0)),
                  _resident(mask.shape)],
        out_specs=pl.BlockSpec((seq, group * MLA_V), lambda b, h: (b, h)),
        out_shape=jax.ShapeDtypeStruct((t, MLA_V_W), BF16),
        scratch_shapes=_softmax_scratch(group, MLA_V, tq, tq),
        compiler_params=_cparams("parallel", "parallel"),
        name="mla_attention",
    )(qt, km, vt, mask)


def _merge_kernel(x_ref, a_ref, b_ref, c_ref, g_ref, wg_ref, wa_ref, wb_ref, wc_ref, wo_ref, pg_ref, o_ref):
    x = x_ref[...]
    d = x.shape[1]
    h = (_rms(x) * g_ref[...]).astype(BF16)
    merged = None
    for n, (br_ref, w_ref) in enumerate(((a_ref, wa_ref), (b_ref, wb_ref), (c_ref, wc_ref))):
        gate = jax.nn.sigmoid(_dot(h, wg_ref[:, n * d:(n + 1) * d]))
        part = gate * _dot(br_ref[...], w_ref[...])
        merged = part if merged is None else merged + part
    y = _dot(merged.astype(BF16), wo_ref[...])
    o_ref[...] = x + _rms(y) * pg_ref[...]


def _merge(x2, a, b, c, g, w_gates, wa, wb, wc, wo, layer, pg, *, tm):
    t, d = x2.shape
    rowblk = pl.BlockSpec((tm, d), lambda i: (i, 0))
    return pl.pallas_call(
        _merge_kernel,
        grid=(t // tm,),
        in_specs=[rowblk, rowblk, rowblk, rowblk, _resident((1, d))]
        + [_layer_slab(w, layer) for w in (w_gates, wa, wb, wc, wo)] + [_resident((1, d))],
        out_specs=rowblk,
        out_shape=jax.ShapeDtypeStruct((t, d), F32),
        compiler_params=_cparams("parallel"),
        name="merge",
    )(x2, a, b, c, g, w_gates, wa, wb, wc, wo, pg)


def _mlp_kernel(x_ref, g_ref, wu_ref, wd_ref, pg_ref, o_ref, *, tf):
    x = x_ref[...]
    h = (_rms(x) * g_ref[...]).astype(BF16)
    acc = None
    for c in range(wu_ref.shape[1] // tf):
        u = jnp.maximum(_dot(h, wu_ref[:, c * tf:(c + 1) * tf]), 0.0)
        part = _dot((u * u).astype(BF16), wd_ref[c * tf:(c + 1) * tf, :])
        acc = part if acc is None else acc + part
    o_ref[...] = x + _rms(acc) * pg_ref[...]


def _mlp(x2, g, wu, wd, layer, pg, *, tm, tf):
    t, d = x2.shape
    rowblk = pl.BlockSpec((tm, d), lambda i: (i, 0))
    return pl.pallas_call(
        functools.partial(_mlp_kernel, tf=tf),
        grid=(t // tm,),
        in_specs=[rowblk, _resident((1, d)), _layer_slab(wu, layer), _layer_slab(wd, layer), _resident((1, d))],
        out_specs=rowblk,
        out_shape=jax.ShapeDtypeStruct((t, d), F32),
        compiler_params=_cparams("parallel"),
        name="mlp",
    )(x2, g, wu, wd, pg)


def _rot_half_cols(w):
    half = w.shape[-1] // 2
    return jnp.concatenate([-w[..., half:], w[..., :half]], axis=-1)


def _cast_transposed_kernel(wt_ref, o_ref):
    o_ref[...] = wt_ref[...].T.astype(o_ref.dtype)


def _cast_transposed(wt, n, *, tn, skip=None):
    depth, _, d = wt.shape
    if skip is None:
        src = lambda c: c
    else:
        src = lambda c: jnp.where(c < skip[0] // tn, c, c + (skip[1] - skip[0]) // tn)
    return pl.pallas_call(
        _cast_transposed_kernel,
        grid=(depth, n // tn),
        in_specs=[pl.BlockSpec((None, tn, d), lambda l, c: (l, src(c), 0))],
        out_specs=pl.BlockSpec((None, d, tn), lambda l, c: (l, 0, c)),
        out_shape=jax.ShapeDtypeStruct((depth, d, n), BF16),
        compiler_params=_cparams("parallel", "parallel"),
        name="w_in_prep",
    )(wt)


def _prep_w_in(w_in, *, tn):
    wt = jnp.swapaxes(w_in, 1, 2)
    cq_at = W_IN_DV + DIFF_W
    kpe_at = cq_at + MLA_Q_LORA + MLA_KV_LORA
    gates_at = kpe_at + MLA_ROPE
    kpe = wt[:, kpe_at:gates_at]
    half = MLA_ROPE // 2
    zeros = jnp.zeros((wt.shape[0], MLA_NOPE, wt.shape[2]), wt.dtype)
    small = jnp.concatenate([wt[:, cq_at:kpe_at], zeros, kpe, -kpe[:, half:], kpe[:, :half]], axis=1)
    dqv = jnp.concatenate([wt[:, W_IN_DQ:W_IN_DK], wt[:, W_IN_DV:cq_at]], axis=1)
    return (_cast_transposed(wt, N_IN_MAIN, tn=tn, skip=(W_IN_DQ, W_IN_DK)),
            dqv.astype(BF16),
            _cast_transposed(wt[:, gates_at:], N_BRANCH * D_MODEL, tn=tn),
            _cast_transposed(small, N_IN_SMALL, tn=tn))


def _in_colscale():
    cs = np.ones((1, N_IN_MAIN), np.float32)
    cs[0, COL_RK:COL_RK + RET_QK_W] = RET_DK ** -0.5
    return jnp.asarray(cs)


def _prep_w_uq(w):
    w = w.reshape(MLA_Q_LORA, MLA_HEADS, MLA_QH)
    pe = w[..., MLA_NOPE:]
    w = jnp.concatenate([w, _rot_half_cols(pe)], axis=-1).reshape(MLA_Q_LORA, MLA_PAD_W)
    return w.T.astype(BF16)


def _prep_w_ukv(w):
    w = w.reshape(MLA_KV_LORA, MLA_HEADS, MLA_KVH)
    wk = jnp.concatenate([w[..., :MLA_NOPE], jnp.zeros_like(w[..., :MLA_NOPE])], axis=-1)
    return (wk.reshape(MLA_KV_LORA, MLA_PAD_W).astype(BF16),
            w[..., MLA_NOPE:].reshape(MLA_KV_LORA, MLA_V_W).T.astype(BF16))


def _ret_tables(seq):
    pos = np.arange(seq, dtype=np.float64)
    inv_freq = 1.0 / (10000.0 ** np.linspace(0.0, 1.0, RET_DK // 2))
    ang = pos[:, None] * inv_freq[None, :]
    cos, sin = np.cos(ang), np.sin(ang)
    cos_t = np.concatenate([cos, cos], axis=-1)
    sin_t = np.concatenate([-sin, sin], axis=-1)
    log_g = np.log1p(-np.exp2(-5.0 - np.arange(RET_HEADS, dtype=np.float64)))
    idx = np.arange(RET_CHUNK, dtype=np.float64)
    rel = idx[:, None] - idx[None, :]
    dmask = np.where(rel >= 0, np.exp(np.maximum(rel, 0.0)[None] * log_g[:, None, None]), 0.0)
    xi = np.exp((idx + 1.0)[None, :] * log_g[:, None])[:, :, None]
    zeta = np.exp((RET_CHUNK - 1.0 - idx)[None, :] * log_g[:, None])[:, :, None]
    g_chunk = np.exp(RET_CHUNK * log_g)[:, None, None]
    tabs = (cos_t, sin_t, dmask,
            np.broadcast_to(xi, (RET_HEADS, RET_CHUNK, RET_DK)),
            np.broadcast_to(zeta, (RET_HEADS, RET_CHUNK, RET_DK)),
            np.broadcast_to(g_chunk, (RET_HEADS, 1, RET_DV)))
    return tuple(jnp.asarray(t, F32) for t in tabs)


def _mla_tables(seq):
    pos = np.arange(seq, dtype=np.float64)
    inv_freq = 1.0 / (ROPE_THETA ** (np.arange(0, MLA_ROPE, 2, dtype=np.float64) / MLA_ROPE))
    ang = pos[:, None] * inv_freq[None, :]
    cos, sin = np.cos(ang), np.sin(ang)
    z_nope = np.zeros((seq, MLA_NOPE))
    z_rope = np.zeros((seq, MLA_ROPE))
    cos_k = np.concatenate([z_nope, cos, cos, z_rope], axis=-1)
    sin_k = np.concatenate([z_nope, sin, sin, z_rope], axis=-1)
    qscale = MLA_QH ** -0.5 * LOG2E
    cos_q = np.concatenate([np.ones((seq, MLA_NOPE)), cos, cos, z_rope], axis=-1) * qscale
    sin_q = sin_k * qscale
    return tuple(jnp.asarray(t, F32) for t in (cos_q.T, sin_q.T, cos_k, sin_k))


def _t5_bucket(n):
    max_exact = REL_BUCKETS // 2
    nf = jnp.maximum(n, 1).astype(F32)
    large = max_exact + (jnp.log(nf / max_exact) / math.log(REL_MAX_DIST / max_exact)
                         * (REL_BUCKETS - max_exact)).astype(jnp.int32)
    large = jnp.minimum(large, REL_BUCKETS - 1)
    return jnp.where(n < max_exact, n, large)


def _toeplitz(g, tq):
    h = g.shape[0]
    padded = jnp.concatenate([g, jnp.zeros((h, 1), g.dtype)], axis=1)
    skewed = jnp.tile(padded, (1, tq))[:, :tq * (2 * tq - 1)].reshape(h, tq, 2 * tq - 1)
    return skewed[:, :, tq - 1:]


def _bias_table(rel_bias, tq):
    assert tq >= REL_MAX_DIST
    bias = rel_bias[_t5_bucket(jnp.arange(2 * tq))].astype(F32).T
    bias = (bias - rel_bias[REL_BUCKETS - 1].astype(F32)[:, None]) * LOG2E
    masked = jnp.full((bias.shape[0], tq - 1), NEG_BIG, F32)
    diagonal = jnp.concatenate([masked, bias[:, :tq]], axis=1)
    behind = bias[:, 1:]
    zeros = jnp.zeros((bias.shape[0], tq, tq), F32)
    return jnp.stack([_toeplitz(diagonal, tq), _toeplitz(behind, tq), zeros], axis=1)


def _causal_table(tq):
    kk = np.arange(tq)[:, None]
    qq = np.arange(tq)[None, :]
    return jnp.asarray(np.stack([np.where(qq >= kk, 0.0, NEG_BIG), np.zeros((tq, tq))]), F32)


def _pick(t, want):
    return want if t % want == 0 else t


class _Tiles(NamedTuple):
    attn: int
    in_rows: int
    in_cols: int
    merge_rows: int
    mlp_rows: int
    mlp_cols: int
    diff_group: int
    mla_group: int


def _plan_tiles(batch, seq):
    t = batch * seq
    return _Tiles(attn=_pick(seq, 256), in_rows=_pick(seq, 512), in_cols=512, merge_rows=_pick(t, 512),
                  mlp_rows=_pick(t, 1024), mlp_cols=1024, diff_group=4, mla_group=8)


def kernel(x, rel_bias, pre_mix_g, w_in, w_ret_o, lambda_q1, lambda_k1, lambda_q2, lambda_k2, diff_subln_g, w_diff_o, mla_q_norm_g, w_mla_uq, mla_kv_norm_g, w_mla_ukv, w_mla_o, w_out, post_mix_g, pre_mlp_g, w_up, w_down, post_mlp_g):
    batch, seq, d = x.shape
    t = batch * seq
    depth = w_in.shape[0]
    tiles = _plan_tiles(batch, seq)
    tq = tiles.attn
    assert d == D_MODEL and w_in.shape[1:] == (D_MODEL, W_IN_DV + DIFF_W + N_IN_SMALL - LANES + MLA_ROPE
                                               + N_BRANCH * D_MODEL)
    assert seq % RET_CHUNK == 0 and seq % tq == 0
    assert DIFF_HEADS % tiles.diff_group == 0 and MLA_HEADS % tiles.mla_group == 0
    ret_tabs = _ret_tables(seq)
    mla_tabs = _mla_tables(seq)
    bias_tab = _bias_table(rel_bias, tq)
    causal_tab = _causal_table(tq)
    colscale = _in_colscale()
    w_main, w_dvt, w_gates, w_small = _prep_w_in(w_in, tn=tiles.in_cols)
    merge_w = tuple(w.astype(BF16) for w in (w_ret_o, w_diff_o, w_mla_o, w_out))
    mlp_w = (w_up.astype(BF16), w_down.astype(BF16))
    row = lambda v: v.reshape(1, -1).astype(F32)

    x2 = x.reshape(t, d)
    for l in range(depth):
        wk, wvt = _prep_w_ukv(w_mla_ukv[l])
        proj, dvt, qt, km, vt = _in_proj(x2, row(pre_mix_g[l]), w_main, w_dvt, w_small, l, colscale,
                                         row(mla_q_norm_g[l]), row(mla_kv_norm_g[l]),
                                         _prep_w_uq(w_mla_uq[l]), wk, wvt, mla_tabs,
                                         seq=seq, tm=tiles.in_rows, tn=tiles.in_cols, tk=tq)

        ret = _retention(proj, ret_tabs, batch=batch, seq=seq)

        lambda_init = 0.8 - 0.6 * math.exp(-0.3 * l)
        lam = (jnp.exp(jnp.sum(lambda_q1[l] * lambda_k1[l]).astype(F32))
               - jnp.exp(jnp.sum(lambda_q2[l] * lambda_k2[l]).astype(F32)) + lambda_init)
        subln_g = jnp.broadcast_to((diff_subln_g[l].astype(F32) * (1.0 - lambda_init))[:, None], (2 * DIFF_DH, tq))
        da = _diff_attention(proj, dvt, lam.reshape(1), bias_tab, subln_g,
                             batch=batch, seq=seq, tq=tq, group=tiles.diff_group)

        mo = _mla_attention(qt, km, vt, causal_tab, batch=batch, seq=seq, tq=tq, group=tiles.mla_group)

        x2 = _merge(x2, ret, da, mo, row(pre_mix_g[l]), w_gates, *merge_w, l, row(post_mix_g[l]),
                    tm=tiles.merge_rows)

        x2 = _mlp(x2, row(pre_mlp_g[l]), *mlp_w, l, row(post_mlp_g[l]), tm=tiles.mlp_rows, tf=tiles.mlp_cols)
    return x2.reshape(batch, seq, d)
```

```python
import functools
import math
from typing import NamedTuple

import jax
import jax.numpy as jnp
import numpy as np
from jax import lax
from jax.experimental import pallas as pl
from jax.experimental.pallas import tpu as pltpu

F32 = jnp.float32
BF16 = jnp.bfloat16

D_MODEL = 1024
EPS = 1e-6
RET_HEADS = 4
RET_DK = 128
RET_DV = 256
RET_CHUNK = 128
DIFF_HEADS = 8
DIFF_DH = 64
MLA_HEADS = 16
MLA_Q_LORA = 256
MLA_KV_LORA = 128
MLA_NOPE = 64
MLA_ROPE = 32
MLA_V = 64
ROPE_THETA = 10000.0
REL_BUCKETS = 32
REL_MAX_DIST = 128
N_BRANCH = 3

LANES = 128
V7X_VMEM_BYTES = 64 * 1024 * 1024
VMEM_LIMIT = V7X_VMEM_BYTES * 7 // 8
LOG2E = math.log2(math.e)
NEG_BIG = -1e30

RET_QK_W = RET_HEADS * RET_DK
RET_V_W = RET_HEADS * RET_DV
DIFF_W = DIFF_HEADS * 2 * DIFF_DH
MLA_QH = MLA_NOPE + MLA_ROPE
MLA_KVH = MLA_NOPE + MLA_V
MLA_V_W = MLA_HEADS * MLA_V
MLA_PAD_W = MLA_HEADS * LANES

COL_RQ = 0
COL_RK = COL_RQ + RET_QK_W
COL_RV = COL_RK + RET_QK_W
COL_RG = COL_RV + RET_V_W
COL_DK = COL_RG + RET_V_W
N_IN_MAIN = COL_DK + DIFF_W
W_IN_DQ = COL_DK
W_IN_DK = W_IN_DQ + DIFF_W
W_IN_DV = W_IN_DK + DIFF_W
DIFF_Q_SCALE = DIFF_DH ** -0.5 * LOG2E
N_IN_SMALL = MLA_Q_LORA + MLA_KV_LORA + LANES


def _cparams(*sem):
    return pltpu.CompilerParams(dimension_semantics=sem, vmem_limit_bytes=VMEM_LIMIT)


def _rms(xf):
    return xf * lax.rsqrt(jnp.mean(xf * xf, axis=-1, keepdims=True) + EPS)


def _dot(a, b):
    return jnp.dot(a, b, preferred_element_type=F32)


def _dot_nt(a, b):
    return lax.dot_general(a, b, (((1,), (1,)), ((), ())), preferred_element_type=F32)


HEADS_PER_UP = 4


def _in_proj_kernel(x_ref, g_ref, wm_ref, wdvt_ref, ws_ref, cs_ref, gq_ref, gkv_ref, wqt_ref, wk_ref, wvt_ref,
                    cqt_tab, sqt_tab, ck_tab, sk_tab, o_ref, dvt_ref, qt_ref, k_ref, vt_ref, *, tn):
    h = (_rms(x_ref[...]) * g_ref[...]).astype(BF16)

    small = _dot(h, ws_ref[...])
    hq = (_rms(small[:, :MLA_Q_LORA]) * gq_ref[...]).astype(BF16)
    hkv = (_rms(small[:, MLA_Q_LORA:MLA_Q_LORA + MLA_KV_LORA]) * gkv_ref[...]).astype(BF16)
    kpe = small[:, MLA_Q_LORA + MLA_KV_LORA:]
    kpe = kpe * ck_tab[...] + pltpu.roll(kpe, LANES - MLA_ROPE, 1) * sk_tab[...]
    tk = vt_ref.shape[-1]
    n_tiles = vt_ref.shape[0]
    for c in range(n_tiles):
        vt_ref[c] = _dot_nt(wvt_ref[...], hkv[c * tk:(c + 1) * tk]).astype(vt_ref.dtype)
    cqt = cqt_tab[...]
    sqt = sqt_tab[...]
    up = HEADS_PER_UP * LANES
    for hg in range(MLA_HEADS // HEADS_PER_UP):
        q_t = _dot_nt(wqt_ref[hg * up:(hg + 1) * up, :], hq)
        ka = _dot(hkv, wk_ref[:, hg * up:(hg + 1) * up])
        for hh in range(HEADS_PER_UP):
            hs = slice(hh * LANES, (hh + 1) * LANES)
            out = slice(hg * up + hh * LANES, hg * up + (hh + 1) * LANES)
            blk = q_t[hs]
            partner = jnp.concatenate([blk[MLA_ROPE:], blk[:MLA_ROPE]], axis=0)
            blk = blk * cqt + partner * sqt
            for c in range(n_tiles):
                qt_ref[c, out, :] = blk[:, c * tk:(c + 1) * tk].astype(qt_ref.dtype)
            k_ref[:, out] = (ka[:, hs] + kpe).astype(k_ref.dtype)

    for c in range(wm_ref.shape[1] // tn):
        cols = slice(c * tn, (c + 1) * tn)
        o_ref[:, cols] = (_dot(h, wm_ref[:, cols]) * cs_ref[:, cols]).astype(o_ref.dtype)
    for r in range(wdvt_ref.shape[0] // tn):
        rows = slice(r * tn, (r + 1) * tn)
        dqv_t = _dot_nt(wdvt_ref[rows, :], h)
        if r * tn < DIFF_W:
            dqv_t = dqv_t * DIFF_Q_SCALE
        for c in range(n_tiles):
            dvt_ref[c, rows, :] = dqv_t[:, c * tk:(c + 1) * tk].astype(dvt_ref.dtype)


def _resident(shape):
    return pl.BlockSpec(shape, lambda *_: (0,) * len(shape), pipeline_mode=pl.Buffered(1))


def _layer_slab(w, layer):
    return pl.BlockSpec((None,) + w.shape[1:], lambda *_: (layer, 0, 0), pipeline_mode=pl.Buffered(1))


def _in_proj(x2, g, w_main, w_dvt, w_small, layer, colscale, gq, gkv, wqt, wk, wvt, tabs, *, seq, tm, tn, tk):
    t, d = x2.shape
    n = w_main.shape[2]
    assert n % tn == 0 and w_dvt.shape[1] % tn == 0 and tm % tk == 0 and seq % tm == 0
    ns = seq // tm
    nc = tm // tk
    cqt, sqt, ck, sk = tabs
    tab_t = pl.BlockSpec((LANES, tm), lambda i: (0, i % ns))
    tab = pl.BlockSpec((tm, LANES), lambda i: (i % ns, 0))
    return pl.pallas_call(
        functools.partial(_in_proj_kernel, tn=tn),
        grid=(t // tm,),
        in_specs=[pl.BlockSpec((tm, d), lambda i: (i, 0)), _resident((1, d)),
                  _layer_slab(w_main, layer), _layer_slab(w_dvt, layer), _layer_slab(w_small, layer),
                  _resident((1, n)), _resident(gq.shape), _resident(gkv.shape),
                  _resident(wqt.shape), _resident(wk.shape), _resident(wvt.shape),
                  tab_t, tab_t, tab, tab],
        out_specs=[pl.BlockSpec((tm, n), lambda i: (i, 0)),
                   pl.BlockSpec((nc, w_dvt.shape[1], tk), lambda i: (i, 0, 0)),
                   pl.BlockSpec((nc, MLA_PAD_W, tk), lambda i: (i, 0, 0)),
                   pl.BlockSpec((tm, MLA_PAD_W), lambda i: (i, 0)),
                   pl.BlockSpec((nc, MLA_V_W, tk), lambda i: (i, 0, 0))],
        out_shape=[jax.ShapeDtypeStruct((t, n), BF16),
                   jax.ShapeDtypeStruct((t // tk, w_dvt.shape[1], tk), BF16),
                   jax.ShapeDtypeStruct((t // tk, MLA_PAD_W, tk), BF16),
                   jax.ShapeDtypeStruct((t, MLA_PAD_W), BF16),
                   jax.ShapeDtypeStruct((t // tk, MLA_V_W, tk), BF16)],
        compiler_params=_cparams("parallel"),
        name="in_proj",
    )(x2, g, w_main, w_dvt, w_small, colscale, gq, gkv, wqt, wk, wvt, cqt, sqt, ck, sk)


def _ret_kernel(q_ref, k_ref, v_ref, rg_ref, cos_ref, sin_ref, dm_ref, xi_ref, zt_ref, gc_ref,
                o_ref, st_ref):
    st_ref[...] = jnp.zeros_like(st_ref)

    def chunk(n, carry):
        rows = _tile(n, RET_CHUNK)
        cos = cos_ref[rows, :]
        sin = sin_ref[rows, :]
        for h in range(RET_HEADS):
            qs = slice(h * RET_DK, (h + 1) * RET_DK)
            vs = slice(h * RET_DV, (h + 1) * RET_DV)
            q = q_ref[rows, qs].astype(F32)
            k = k_ref[rows, qs].astype(F32)
            qr = q * cos + pltpu.roll(q, RET_DK // 2, 1) * sin
            kr = k * cos + pltpu.roll(k, RET_DK // 2, 1) * sin
            v = v_ref[rows, vs]
            st = st_ref[h]
            inner = _dot_nt(qr.astype(BF16), kr.astype(BF16)) * dm_ref[h]
            out = _dot(inner.astype(BF16), v) + _dot((qr * xi_ref[h]).astype(BF16), st.astype(BF16))
            kz_t = (kr * zt_ref[h]).T.astype(BF16)
            st_ref[h] = gc_ref[h] * st + _dot(kz_t, v)
            rg = rg_ref[rows, vs].astype(F32)
            o_ref[rows, vs] = (_rms(out) * (rg * jax.nn.sigmoid(rg))).astype(o_ref.dtype)
        return carry

    lax.fori_loop(0, q_ref.shape[0] // RET_CHUNK, chunk, 0, unroll=4)


def _retention(proj, tabs, *, batch, seq):
    t = proj.shape[0]
    c = RET_CHUNK
    cos, sin, dmask, xi, zeta, gch = tabs
    return pl.pallas_call(
        _ret_kernel,
        grid=(batch,),
        in_specs=[pl.BlockSpec((seq, RET_QK_W), lambda b: (b, COL_RQ // RET_QK_W)),
                  pl.BlockSpec((seq, RET_QK_W), lambda b: (b, COL_RK // RET_QK_W)),
                  pl.BlockSpec((seq, RET_V_W), lambda b: (b, COL_RV // RET_V_W)),
                  pl.BlockSpec((seq, RET_V_W), lambda b: (b, COL_RG // RET_V_W)),
                  _resident((seq, RET_DK)), _resident((seq, RET_DK)),
                  _resident((RET_HEADS, c, c)), _resident((RET_HEADS, c, RET_DK)),
                  _resident((RET_HEADS, c, RET_DK)), _resident((RET_HEADS, 1, RET_DV))],
        out_specs=pl.BlockSpec((seq, RET_V_W), lambda b: (b, 0)),
        out_shape=jax.ShapeDtypeStruct((t, RET_V_W), BF16),
        scratch_shapes=[pltpu.VMEM((RET_HEADS, RET_DK, RET_DV), F32)],
        compiler_params=_cparams("parallel"),
        name="retention",
    )(proj, proj, proj, proj, cos, sin, dmask, xi, zeta, gch)


SUM_ROWS = 16
PAIRS_PER_BLOCK = 4


def _store_logits(s, s_ref, smax_ref):
    s_ref[...] = s
    smax_ref[...] = jnp.max(s, axis=0, keepdims=True)


def _softmax_update(s_ref, smax_ref, vt, m_ref, acc_ref):
    m = m_ref[...]
    m_new = jnp.maximum(m, smax_ref[...])
    alpha = jnp.exp2(m - m_new)
    p = jnp.exp2(s_ref[...] - m_new).astype(BF16)
    m_ref[...] = m_new
    vt_ones = jnp.concatenate([vt, jnp.ones((SUM_ROWS, vt.shape[1]), BF16)], axis=0)
    acc_ref[...] = alpha * acc_ref[...] + _dot(vt_ones, p)


def _softmax_result(acc_ref):
    dv = acc_ref.shape[0] - SUM_ROWS
    return acc_ref[:dv, :] / acc_ref[dv:dv + 1, :]


def _softmax_reset(m_ref, acc_ref):
    m_ref[...] = jnp.full(m_ref.shape, NEG_BIG, F32)
    acc_ref[...] = jnp.zeros(acc_ref.shape, F32)


def _softmax_restart(m_ref):
    m_ref[...] = jnp.full(m_ref.shape, NEG_BIG, F32)


def _softmax_scratch(group, dv, tk, n):
    ring = PAIRS_PER_BLOCK
    return [pltpu.VMEM((2, group, tk, n), F32), pltpu.VMEM((2, group, 1, n), F32),
            pltpu.VMEM((ring, group, 1, n), F32), pltpu.VMEM((ring, group, dv + SUM_ROWS, n), F32)]


def _tile(i, size):
    return pl.ds(pl.multiple_of(i * size, size), size)


def _causal_tile_pipeline(nq, group, logits, softmax, finalize):
    def following(pair):
        qi, j = pair
        last = j == qi
        return jnp.where(last, qi + 1, qi), jnp.where(last, 0, j + 1)

    def run(first, count, prefetch_next):
        pairs = [first]
        for _ in range(count):
            pairs.append(following(pairs[-1]))
        for i in range(count):
            if i + 1 < count or prefetch_next:
                qn, jn = pairs[i + 1]
                qn = jnp.minimum(qn, nq - 1)
                for g in range(group):
                    logits(g, qn, jn, (i + 1) % 2)
            for g in range(group):
                softmax(g, *pairs[i], i % 2)
        for qi, j in pairs[:count]:
            @pl.when(j == qi)
            def _(qi=qi):
                finalize(qi)
        return pairs[count]

    n_pairs = nq * (nq + 1) // 2
    zero = jnp.int32(0)
    for g in range(group):
        logits(g, zero, zero, 0)
    rest = lax.fori_loop(0, n_pairs // PAIRS_PER_BLOCK, lambda _, c: run(c, PAIRS_PER_BLOCK, True), (zero, zero))
    if n_pairs % PAIRS_PER_BLOCK:
        run(rest, n_pairs % PAIRS_PER_BLOCK, False)


def _diff_kernel(lam_ref, qt_ref, k_ref, vt_ref, tab_ref, gt_ref, o_ref, s_ref, smax_ref, m_ref, acc_ref,
                 *, tq, group):
    w = 2 * DIFF_DH
    nq = vt_ref.shape[0]
    hslice = [slice(g * w, (g + 1) * w) for g in range(group)]
    _softmax_reset(m_ref, acc_ref)
    zeros = jnp.zeros((DIFF_DH, tq), BF16)

    def logits(g, qi, j, slot):
        qt = qt_ref[qi, hslice[g], :]
        q2 = jnp.concatenate([jnp.concatenate([qt[:DIFF_DH], zeros], axis=0),
                              jnp.concatenate([zeros, qt[DIFF_DH:]], axis=0)], axis=1)
        s = _dot(k_ref[_tile(j, tq), hslice[g]], q2)
        bias = tab_ref[g, jnp.minimum(qi - j, tab_ref.shape[1] - 1)]
        s = s + jnp.concatenate([bias, bias], axis=1)
        _store_logits(s, s_ref.at[slot, g], smax_ref.at[slot, g])

    def softmax(g, qi, j, slot):
        _softmax_update(s_ref.at[slot, g], smax_ref.at[slot, g], vt_ref[j, hslice[g], :],
                        m_ref.at[qi % PAIRS_PER_BLOCK, g], acc_ref.at[qi % PAIRS_PER_BLOCK, g])

    def finalize(qi):
        for g in range(group):
            o = _softmax_result(acc_ref.at[qi % PAIRS_PER_BLOCK, g])
            a = o[:, :tq] - lam_ref[0] * o[:, tq:]
            a = a * lax.rsqrt(jnp.mean(a * a, axis=0, keepdims=True) + EPS) * gt_ref[...]
            o_ref[_tile(qi, tq), hslice[g]] = a.T.astype(o_ref.dtype)
            _softmax_restart(m_ref.at[qi % PAIRS_PER_BLOCK, g])

    _causal_tile_pipeline(nq, group, logits, softmax, finalize)


def _diff_attention(proj, qvt, lam, tab, g_t, *, batch, seq, tq, group):
    t = proj.shape[0]
    nq = seq // tq
    w = group * 2 * DIFF_DH
    return pl.pallas_call(
        functools.partial(_diff_kernel, tq=tq, group=group),
        grid=(batch, DIFF_HEADS // group),
        in_specs=[pl.BlockSpec(memory_space=pltpu.SMEM),
                  pl.BlockSpec((nq, w, tq), lambda b, h: (b, h, 0)),
                  pl.BlockSpec((seq, w), lambda b, h: (b, COL_DK // w + h)),
                  pl.BlockSpec((nq, w, tq), lambda b, h: (b, DIFF_W // w + h, 0)),
                  pl.BlockSpec((group,) + tab.shape[1:], lambda b, h: (h, 0, 0, 0)),
                  _resident(g_t.shape)],
        out_specs=pl.BlockSpec((seq, w), lambda b, h: (b, h)),
        out_shape=jax.ShapeDtypeStruct((t, DIFF_W), BF16),
        scratch_shapes=_softmax_scratch(group, 2 * DIFF_DH, tq, 2 * tq),
        compiler_params=_cparams("parallel", "parallel"),
        name="diff_attention",
    )(lam, qvt, proj, qvt, tab, g_t)


def _mla_kernel(qt_ref, k_ref, vt_ref, mask_ref, o_ref, s_ref, smax_ref, m_ref, acc_ref, *, tq, group):
    hslice = [slice(g * LANES, (g + 1) * LANES) for g in range(group)]
    vslice = [slice(g * MLA_V, (g + 1) * MLA_V) for g in range(group)]
    _softmax_reset(m_ref, acc_ref)

    def logits(g, qi, j, slot):
        qt = jnp.concatenate([qt_ref[qi, hslice[g], :][:MLA_QH], jnp.zeros((LANES - MLA_QH, tq), BF16)], axis=0)
        s = _dot(k_ref[_tile(j, tq), hslice[g]], qt)
        s = s + mask_ref[jnp.minimum(qi - j, 1)]
        _store_logits(s, s_ref.at[slot, g], smax_ref.at[slot, g])

    def softmax(g, qi, j, slot):
        _softmax_update(s_ref.at[slot, g], smax_ref.at[slot, g], vt_ref[j, vslice[g], :],
                        m_ref.at[qi % PAIRS_PER_BLOCK, g], acc_ref.at[qi % PAIRS_PER_BLOCK, g])

    def finalize(qi):
        o = jnp.concatenate([_softmax_result(acc_ref.at[qi % PAIRS_PER_BLOCK, g]) for g in range(group)], axis=0)
        o_ref[_tile(qi, tq), :] = o.T.astype(o_ref.dtype)
        for g in range(group):
            _softmax_restart(m_ref.at[qi % PAIRS_PER_BLOCK, g])

    _causal_tile_pipeline(vt_ref.shape[0], group, logits, softmax, finalize)


def _mla_attention(qt, km, vt, mask, *, batch, seq, tq, group):
    t = km.shape[0]
    nq = seq // tq
    return pl.pallas_call(
        functools.partial(_mla_kernel, tq=tq, group=group),
        grid=(batch, MLA_HEADS // group),
        in_specs=[pl.BlockSpec((nq, group * LANES, tq), lambda b, h: (b, h, 0)),
                  pl.BlockSpec((seq, group * LANES), lambda b, h: (b, h)),
                  pl.BlockSpec((nq, group * MLA_V, tq), lambda b, h: (b, h, 0)),
                  _resident(mask.shape)],
        out_specs=pl.BlockSpec((seq, group * MLA_V), lambda b, h: (b, h)),
        out_shape=jax.ShapeDtypeStruct((t, MLA_V_W), BF16),
        scratch_shapes=_softmax_scratch(group, MLA_V, tq, tq),
        compiler_params=_cparams("parallel", "parallel"),
        name="mla_attention",
    )(qt, km, vt, mask)


def _merge_kernel(x_ref, a_ref, b_ref, c_ref, g_ref, wg_ref, wa_ref, wb_ref, wc_ref, wo_ref, pg_ref, o_ref):
    x = x_ref[...]
    d = x.shape[1]
    h = (_rms(x) * g_ref[...]).astype(BF16)
    merged = None
    for n, (br_ref, w_ref) in enumerate(((a_ref, wa_ref), (b_ref, wb_ref), (c_ref, wc_ref))):
        gate = jax.nn.sigmoid(_dot(h, wg_ref[:, n * d:(n + 1) * d]))
        part = gate * _dot(br_ref[...], w_ref[...])
        merged = part if merged is None else merged + part
    y = _dot(merged.astype(BF16), wo_ref[...])
    o_ref[...] = x + _rms(y) * pg_ref[...]


def _merge(x2, a, b, c, g, w_gates, wa, wb, wc, wo, layer, pg, *, tm):
    t, d = x2.shape
    rowblk = pl.BlockSpec((tm, d), lambda i: (i, 0))
    return pl.pallas_call(
        _merge_kernel,
        grid=(t // tm,),
        in_specs=[rowblk, rowblk, rowblk, rowblk, _resident((1, d))]
        + [_layer_slab(w, layer) for w in (w_gates, wa, wb, wc, wo)] + [_resident((1, d))],
        out_specs=rowblk,
        out_shape=jax.ShapeDtypeStruct((t, d), F32),
        compiler_params=_cparams("parallel"),
        name="merge",
    )(x2, a, b, c, g, w_gates, wa, wb, wc, wo, pg)


def _mlp_kernel(x_ref, g_ref, wu_ref, wd_ref, pg_ref, o_ref, *, tf):
    x = x_ref[...]
    h = (_rms(x) * g_ref[...]).astype(BF16)
    acc = None
    for c in range(wu_ref.shape[1] // tf):
        u = jnp.maximum(_dot(h, wu_ref[:, c * tf:(c + 1) * tf]), 0.0)
        part = _dot((u * u).astype(BF16), wd_ref[c * tf:(c + 1) * tf, :])
        acc = part if acc is None else acc + part
    o_ref[...] = x + _rms(acc) * pg_ref[...]


def _mlp(x2, g, wu, wd, layer, pg, *, tm, tf):
    t, d = x2.shape
    rowblk = pl.BlockSpec((tm, d), lambda i: (i, 0))
    return pl.pallas_call(
        functools.partial(_mlp_kernel, tf=tf),
        grid=(t // tm,),
        in_specs=[rowblk, _resident((1, d)), _layer_slab(wu, layer), _layer_slab(wd, layer), _resident((1, d))],
        out_specs=rowblk,
        out_shape=jax.ShapeDtypeStruct((t, d), F32),
        compiler_params=_cparams("parallel"),
        name="mlp",
    )(x2, g, wu, wd, pg)


def _rot_half_cols(w):
    half = w.shape[-1] // 2
    return jnp.concatenate([-w[..., half:], w[..., :half]], axis=-1)


def _cast_transposed_kernel(wt_ref, o_ref):
    o_ref[...] = wt_ref[...].T.astype(o_ref.dtype)


def _cast_transposed(wt, n, *, tn, skip=None):
    depth, _, d = wt.shape
    if skip is None:
        src = lambda c: c
    else:
        src = lambda c: jnp.where(c < skip[0] // tn, c, c + (skip[1] - skip[0]) // tn)
    return pl.pallas_call(
        _cast_transposed_kernel,
        grid=(depth, n // tn),
        in_specs=[pl.BlockSpec((None, tn, d), lambda l, c: (l, src(c), 0))],
        out_specs=pl.BlockSpec((None, d, tn), lambda l, c: (l, 0, c)),
        out_shape=jax.ShapeDtypeStruct((depth, d, n), BF16),
        compiler_params=_cparams("parallel", "parallel"),
        name="w_in_prep",
    )(wt)


def _prep_w_in(w_in, *, tn):
    wt = jnp.swapaxes(w_in, 1, 2)
    cq_at = W_IN_DV + DIFF_W
    kpe_at = cq_at + MLA_Q_LORA + MLA_KV_LORA
    gates_at = kpe_at + MLA_ROPE
    kpe = wt[:, kpe_at:gates_at]
    half = MLA_ROPE // 2
    zeros = jnp.zeros((wt.shape[0], MLA_NOPE, wt.shape[2]), wt.dtype)
    small = jnp.concatenate([wt[:, cq_at:kpe_at], zeros, kpe, -kpe[:, half:], kpe[:, :half]], axis=1)
    dqv = jnp.concatenate([wt[:, W_IN_DQ:W_IN_DK], wt[:, W_IN_DV:cq_at]], axis=1)
    return (_cast_transposed(wt, N_IN_MAIN, tn=tn, skip=(W_IN_DQ, W_IN_DK)),
            dqv.astype(BF16),
            _cast_transposed(wt[:, gates_at:], N_BRANCH * D_MODEL, tn=tn),
            _cast_transposed(small, N_IN_SMALL, tn=tn))


def _in_colscale():
    cs = np.ones((1, N_IN_MAIN), np.float32)
    cs[0, COL_RK:COL_RK + RET_QK_W] = RET_DK ** -0.5
    return jnp.asarray(cs)


def _prep_w_uq(w):
    w = w.reshape(MLA_Q_LORA, MLA_HEADS, MLA_QH)
    pe = w[..., MLA_NOPE:]
    w = jnp.concatenate([w, _rot_half_cols(pe)], axis=-1).reshape(MLA_Q_LORA, MLA_PAD_W)
    return w.T.astype(BF16)


def _prep_w_ukv(w):
    w = w.reshape(MLA_KV_LORA, MLA_HEADS, MLA_KVH)
    wk = jnp.concatenate([w[..., :MLA_NOPE], jnp.zeros_like(w[..., :MLA_NOPE])], axis=-1)
    return (wk.reshape(MLA_KV_LORA, MLA_PAD_W).astype(BF16),
            w[..., MLA_NOPE:].reshape(MLA_KV_LORA, MLA_V_W).T.astype(BF16))


def _ret_tables(seq):
    pos = np.arange(seq, dtype=np.float64)
    inv_freq = 1.0 / (10000.0 ** np.linspace(0.0, 1.0, RET_DK // 2))
    ang = pos[:, None] * inv_freq[None, :]
    cos, sin = np.cos(ang), np.sin(ang)
    cos_t = np.concatenate([cos, cos], axis=-1)
    sin_t = np.concatenate([-sin, sin], axis=-1)
    log_g = np.log1p(-np.exp2(-5.0 - np.arange(RET_HEADS, dtype=np.float64)))
    idx = np.arange(RET_CHUNK, dtype=np.float64)
    rel = idx[:, None] - idx[None, :]
    dmask = np.where(rel >= 0, np.exp(np.maximum(rel, 0.0)[None] * log_g[:, None, None]), 0.0)
    xi = np.exp((idx + 1.0)[None, :] * log_g[:, None])[:, :, None]
    zeta = np.exp((RET_CHUNK - 1.0 - idx)[None, :] * log_g[:, None])[:, :, None]
    g_chunk = np.exp(RET_CHUNK * log_g)[:, None, None]
    tabs = (cos_t, sin_t, dmask,
            np.broadcast_to(xi, (RET_HEADS, RET_CHUNK, RET_DK)),
            np.broadcast_to(zeta, (RET_HEADS, RET_CHUNK, RET_DK)),
            np.broadcast_to(g_chunk, (RET_HEADS, 1, RET_DV)))
    return tuple(jnp.asarray(t, F32) for t in tabs)


def _mla_tables(seq):
    pos = np.arange(seq, dtype=np.float64)
    inv_freq = 1.0 / (ROPE_THETA ** (np.arange(0, MLA_ROPE, 2, dtype=np.float64) / MLA_ROPE))
    ang = pos[:, None] * inv_freq[None, :]
    cos, sin = np.cos(ang), np.sin(ang)
    z_nope = np.zeros((seq, MLA_NOPE))
    z_rope = np.zeros((seq, MLA_ROPE))
    cos_k = np.concatenate([z_nope, cos, cos, z_rope], axis=-1)
    sin_k = np.concatenate([z_nope, sin, sin, z_rope], axis=-1)
    qscale = MLA_QH ** -0.5 * LOG2E
    cos_q = np.concatenate([np.ones((seq, MLA_NOPE)), cos, cos, z_rope], axis=-1) * qscale
    sin_q = sin_k * qscale
    return tuple(jnp.asarray(t, F32) for t in (cos_q.T, sin_q.T, cos_k, sin_k))


def _t5_bucket(n):
    max_exact = REL_BUCKETS // 2
    nf = jnp.maximum(n, 1).astype(F32)
    large = max_exact + (jnp.log(nf / max_exact) / math.log(REL_MAX_DIST / max_exact)
                         * (REL_BUCKETS - max_exact)).astype(jnp.int32)
    large = jnp.minimum(large, REL_BUCKETS - 1)
    return jnp.where(n < max_exact, n, large)


def _toeplitz(g, tq):
    h = g.shape[0]
    padded = jnp.concatenate([g, jnp.zeros((h, 1), g.dtype)], axis=1)
    skewed = jnp.tile(padded, (1, tq))[:, :tq * (2 * tq - 1)].reshape(h, tq, 2 * tq - 1)
    return skewed[:, :, tq - 1:]


def _bias_table(rel_bias, tq):
    assert tq >= REL_MAX_DIST
    bias = rel_bias[_t5_bucket(jnp.arange(2 * tq))].astype(F32).T
    bias = (bias - rel_bias[REL_BUCKETS - 1].astype(F32)[:, None]) * LOG2E
    masked = jnp.full((bias.shape[0], tq - 1), NEG_BIG, F32)
    diagonal = jnp.concatenate([masked, bias[:, :tq]], axis=1)
    behind = bias[:, 1:]
    zeros = jnp.zeros((bias.shape[0], tq, tq), F32)
    return jnp.stack([_toeplitz(diagonal, tq), _toeplitz(behind, tq), zeros], axis=1)


def _causal_table(tq):
    kk = np.arange(tq)[:, None]
    qq = np.arange(tq)[None, :]
    return jnp.asarray(np.stack([np.where(qq >= kk, 0.0, NEG_BIG), np.zeros((tq, tq))]), F32)


def _pick(t, want):
    return want if t % want == 0 else t


class _Tiles(NamedTuple):
    attn: int
    in_rows: int
    in_cols: int
    merge_rows: int
    mlp_rows: int
    mlp_cols: int
    diff_group: int
    mla_group: int


def _plan_tiles(batch, seq):
    t = batch * seq
    return _Tiles(attn=_pick(seq, 256), in_rows=_pick(seq, 512), in_cols=512, merge_rows=_pick(t, 512),
                  mlp_rows=_pick(t, 1024), mlp_cols=1024, diff_group=4, mla_group=8)


def kernel(x, rel_bias, pre_mix_g, w_in, w_ret_o, lambda_q1, lambda_k1, lambda_q2, lambda_k2, diff_subln_g, w_diff_o, mla_q_norm_g, w_mla_uq, mla_kv_norm_g, w_mla_ukv, w_mla_o, w_out, post_mix_g, pre_mlp_g, w_up, w_down, post_mlp_g):
    batch, seq, d = x.shape
    t = batch * seq
    depth = w_in.shape[0]
    tiles = _plan_tiles(batch, seq)
    tq = tiles.attn
    assert d == D_MODEL and w_in.shape[1:] == (D_MODEL, W_IN_DV + DIFF_W + N_IN_SMALL - LANES + MLA_ROPE
                                               + N_BRANCH * D_MODEL)
    assert seq % RET_CHUNK == 0 and seq % tq == 0
    assert DIFF_HEADS % tiles.diff_group == 0 and MLA_HEADS % tiles.mla_group == 0
    ret_tabs = _ret_tables(seq)
    mla_tabs = _mla_tables(seq)
    bias_tab = _bias_table(rel_bias, tq)
    causal_tab = _causal_table(tq)
    colscale = _in_colscale()
    w_main, w_dvt, w_gates, w_small = _prep_w_in(w_in, tn=tiles.in_cols)
    merge_w = tuple(w.astype(BF16) for w in (w_ret_o, w_diff_o, w_mla_o, w_out))
    mlp_w = (w_up.astype(BF16), w_down.astype(BF16))
    row = lambda v: v.reshape(1, -1).astype(F32)

    x2 = x.reshape(t, d)
    for l in range(depth):
        wk, wvt = _prep_w_ukv(w_mla_ukv[l])
        proj, dvt, qt, km, vt = _in_proj(x2, row(pre_mix_g[l]), w_main, w_dvt, w_small, l, colscale,
                                         row(mla_q_norm_g[l]), row(mla_kv_norm_g[l]),
                                         _prep_w_uq(w_mla_uq[l]), wk, wvt, mla_tabs,
                                         seq=seq, tm=tiles.in_rows, tn=tiles.in_cols, tk=tq)

        ret = _retention(proj, ret_tabs, batch=batch, seq=seq)

        lambda_init = 0.8 - 0.6 * math.exp(-0.3 * l)
        lam = (jnp.exp(jnp.sum(lambda_q1[l] * lambda_k1[l]).astype(F32))
               - jnp.exp(jnp.sum(lambda_q2[l] * lambda_k2[l]).astype(F32)) + lambda_init)
        subln_g = jnp.broadcast_to((diff_subln_g[l].astype(F32) * (1.0 - lambda_init))[:, None], (2 * DIFF_DH, tq))
        da = _diff_attention(proj, dvt, lam.reshape(1), bias_tab, subln_g,
                             batch=batch, seq=seq, tq=tq, group=tiles.diff_group)

        mo = _mla_attention(qt, km, vt, causal_tab, batch=batch, seq=seq, tq=tq, group=tiles.mla_group)

        x2 = _merge(x2, ret, da, mo, row(pre_mix_g[l]), w_gates, *merge_w, l, row(post_mix_g[l]),
                    tm=tiles.merge_rows)

        x2 = _mlp(x2, row(pre_mlp_g[l]), *mlp_w, l, row(post_mlp_g[l]), tm=tiles.mlp_rows, tf=tiles.mlp_cols)
    return x2.reshape(batch, seq, d)
```

```python
import functools
import math
from typing import NamedTuple

import jax
import jax.numpy as jnp
import numpy as np
from jax import lax
from jax.experimental import pallas as pl
from jax.experimental.pallas import tpu as pltpu

F32 = jnp.float32
BF16 = jnp.bfloat16

D_MODEL = 1024
EPS = 1e-6
RET_HEADS = 4
RET_DK = 128
RET_DV = 256
RET_CHUNK = 128
DIFF_HEADS = 8
DIFF_DH = 64
MLA_HEADS = 16
MLA_Q_LORA = 256
MLA_KV_LORA = 128
MLA_NOPE = 64
MLA_ROPE = 32
MLA_V = 64
ROPE_THETA = 10000.0
REL_BUCKETS = 32
REL_MAX_DIST = 128
N_BRANCH = 3

LANES = 128
V7X_VMEM_BYTES = 64 * 1024 * 1024
VMEM_LIMIT = V7X_VMEM_BYTES * 7 // 8
LOG2E = math.log2(math.e)
NEG_BIG = -1e30

RET_QK_W = RET_HEADS * RET_DK
RET_V_W = RET_HEADS * RET_DV
DIFF_W = DIFF_HEADS * 2 * DIFF_DH
MLA_QH = MLA_NOPE + MLA_ROPE
MLA_KVH = MLA_NOPE + MLA_V
MLA_V_W = MLA_HEADS * MLA_V
MLA_PAD_W = MLA_HEADS * LANES

COL_RQ = 0
COL_RK = COL_RQ + RET_QK_W
COL_RV = COL_RK + RET_QK_W
COL_RG = COL_RV + RET_V_W
COL_DK = COL_RG + RET_V_W
N_IN_MAIN = COL_DK + DIFF_W
W_IN_DQ = COL_DK
W_IN_DK = W_IN_DQ + DIFF_W
W_IN_DV = W_IN_DK + DIFF_W
DIFF_Q_SCALE = DIFF_DH ** -0.5 * LOG2E
N_IN_SMALL = MLA_Q_LORA + MLA_KV_LORA + LANES


def _cparams(*sem):
    return pltpu.CompilerParams(dimension_semantics=sem, vmem_limit_bytes=VMEM_LIMIT)


def _rms(xf):
    return xf * lax.rsqrt(jnp.mean(xf * xf, axis=-1, keepdims=True) + EPS)


def _dot(a, b):
    return jnp.dot(a, b, preferred_element_type=F32)


def _dot_nt(a, b):
    return lax.dot_general(a, b, (((1,), (1,)), ((), ())), preferred_element_type=F32)


HEADS_PER_UP = 4


def _in_proj_kernel(x_ref, g_ref, wm_ref, wdvt_ref, ws_ref, cs_ref, gq_ref, gkv_ref, wqt_ref, wk_ref, wvt_ref,
                    cqt_tab, sqt_tab, ck_tab, sk_tab, o_ref, dvt_ref, qt_ref, k_ref, vt_ref, *, tn):
    h = (_rms(x_ref[...]) * g_ref[...]).astype(BF16)

    small = _dot(h, ws_ref[...])
    hq = (_rms(small[:, :MLA_Q_LORA]) * gq_ref[...]).astype(BF16)
    hkv = (_rms(small[:, MLA_Q_LORA:MLA_Q_LORA + MLA_KV_LORA]) * gkv_ref[...]).astype(BF16)
    kpe = small[:, MLA_Q_LORA + MLA_KV_LORA:]
    kpe = kpe * ck_tab[...] + pltpu.roll(kpe, LANES - MLA_ROPE, 1) * sk_tab[...]
    tk = vt_ref.shape[-1]
    n_tiles = vt_ref.shape[0]
    for c in range(n_tiles):
        vt_ref[c] = _dot_nt(wvt_ref[...], hkv[c * tk:(c + 1) * tk]).astype(vt_ref.dtype)
    cqt = cqt_tab[...]
    sqt = sqt_tab[...]
    up = HEADS_PER_UP * LANES
    for hg in range(MLA_HEADS // HEADS_PER_UP):
        q_t = _dot_nt(wqt_ref[hg * up:(hg + 1) * up, :], hq)
        ka = _dot(hkv, wk_ref[:, hg * up:(hg + 1) * up])
        for hh in range(HEADS_PER_UP):
            hs = slice(hh * LANES, (hh + 1) * LANES)
            out = slice(hg * up + hh * LANES, hg * up + (hh + 1) * LANES)
            blk = q_t[hs]
            partner = jnp.concatenate([blk[MLA_ROPE:], blk[:MLA_ROPE]], axis=0)
            blk = blk * cqt + partner * sqt
            for c in range(n_tiles):
                qt_ref[c, out, :] = blk[:, c * tk:(c + 1) * tk].astype(qt_ref.dtype)
            k_ref[:, out] = (ka[:, hs] + kpe).astype(k_ref.dtype)

    for c in range(wm_ref.shape[1] // tn):
        cols = slice(c * tn, (c + 1) * tn)
        o_ref[:, cols] = (_dot(h, wm_ref[:, cols]) * cs_ref[:, cols]).astype(o_ref.dtype)
    for r in range(wdvt_ref.shape[0] // tn):
        rows = slice(r * tn, (r + 1) * tn)
        dqv_t = _dot_nt(wdvt_ref[rows, :], h)
        if r * tn < DIFF_W:
            dqv_t = dqv_t * DIFF_Q_SCALE
        for c in range(n_tiles):
            dvt_ref[c, rows, :] = dqv_t[:, c * tk:(c + 1) * tk].astype(dvt_ref.dtype)


def _resident(shape):
    return pl.BlockSpec(shape, lambda *_: (0,) * len(shape), pipeline_mode=pl.Buffered(1))


def _layer_slab(w, layer):
    return pl.BlockSpec((None,) + w.shape[1:], lambda *_: (layer, 0, 0), pipeline_mode=pl.Buffered(1))


def _in_proj(x2, g, w_main, w_dvt, w_small, layer, colscale, gq, gkv, wqt, wk, wvt, tabs, *, seq, tm, tn, tk):
    t, d = x2.shape
    n = w_main.shape[2]
    assert n % tn == 0 and w_dvt.shape[1] % tn == 0 and tm % tk == 0 and seq % tm == 0
    ns = seq // tm
    nc = tm // tk
    cqt, sqt, ck, sk = tabs
    tab_t = pl.BlockSpec((LANES, tm), lambda i: (0, i % ns))
    tab = pl.BlockSpec((tm, LANES), lambda i: (i % ns, 0))
    return pl.pallas_call(
        functools.partial(_in_proj_kernel, tn=tn),
        grid=(t // tm,),
        in_specs=[pl.BlockSpec((tm, d), lambda i: (i, 0)), _resident((1, d)),
                  _layer_slab(w_main, layer), _layer_slab(w_dvt, layer), _layer_slab(w_small, layer),
                  _resident((1, n)), _resident(gq.shape), _resident(gkv.shape),
                  _resident(wqt.shape), _resident(wk.shape), _resident(wvt.shape),
                  tab_t, tab_t, tab, tab],
        out_specs=[pl.BlockSpec((tm, n), lambda i: (i, 0)),
                   pl.BlockSpec((nc, w_dvt.shape[1], tk), lambda i: (i, 0, 0)),
                   pl.BlockSpec((nc, MLA_PAD_W, tk), lambda i: (i, 0, 0)),
                   pl.BlockSpec((tm, MLA_PAD_W), lambda i: (i, 0)),
                   pl.BlockSpec((nc, MLA_V_W, tk), lambda i: (i, 0, 0))],
        out_shape=[jax.ShapeDtypeStruct((t, n), BF16),
                   jax.ShapeDtypeStruct((t // tk, w_dvt.shape[1], tk), BF16),
                   jax.ShapeDtypeStruct((t // tk, MLA_PAD_W, tk), BF16),
                   jax.ShapeDtypeStruct((t, MLA_PAD_W), BF16),
                   jax.ShapeDtypeStruct((t // tk, MLA_V_W, tk), BF16)],
        compiler_params=_cparams("parallel"),
        name="in_proj",
    )(x2, g, w_main, w_dvt, w_small, colscale, gq, gkv, wqt, wk, wvt, cqt, sqt, ck, sk)


def _ret_kernel(q_ref, k_ref, v_ref, rg_ref, cos_ref, sin_ref, dm_ref, xi_ref, zt_ref, gc_ref,
                o_ref, st_ref):
    st_ref[...] = jnp.zeros_like(st_ref)

    def chunk(n, carry):
        rows = _tile(n, RET_CHUNK)
        cos = cos_ref[rows, :]
        sin = sin_ref[rows, :]
        for h in range(RET_HEADS):
            qs = slice(h * RET_DK, (h + 1) * RET_DK)
            vs = slice(h * RET_DV, (h + 1) * RET_DV)
            q = q_ref[rows, qs].astype(F32)
            k = k_ref[rows, qs].astype(F32)
            qr = q * cos + pltpu.roll(q, RET_DK // 2, 1) * sin
            kr = k * cos + pltpu.roll(k, RET_DK // 2, 1) * sin
            v = v_ref[rows, vs]
            st = st_ref[h]
            inner = _dot_nt(qr.astype(BF16), kr.astype(BF16)) * dm_ref[h]
            out = _dot(inner.astype(BF16), v) + _dot((qr * xi_ref[h]).astype(BF16), st.astype(BF16))
            kz_t = (kr * zt_ref[h]).T.astype(BF16)
            st_ref[h] = gc_ref[h] * st + _dot(kz_t, v)
            rg = rg_ref[rows, vs].astype(F32)
            o_ref[rows, vs] = (_rms(out) * (rg * jax.nn.sigmoid(rg))).astype(o_ref.dtype)
        return carry

    lax.fori_loop(0, q_ref.shape[0] // RET_CHUNK, chunk, 0, unroll=4)


def _retention(proj, tabs, *, batch, seq):
    t = proj.shape[0]
    c = RET_CHUNK
    cos, sin, dmask, xi, zeta, gch = tabs
    return pl.pallas_call(
        _ret_kernel,
        grid=(batch,),
        in_specs=[pl.BlockSpec((seq, RET_QK_W), lambda b: (b, COL_RQ // RET_QK_W)),
                  pl.BlockSpec((seq, RET_QK_W), lambda b: (b, COL_RK // RET_QK_W)),
                  pl.BlockSpec((seq, RET_V_W), lambda b: (b, COL_RV // RET_V_W)),
                  pl.BlockSpec((seq, RET_V_W), lambda b: (b, COL_RG // RET_V_W)),
                  _resident((seq, RET_DK)), _resident((seq, RET_DK)),
                  _resident((RET_HEADS, c, c)), _resident((RET_HEADS, c, RET_DK)),
                  _resident((RET_HEADS, c, RET_DK)), _resident((RET_HEADS, 1, RET_DV))],
        out_specs=pl.BlockSpec((seq, RET_V_W), lambda b: (b, 0)),
        out_shape=jax.ShapeDtypeStruct((t, RET_V_W), BF16),
        scratch_shapes=[pltpu.VMEM((RET_HEADS, RET_DK, RET_DV), F32)],
        compiler_params=_cparams("parallel"),
        name="retention",
    )(proj, proj, proj, proj, cos, sin, dmask, xi, zeta, gch)


SUM_ROWS = 16
PAIRS_PER_BLOCK = 12


def _state_ring(nq):
    pairs = [(qi, j) for qi in range(nq) for j in range(qi + 1)]
    return max(len({qi for qi, _ in pairs[i:i + PAIRS_PER_BLOCK]}) for i in range(0, len(pairs), PAIRS_PER_BLOCK))


def _store_logits(s, s_ref, smax_ref):
    s_ref[...] = s
    smax_ref[...] = jnp.max(s, axis=0, keepdims=True)


def _softmax_update(s_ref, smax_ref, vt, m_ref, acc_ref):
    m = m_ref[...]
    m_new = jnp.maximum(m, smax_ref[...])
    alpha = jnp.exp2(m - m_new)
    p = jnp.exp2(s_ref[...] - m_new).astype(BF16)
    m_ref[...] = m_new
    vt_ones = jnp.concatenate([vt, jnp.ones((SUM_ROWS, vt.shape[1]), BF16)], axis=0)
    acc_ref[...] = alpha * acc_ref[...] + _dot(vt_ones, p)


def _softmax_result(acc_ref):
    dv = acc_ref.shape[0] - SUM_ROWS
    return acc_ref[:dv, :] / acc_ref[dv:dv + 1, :]


def _softmax_reset(m_ref, acc_ref):
    m_ref[...] = jnp.full(m_ref.shape, NEG_BIG, F32)
    acc_ref[...] = jnp.zeros(acc_ref.shape, F32)


def _softmax_restart(m_ref):
    m_ref[...] = jnp.full(m_ref.shape, NEG_BIG, F32)


def _softmax_scratch(group, dv, tk, n, ring):
    return [pltpu.VMEM((2, group, tk, n), F32), pltpu.VMEM((2, group, 1, n), F32),
            pltpu.VMEM((ring, group, 1, n), F32), pltpu.VMEM((ring, group, dv + SUM_ROWS, n), F32)]


def _tile(i, size):
    return pl.ds(pl.multiple_of(i * size, size), size)


def _causal_tile_pipeline(nq, group, logits, softmax, finalize):
    def following(pair):
        qi, j = pair
        last = j == qi
        return jnp.where(last, qi + 1, qi), jnp.where(last, 0, j + 1)

    def run(first, count, prefetch_next):
        pairs = [first]
        for _ in range(count):
            pairs.append(following(pairs[-1]))
        for i in range(count):
            if i + 1 < count or prefetch_next:
                qn, jn = pairs[i + 1]
                qn = jnp.minimum(qn, nq - 1)
                for g in range(group):
                    logits(g, qn, jn, (i + 1) % 2)
            for g in range(group):
                softmax(g, *pairs[i], i % 2)
        for qi, j in pairs[:count]:
            @pl.when(j == qi)
            def _(qi=qi):
                finalize(qi)
        return pairs[count]

    n_pairs = nq * (nq + 1) // 2
    zero = jnp.int32(0)
    for g in range(group):
        logits(g, zero, zero, 0)
    rest = lax.fori_loop(0, n_pairs // PAIRS_PER_BLOCK, lambda _, c: run(c, PAIRS_PER_BLOCK, True), (zero, zero))
    if n_pairs % PAIRS_PER_BLOCK:
        run(rest, n_pairs % PAIRS_PER_BLOCK, False)


def _diff_kernel(lam_ref, qt_ref, k_ref, vt_ref, tab_ref, gt_ref, o_ref, s_ref, smax_ref, m_ref, acc_ref,
                 *, tq, group):
    w = 2 * DIFF_DH
    nq = vt_ref.shape[0]
    hslice = [slice(g * w, (g + 1) * w) for g in range(group)]
    _softmax_reset(m_ref, acc_ref)
    ring = m_ref.shape[0]
    zeros = jnp.zeros((DIFF_DH, tq), BF16)

    def logits(g, qi, j, slot):
        qt = qt_ref[qi, hslice[g], :]
        q2 = jnp.concatenate([jnp.concatenate([qt[:DIFF_DH], zeros], axis=0),
                              jnp.concatenate([zeros, qt[DIFF_DH:]], axis=0)], axis=1)
        s = _dot(k_ref[_tile(j, tq), hslice[g]], q2)
        bias = tab_ref[g, jnp.minimum(qi - j, tab_ref.shape[1] - 1)]
        s = s + jnp.concatenate([bias, bias], axis=1)
        _store_logits(s, s_ref.at[slot, g], smax_ref.at[slot, g])

    def softmax(g, qi, j, slot):
        _softmax_update(s_ref.at[slot, g], smax_ref.at[slot, g], vt_ref[j, hslice[g], :],
                        m_ref.at[qi % ring, g], acc_ref.at[qi % ring, g])

    def finalize(qi):
        for g in range(group):
            o = _softmax_result(acc_ref.at[qi % ring, g])
            a = o[:, :tq] - lam_ref[0] * o[:, tq:]
            a = a * lax.rsqrt(jnp.mean(a * a, axis=0, keepdims=True) + EPS) * gt_ref[...]
            o_ref[_tile(qi, tq), hslice[g]] = a.T.astype(o_ref.dtype)
            _softmax_restart(m_ref.at[qi % ring, g])

    _causal_tile_pipeline(nq, group, logits, softmax, finalize)


def _diff_attention(proj, qvt, lam, tab, g_t, *, batch, seq, tq, group):
    t = proj.shape[0]
    nq = seq // tq
    w = group * 2 * DIFF_DH
    return pl.pallas_call(
        functools.partial(_diff_kernel, tq=tq, group=group),
        grid=(batch, DIFF_HEADS // group),
        in_specs=[pl.BlockSpec(memory_space=pltpu.SMEM),
                  pl.BlockSpec((nq, w, tq), lambda b, h: (b, h, 0)),
                  pl.BlockSpec((seq, w), lambda b, h: (b, COL_DK // w + h)),
                  pl.BlockSpec((nq, w, tq), lambda b, h: (b, DIFF_W // w + h, 0)),
                  pl.BlockSpec((group,) + tab.shape[1:], lambda b, h: (h, 0, 0, 0)),
                  _resident(g_t.shape)],
        out_specs=pl.BlockSpec((seq, w), lambda b, h: (b, h)),
        out_shape=jax.ShapeDtypeStruct((t, DIFF_W), BF16),
        scratch_shapes=_softmax_scratch(group, 2 * DIFF_DH, tq, 2 * tq, _state_ring(nq)),
        compiler_params=_cparams("parallel", "parallel"),
        name="diff_attention",
    )(lam, qvt, proj, qvt, tab, g_t)


def _mla_kernel(qt_ref, k_ref, vt_ref, mask_ref, o_ref, s_ref, smax_ref, m_ref, acc_ref, *, tq, group):
    hslice = [slice(g * LANES, (g + 1) * LANES) for g in range(group)]
    vslice = [slice(g * MLA_V, (g + 1) * MLA_V) for g in range(group)]
    _softmax_reset(m_ref, acc_ref)
    ring = m_ref.shape[0]

    def logits(g, qi, j, slot):
        qt = jnp.concatenate([qt_ref[qi, hslice[g], :][:MLA_QH], jnp.zeros((LANES - MLA_QH, tq), BF16)], axis=0)
        s = _dot(k_ref[_tile(j, tq), hslice[g]], qt)
        s = s + mask_ref[jnp.minimum(qi - j, 1)]
        _store_logits(s, s_ref.at[slot, g], smax_ref.at[slot, g])

    def softmax(g, qi, j, slot):
        _softmax_update(s_ref.at[slot, g], smax_ref.at[slot, g], vt_ref[j, vslice[g], :],
                        m_ref.at[qi % ring, g], acc_ref.at[qi % ring, g])

    def finalize(qi):
        o = jnp.concatenate([_softmax_result(acc_ref.at[qi % ring, g]) for g in range(group)], axis=0)
        o_ref[_tile(qi, tq), :] = o.T.astype(o_ref.dtype)
        for g in range(group):
            _softmax_restart(m_ref.at[qi % ring, g])

    _causal_tile_pipeline(vt_ref.shape[0], group, logits, softmax, finalize)


def _mla_attention(qt, km, vt, mask, *, batch, seq, tq, group):
    t = km.shape[0]
    nq = seq // tq
    return pl.pallas_call(
        functools.partial(_mla_kernel, tq=tq, group=group),
        grid=(batch, MLA_HEADS // group),
        in_specs=[pl.BlockSpec((nq, group * LANES, tq), lambda b, h: (b, h, 0)),
                  pl.BlockSpec((seq, group * LANES), lambda b, h: (b, h)),
                  pl.BlockSpec((nq, group * MLA_V, tq), lambda b, h: (b, h, 0)),
                  _resident(mask.shape)],
        out_specs=pl.BlockSpec((seq, group * MLA_V), lambda b, h: (b, h)),
        out_shape=jax.ShapeDtypeStruct((t, MLA_V_W), BF16),
        scratch_shapes=_softmax_scratch(group, MLA_V, tq, tq, _state_ring(nq)),
        compiler_params=_cparams("parallel", "parallel"),
        name="mla_attention",
    )(qt, km, vt, mask)


def _merge_kernel(x_ref, a_ref, b_ref, c_ref, g_ref, wg_ref, wa_ref, wb_ref, wc_ref, wo_ref, pg_ref, o_ref):
    x = x_ref[...]
    d = x.shape[1]
    h = (_rms(x) * g_ref[...]).astype(BF16)
    merged = None
    for n, (br_ref, w_ref) in enumerate(((a_ref, wa_ref), (b_ref, wb_ref), (c_ref, wc_ref))):
        gate = jax.nn.sigmoid(_dot(h, wg_ref[:, n * d:(n + 1) * d]))
        part = gate * _dot(br_ref[...], w_ref[...])
        merged = part if merged is None else merged + part
    y = _dot(merged.astype(BF16), wo_ref[...])
    o_ref[...] = x + _rms(y) * pg_ref[...]


def _merge(x2, a, b, c, g, w_gates, wa, wb, wc, wo, layer, pg, *, tm):
    t, d = x2.shape
    rowblk = pl.BlockSpec((tm, d), lambda i: (i, 0))
    return pl.pallas_call(
        _merge_kernel,
        grid=(t // tm,),
        in_specs=[rowblk, rowblk, rowblk, rowblk, _resident((1, d))]
        + [_layer_slab(w, layer) for w in (w_gates, wa, wb, wc, wo)] + [_resident((1, d))],
        out_specs=rowblk,
        out_shape=jax.ShapeDtypeStruct((t, d), F32),
        compiler_params=_cparams("parallel"),
        name="merge",
    )(x2, a, b, c, g, w_gates, wa, wb, wc, wo, pg)


def _mlp_kernel(x_ref, g_ref, wu_ref, wd_ref, pg_ref, o_ref, *, tf):
    x = x_ref[...]
    h = (_rms(x) * g_ref[...]).astype(BF16)
    acc = None
    for c in range(wu_ref.shape[1] // tf):
        u = jnp.maximum(_dot(h, wu_ref[:, c * tf:(c + 1) * tf]), 0.0)
        part = _dot((u * u).astype(BF16), wd_ref[c * tf:(c + 1) * tf, :])
        acc = part if acc is None else acc + part
    o_ref[...] = x + _rms(acc) * pg_ref[...]


def _mlp(x2, g, wu, wd, layer, pg, *, tm, tf):
    t, d = x2.shape
    rowblk = pl.BlockSpec((tm, d), lambda i: (i, 0))
    return pl.pallas_call(
        functools.partial(_mlp_kernel, tf=tf),
        grid=(t // tm,),
        in_specs=[rowblk, _resident((1, d)), _layer_slab(wu, layer), _layer_slab(wd, layer), _resident((1, d))],
        out_specs=rowblk,
        out_shape=jax.ShapeDtypeStruct((t, d), F32),
        compiler_params=_cparams("parallel"),
        name="mlp",
    )(x2, g, wu, wd, pg)


def _rot_half_cols(w):
    half = w.shape[-1] // 2
    return jnp.concatenate([-w[..., half:], w[..., :half]], axis=-1)


def _cast_transposed_kernel(wt_ref, o_ref):
    o_ref[...] = wt_ref[...].T.astype(o_ref.dtype)


def _cast_transposed(wt, n, *, tn, skip=None):
    depth, _, d = wt.shape
    if skip is None:
        src = lambda c: c
    else:
        src = lambda c: jnp.where(c < skip[0] // tn, c, c + (skip[1] - skip[0]) // tn)
    return pl.pallas_call(
        _cast_transposed_kernel,
        grid=(depth, n // tn),
        in_specs=[pl.BlockSpec((None, tn, d), lambda l, c: (l, src(c), 0))],
        out_specs=pl.BlockSpec((None, d, tn), lambda l, c: (l, 0, c)),
        out_shape=jax.ShapeDtypeStruct((depth, d, n), BF16),
        compiler_params=_cparams("parallel", "parallel"),
        name="w_in_prep",
    )(wt)


def _prep_w_in(w_in, *, tn):
    wt = jnp.swapaxes(w_in, 1, 2)
    cq_at = W_IN_DV + DIFF_W
    kpe_at = cq_at + MLA_Q_LORA + MLA_KV_LORA
    gates_at = kpe_at + MLA_ROPE
    kpe = wt[:, kpe_at:gates_at]
    half = MLA_ROPE // 2
    zeros = jnp.zeros((wt.shape[0], MLA_NOPE, wt.shape[2]), wt.dtype)
    small = jnp.concatenate([wt[:, cq_at:kpe_at], zeros, kpe, -kpe[:, half:], kpe[:, :half]], axis=1)
    dqv = jnp.concatenate([wt[:, W_IN_DQ:W_IN_DK], wt[:, W_IN_DV:cq_at]], axis=1)
    return (_cast_transposed(wt, N_IN_MAIN, tn=tn, skip=(W_IN_DQ, W_IN_DK)),
            dqv.astype(BF16),
            _cast_transposed(wt[:, gates_at:], N_BRANCH * D_MODEL, tn=tn),
            _cast_transposed(small, N_IN_SMALL, tn=tn))


def _in_colscale():
    cs = np.ones((1, N_IN_MAIN), np.float32)
    cs[0, COL_RK:COL_RK + RET_QK_W] = RET_DK ** -0.5
    return jnp.asarray(cs)


def _prep_w_uq(w):
    w = w.reshape(MLA_Q_LORA, MLA_HEADS, MLA_QH)
    pe = w[..., MLA_NOPE:]
    w = jnp.concatenate([w, _rot_half_cols(pe)], axis=-1).reshape(MLA_Q_LORA, MLA_PAD_W)
    return w.T.astype(BF16)


def _prep_w_ukv(w):
    w = w.reshape(MLA_KV_LORA, MLA_HEADS, MLA_KVH)
    wk = jnp.concatenate([w[..., :MLA_NOPE], jnp.zeros_like(w[..., :MLA_NOPE])], axis=-1)
    return (wk.reshape(MLA_KV_LORA, MLA_PAD_W).astype(BF16),
            w[..., MLA_NOPE:].reshape(MLA_KV_LORA, MLA_V_W).T.astype(BF16))


def _ret_tables(seq):
    pos = np.arange(seq, dtype=np.float64)
    inv_freq = 1.0 / (10000.0 ** np.linspace(0.0, 1.0, RET_DK // 2))
    ang = pos[:, None] * inv_freq[None, :]
    cos, sin = np.cos(ang), np.sin(ang)
    cos_t = np.concatenate([cos, cos], axis=-1)
    sin_t = np.concatenate([-sin, sin], axis=-1)
    log_g = np.log1p(-np.exp2(-5.0 - np.arange(RET_HEADS, dtype=np.float64)))
    idx = np.arange(RET_CHUNK, dtype=np.float64)
    rel = idx[:, None] - idx[None, :]
    dmask = np.where(rel >= 0, np.exp(np.maximum(rel, 0.0)[None] * log_g[:, None, None]), 0.0)
    xi = np.exp((idx + 1.0)[None, :] * log_g[:, None])[:, :, None]
    zeta = np.exp((RET_CHUNK - 1.0 - idx)[None, :] * log_g[:, None])[:, :, None]
    g_chunk = np.exp(RET_CHUNK * log_g)[:, None, None]
    tabs = (cos_t, sin_t, dmask,
            np.broadcast_to(xi, (RET_HEADS, RET_CHUNK, RET_DK)),
            np.broadcast_to(zeta, (RET_HEADS, RET_CHUNK, RET_DK)),
            np.broadcast_to(g_chunk, (RET_HEADS, 1, RET_DV)))
    return tuple(jnp.asarray(t, F32) for t in tabs)


def _mla_tables(seq):
    pos = np.arange(seq, dtype=np.float64)
    inv_freq = 1.0 / (ROPE_THETA ** (np.arange(0, MLA_ROPE, 2, dtype=np.float64) / MLA_ROPE))
    ang = pos[:, None] * inv_freq[None, :]
    cos, sin = np.cos(ang), np.sin(ang)
    z_nope = np.zeros((seq, MLA_NOPE))
    z_rope = np.zeros((seq, MLA_ROPE))
    cos_k = np.concatenate([z_nope, cos, cos, z_rope], axis=-1)
    sin_k = np.concatenate([z_nope, sin, sin, z_rope], axis=-1)
    qscale = MLA_QH ** -0.5 * LOG2E
    cos_q = np.concatenate([np.ones((seq, MLA_NOPE)), cos, cos, z_rope], axis=-1) * qscale
    sin_q = sin_k * qscale
    return tuple(jnp.asarray(t, F32) for t in (cos_q.T, sin_q.T, cos_k, sin_k))


def _t5_bucket(n):
    max_exact = REL_BUCKETS // 2
    nf = jnp.maximum(n, 1).astype(F32)
    large = max_exact + (jnp.log(nf / max_exact) / math.log(REL_MAX_DIST / max_exact)
                         * (REL_BUCKETS - max_exact)).astype(jnp.int32)
    large = jnp.minimum(large, REL_BUCKETS - 1)
    return jnp.where(n < max_exact, n, large)


def _toeplitz(g, tq):
    h = g.shape[0]
    padded = jnp.concatenate([g, jnp.zeros((h, 1), g.dtype)], axis=1)
    skewed = jnp.tile(padded, (1, tq))[:, :tq * (2 * tq - 1)].reshape(h, tq, 2 * tq - 1)
    return skewed[:, :, tq - 1:]


def _bias_table(rel_bias, tq):
    assert tq >= REL_MAX_DIST
    bias = rel_bias[_t5_bucket(jnp.arange(2 * tq))].astype(F32).T
    bias = (bias - rel_bias[REL_BUCKETS - 1].astype(F32)[:, None]) * LOG2E
    masked = jnp.full((bias.shape[0], tq - 1), NEG_BIG, F32)
    diagonal = jnp.concatenate([masked, bias[:, :tq]], axis=1)
    behind = bias[:, 1:]
    zeros = jnp.zeros((bias.shape[0], tq, tq), F32)
    return jnp.stack([_toeplitz(diagonal, tq), _toeplitz(behind, tq), zeros], axis=1)


def _causal_table(tq):
    kk = np.arange(tq)[:, None]
    qq = np.arange(tq)[None, :]
    return jnp.asarray(np.stack([np.where(qq >= kk, 0.0, NEG_BIG), np.zeros((tq, tq))]), F32)


def _pick(t, want):
    return want if t % want == 0 else t


class _Tiles(NamedTuple):
    attn: int
    in_rows: int
    in_cols: int
    merge_rows: int
    mlp_rows: int
    mlp_cols: int
    diff_group: int
    mla_group: int


def _plan_tiles(batch, seq):
    t = batch * seq
    return _Tiles(attn=_pick(seq, 256), in_rows=_pick(seq, 512), in_cols=512, merge_rows=_pick(t, 512),
                  mlp_rows=_pick(t, 1024), mlp_cols=1024, diff_group=4, mla_group=8)


def kernel(x, rel_bias, pre_mix_g, w_in, w_ret_o, lambda_q1, lambda_k1, lambda_q2, lambda_k2, diff_subln_g, w_diff_o, mla_q_norm_g, w_mla_uq, mla_kv_norm_g, w_mla_ukv, w_mla_o, w_out, post_mix_g, pre_mlp_g, w_up, w_down, post_mlp_g):
    batch, seq, d = x.shape
    t = batch * seq
    depth = w_in.shape[0]
    tiles = _plan_tiles(batch, seq)
    tq = tiles.attn
    assert d == D_MODEL and w_in.shape[1:] == (D_MODEL, W_IN_DV + DIFF_W + N_IN_SMALL - LANES + MLA_ROPE
                                               + N_BRANCH * D_MODEL)
    assert seq % RET_CHUNK == 0 and seq % tq == 0
    assert DIFF_HEADS % tiles.diff_group == 0 and MLA_HEADS % tiles.mla_group == 0
    ret_tabs = _ret_tables(seq)
    mla_tabs = _mla_tables(seq)
    bias_tab = _bias_table(rel_bias, tq)
    causal_tab = _causal_table(tq)
    colscale = _in_colscale()
    w_main, w_dvt, w_gates, w_small = _prep_w_in(w_in, tn=tiles.in_cols)
    merge_w = tuple(w.astype(BF16) for w in (w_ret_o, w_diff_o, w_mla_o, w_out))
    mlp_w = (w_up.astype(BF16), w_down.astype(BF16))
    row = lambda v: v.reshape(1, -1).astype(F32)

    x2 = x.reshape(t, d)
    for l in range(depth):
        wk, wvt = _prep_w_ukv(w_mla_ukv[l])
        proj, dvt, qt, km, vt = _in_proj(x2, row(pre_mix_g[l]), w_main, w_dvt, w_small, l, colscale,
                                         row(mla_q_norm_g[l]), row(mla_kv_norm_g[l]),
                                         _prep_w_uq(w_mla_uq[l]), wk, wvt, mla_tabs,
                                         seq=seq, tm=tiles.in_rows, tn=tiles.in_cols, tk=tq)

        ret = _retention(proj, ret_tabs, batch=batch, seq=seq)

        lambda_init = 0.8 - 0.6 * math.exp(-0.3 * l)
        lam = (jnp.exp(jnp.sum(lambda_q1[l] * lambda_k1[l]).astype(F32))
               - jnp.exp(jnp.sum(lambda_q2[l] * lambda_k2[l]).astype(F32)) + lambda_init)
        subln_g = jnp.broadcast_to((diff_subln_g[l].astype(F32) * (1.0 - lambda_init))[:, None], (2 * DIFF_DH, tq))
        da = _diff_attention(proj, dvt, lam.reshape(1), bias_tab, subln_g,
                             batch=batch, seq=seq, tq=tq, group=tiles.diff_group)

        mo = _mla_attention(qt, km, vt, causal_tab, batch=batch, seq=seq, tq=tq, group=tiles.mla_group)

        x2 = _merge(x2, ret, da, mo, row(pre_mix_g[l]), w_gates, *merge_w, l, row(post_mix_g[l]),
                    tm=tiles.merge_rows)

        x2 = _mlp(x2, row(pre_mlp_g[l]), *mlp_w, l, row(post_mlp_g[l]), tm=tiles.mlp_rows, tf=tiles.mlp_cols)
    return x2.reshape(batch, seq, d)
```

```python
import functools
import math
from typing import NamedTuple

import jax
import jax.numpy as jnp
import numpy as np
from jax import lax
from jax.experimental import pallas as pl
from jax.experimental.pallas import tpu as pltpu

F32 = jnp.float32
BF16 = jnp.bfloat16

D_MODEL = 1024
EPS = 1e-6
RET_HEADS = 4
RET_DK = 128
RET_DV = 256
RET_CHUNK = 128
DIFF_HEADS = 8
DIFF_DH = 64
MLA_HEADS = 16
MLA_Q_LORA = 256
MLA_KV_LORA = 128
MLA_NOPE = 64
MLA_ROPE = 32
MLA_V = 64
ROPE_THETA = 10000.0
REL_BUCKETS = 32
REL_MAX_DIST = 128
N_BRANCH = 3

LANES = 128
V7X_VMEM_BYTES = 64 * 1024 * 1024
VMEM_LIMIT = V7X_VMEM_BYTES * 7 // 8
LOG2E = math.log2(math.e)
NEG_BIG = -1e30

RET_QK_W = RET_HEADS * RET_DK
RET_V_W = RET_HEADS * RET_DV
DIFF_W = DIFF_HEADS * 2 * DIFF_DH
MLA_QH = MLA_NOPE + MLA_ROPE
MLA_KVH = MLA_NOPE + MLA_V
MLA_V_W = MLA_HEADS * MLA_V
MLA_PAD_W = MLA_HEADS * LANES

COL_RQ = 0
COL_RK = COL_RQ + RET_QK_W
COL_RV = COL_RK + RET_QK_W
COL_RG = COL_RV + RET_V_W
COL_DK = COL_RG + RET_V_W
N_IN_MAIN = COL_DK + DIFF_W
W_IN_DQ = COL_DK
W_IN_DK = W_IN_DQ + DIFF_W
W_IN_DV = W_IN_DK + DIFF_W
DIFF_Q_SCALE = DIFF_DH ** -0.5 * LOG2E
N_IN_SMALL = MLA_Q_LORA + MLA_KV_LORA + LANES


def _cparams(*sem):
    return pltpu.CompilerParams(dimension_semantics=sem, vmem_limit_bytes=VMEM_LIMIT)


def _rms(xf):
    return xf * lax.rsqrt(jnp.mean(xf * xf, axis=-1, keepdims=True) + EPS)


def _dot(a, b):
    return jnp.dot(a, b, preferred_element_type=F32)


def _dot_nt(a, b):
    return lax.dot_general(a, b, (((1,), (1,)), ((), ())), preferred_element_type=F32)


HEADS_PER_UP = 4


def _in_proj_kernel(x_ref, g_ref, wm_ref, wdvt_ref, ws_ref, cs_ref, gq_ref, gkv_ref, wqt_ref, wk_ref, wvt_ref,
                    cqt_tab, sqt_tab, ck_tab, sk_tab, o_ref, dvt_ref, qt_ref, k_ref, vt_ref, *, tn):
    h = (_rms(x_ref[...]) * g_ref[...]).astype(BF16)

    small = _dot(h, ws_ref[...])
    hq = (_rms(small[:, :MLA_Q_LORA]) * gq_ref[...]).astype(BF16)
    hkv = (_rms(small[:, MLA_Q_LORA:MLA_Q_LORA + MLA_KV_LORA]) * gkv_ref[...]).astype(BF16)
    kpe = small[:, MLA_Q_LORA + MLA_KV_LORA:]
    kpe = kpe * ck_tab[...] + pltpu.roll(kpe, LANES - MLA_ROPE, 1) * sk_tab[...]
    tk = vt_ref.shape[-1]
    n_tiles = vt_ref.shape[0]
    for c in range(n_tiles):
        vt_ref[c] = _dot_nt(wvt_ref[...], hkv[c * tk:(c + 1) * tk]).astype(vt_ref.dtype)
    cqt = cqt_tab[...]
    sqt = sqt_tab[...]
    up = HEADS_PER_UP * LANES
    for hg in range(MLA_HEADS // HEADS_PER_UP):
        q_t = _dot_nt(wqt_ref[hg * up:(hg + 1) * up, :], hq)
        ka = _dot(hkv, wk_ref[:, hg * up:(hg + 1) * up])
        for hh in range(HEADS_PER_UP):
            hs = slice(hh * LANES, (hh + 1) * LANES)
            out = slice(hg * up + hh * LANES, hg * up + (hh + 1) * LANES)
            blk = q_t[hs]
            partner = jnp.concatenate([blk[MLA_ROPE:], blk[:MLA_ROPE]], axis=0)
            blk = blk * cqt + partner * sqt
            for c in range(n_tiles):
                qt_ref[c, out, :] = blk[:, c * tk:(c + 1) * tk].astype(qt_ref.dtype)
            k_ref[:, out] = (ka[:, hs] + kpe).astype(k_ref.dtype)

    for c in range(wm_ref.shape[1] // tn):
        cols = slice(c * tn, (c + 1) * tn)
        o_ref[:, cols] = (_dot(h, wm_ref[:, cols]) * cs_ref[:, cols]).astype(o_ref.dtype)
    for r in range(wdvt_ref.shape[0] // tn):
        rows = slice(r * tn, (r + 1) * tn)
        dqv_t = _dot_nt(wdvt_ref[rows, :], h)
        if r * tn < DIFF_W:
            dqv_t = dqv_t * DIFF_Q_SCALE
        for c in range(n_tiles):
            dvt_ref[c, rows, :] = dqv_t[:, c * tk:(c + 1) * tk].astype(dvt_ref.dtype)


def _resident(shape):
    return pl.BlockSpec(shape, lambda *_: (0,) * len(shape), pipeline_mode=pl.Buffered(1))


def _layer_slab(w, layer):
    return pl.BlockSpec((None,) + w.shape[1:], lambda *_: (layer, 0, 0), pipeline_mode=pl.Buffered(1))


def _in_proj(x2, g, w_main, w_dvt, w_small, layer, colscale, gq, gkv, wqt, wk, wvt, tabs, *, seq, tm, tn, tk):
    t, d = x2.shape
    n = w_main.shape[2]
    assert n % tn == 0 and w_dvt.shape[1] % tn == 0 and tm % tk == 0 and seq % tm == 0
    ns = seq // tm
    nc = tm // tk
    cqt, sqt, ck, sk = tabs
    tab_t = pl.BlockSpec((LANES, tm), lambda i: (0, i % ns))
    tab = pl.BlockSpec((tm, LANES), lambda i: (i % ns, 0))
    return pl.pallas_call(
        functools.partial(_in_proj_kernel, tn=tn),
        grid=(t // tm,),
        in_specs=[pl.BlockSpec((tm, d), lambda i: (i, 0)), _resident((1, d)),
                  _layer_slab(w_main, layer), _layer_slab(w_dvt, layer), _layer_slab(w_small, layer),
                  _resident((1, n)), _resident(gq.shape), _resident(gkv.shape),
                  _resident(wqt.shape), _resident(wk.shape), _resident(wvt.shape),
                  tab_t, tab_t, tab, tab],
        out_specs=[pl.BlockSpec((tm, n), lambda i: (i, 0)),
                   pl.BlockSpec((nc, w_dvt.shape[1], tk), lambda i: (i, 0, 0)),
                   pl.BlockSpec((nc, MLA_PAD_W, tk), lambda i: (i, 0, 0)),
                   pl.BlockSpec((tm, MLA_PAD_W), lambda i: (i, 0)),
                   pl.BlockSpec((nc, MLA_V_W, tk), lambda i: (i, 0, 0))],
        out_shape=[jax.ShapeDtypeStruct((t, n), BF16),
                   jax.ShapeDtypeStruct((t // tk, w_dvt.shape[1], tk), BF16),
                   jax.ShapeDtypeStruct((t // tk, MLA_PAD_W, tk), BF16),
                   jax.ShapeDtypeStruct((t, MLA_PAD_W), BF16),
                   jax.ShapeDtypeStruct((t // tk, MLA_V_W, tk), BF16)],
        compiler_params=_cparams("parallel"),
        name="in_proj",
    )(x2, g, w_main, w_dvt, w_small, colscale, gq, gkv, wqt, wk, wvt, cqt, sqt, ck, sk)


def _ret_kernel(q_ref, k_ref, v_ref, rg_ref, cos_ref, sin_ref, dm_ref, xi_ref, zt_ref, gc_ref,
                o_ref, st_ref):
    st_ref[...] = jnp.zeros_like(st_ref)

    def chunk(n, carry):
        rows = _tile(n, RET_CHUNK)
        cos = cos_ref[rows, :]
        sin = sin_ref[rows, :]
        for h in range(RET_HEADS):
            qs = slice(h * RET_DK, (h + 1) * RET_DK)
            vs = slice(h * RET_DV, (h + 1) * RET_DV)
            q = q_ref[rows, qs].astype(F32)
            k = k_ref[rows, qs].astype(F32)
            qr = q * cos + pltpu.roll(q, RET_DK // 2, 1) * sin
            kr = k * cos + pltpu.roll(k, RET_DK // 2, 1) * sin
            v = v_ref[rows, vs]
            st = st_ref[h]
            inner = _dot_nt(qr.astype(BF16), kr.astype(BF16)) * dm_ref[h]
            out = _dot(inner.astype(BF16), v) + _dot((qr * xi_ref[h]).astype(BF16), st.astype(BF16))
            kz_t = (kr * zt_ref[h]).T.astype(BF16)
            st_ref[h] = gc_ref[h] * st + _dot(kz_t, v)
            rg = rg_ref[rows, vs].astype(F32)
            o_ref[rows, vs] = (_rms(out) * (rg * jax.nn.sigmoid(rg))).astype(o_ref.dtype)
        return carry

    lax.fori_loop(0, q_ref.shape[0] // RET_CHUNK, chunk, 0, unroll=4)


def _retention(proj, tabs, *, batch, seq):
    t = proj.shape[0]
    c = RET_CHUNK
    cos, sin, dmask, xi, zeta, gch = tabs
    return pl.pallas_call(
        _ret_kernel,
        grid=(batch,),
        in_specs=[pl.BlockSpec((seq, RET_QK_W), lambda b: (b, COL_RQ // RET_QK_W)),
                  pl.BlockSpec((seq, RET_QK_W), lambda b: (b, COL_RK // RET_QK_W)),
                  pl.BlockSpec((seq, RET_V_W), lambda b: (b, COL_RV // RET_V_W)),
                  pl.BlockSpec((seq, RET_V_W), lambda b: (b, COL_RG // RET_V_W)),
                  _resident((seq, RET_DK)), _resident((seq, RET_DK)),
                  _resident((RET_HEADS, c, c)), _resident((RET_HEADS, c, RET_DK)),
                  _resident((RET_HEADS, c, RET_DK)), _resident((RET_HEADS, 1, RET_DV))],
        out_specs=pl.BlockSpec((seq, RET_V_W), lambda b: (b, 0)),
        out_shape=jax.ShapeDtypeStruct((t, RET_V_W), BF16),
        scratch_shapes=[pltpu.VMEM((RET_HEADS, RET_DK, RET_DV), F32)],
        compiler_params=_cparams("parallel"),
        name="retention",
    )(proj, proj, proj, proj, cos, sin, dmask, xi, zeta, gch)


SUM_ROWS = 16
PAIRS_PER_BLOCK = 6


def _state_ring(nq):
    pairs = [(qi, j) for qi in range(nq) for j in range(qi + 1)]
    return max(len({qi for qi, _ in pairs[i:i + PAIRS_PER_BLOCK]}) for i in range(0, len(pairs), PAIRS_PER_BLOCK))


def _store_logits(s, s_ref, smax_ref):
    s_ref[...] = s
    smax_ref[...] = jnp.max(s, axis=0, keepdims=True)


def _softmax_update(s_ref, smax_ref, vt, m_ref, acc_ref):
    m = m_ref[...]
    m_new = jnp.maximum(m, smax_ref[...])
    alpha = jnp.exp2(m - m_new)
    p = jnp.exp2(s_ref[...] - m_new).astype(BF16)
    m_ref[...] = m_new
    vt_ones = jnp.concatenate([vt, jnp.ones((SUM_ROWS, vt.shape[1]), BF16)], axis=0)
    acc_ref[...] = alpha * acc_ref[...] + _dot(vt_ones, p)


def _softmax_result(acc_ref):
    dv = acc_ref.shape[0] - SUM_ROWS
    return acc_ref[:dv, :] / acc_ref[dv:dv + 1, :]


def _softmax_reset(m_ref, acc_ref):
    m_ref[...] = jnp.full(m_ref.shape, NEG_BIG, F32)
    acc_ref[...] = jnp.zeros(acc_ref.shape, F32)


def _softmax_restart(m_ref):
    m_ref[...] = jnp.full(m_ref.shape, NEG_BIG, F32)


def _softmax_scratch(group, dv, tk, n, ring):
    return [pltpu.VMEM((2, group, tk, n), F32), pltpu.VMEM((2, group, 1, n), F32),
            pltpu.VMEM((ring, group, 1, n), F32), pltpu.VMEM((ring, group, dv + SUM_ROWS, n), F32)]


def _tile(i, size):
    return pl.ds(pl.multiple_of(i * size, size), size)


def _causal_tile_pipeline(nq, group, logits, softmax, finalize):
    def following(pair):
        qi, j = pair
        last = j == qi
        return jnp.where(last, qi + 1, qi), jnp.where(last, 0, j + 1)

    def run(first, count, prefetch_next):
        pairs = [first]
        for _ in range(count):
            pairs.append(following(pairs[-1]))
        for i in range(count):
            if i + 1 < count or prefetch_next:
                qn, jn = pairs[i + 1]
                qn = jnp.minimum(qn, nq - 1)
                for g in range(group):
                    logits(g, qn, jn, (i + 1) % 2)
            for g in range(group):
                softmax(g, *pairs[i], i % 2)
        for qi, j in pairs[:count]:
            @pl.when(j == qi)
            def _(qi=qi):
                finalize(qi)
        return pairs[count]

    n_pairs = nq * (nq + 1) // 2
    zero = jnp.int32(0)
    for g in range(group):
        logits(g, zero, zero, 0)
    rest = lax.fori_loop(0, n_pairs // PAIRS_PER_BLOCK, lambda _, c: run(c, PAIRS_PER_BLOCK, True), (zero, zero))
    if n_pairs % PAIRS_PER_BLOCK:
        run(rest, n_pairs % PAIRS_PER_BLOCK, False)


def _diff_kernel(lam_ref, qt_ref, k_ref, vt_ref, tab_ref, gt_ref, o_ref, s_ref, smax_ref, m_ref, acc_ref,
                 *, tq, group):
    w = 2 * DIFF_DH
    nq = vt_ref.shape[0]
    hslice = [slice(g * w, (g + 1) * w) for g in range(group)]
    _softmax_reset(m_ref, acc_ref)
    ring = m_ref.shape[0]
    zeros = jnp.zeros((DIFF_DH, tq), BF16)

    def logits(g, qi, j, slot):
        qt = qt_ref[qi, hslice[g], :]
        q2 = jnp.concatenate([jnp.concatenate([qt[:DIFF_DH], zeros], axis=0),
                              jnp.concatenate([zeros, qt[DIFF_DH:]], axis=0)], axis=1)
        s = _dot(k_ref[_tile(j, tq), hslice[g]], q2)
        bias = tab_ref[g, jnp.minimum(qi - j, tab_ref.shape[1] - 1)]
        s = s + jnp.concatenate([bias, bias], axis=1)
        _store_logits(s, s_ref.at[slot, g], smax_ref.at[slot, g])

    def softmax(g, qi, j, slot):
        _softmax_update(s_ref.at[slot, g], smax_ref.at[slot, g], vt_ref[j, hslice[g], :],
                        m_ref.at[qi % ring, g], acc_ref.at[qi % ring, g])

    def finalize(qi):
        for g in range(group):
            o = _softmax_result(acc_ref.at[qi % ring, g])
            a = o[:, :tq] - lam_ref[0] * o[:, tq:]
            a = a * lax.rsqrt(jnp.mean(a * a, axis=0, keepdims=True) + EPS) * gt_ref[...]
            o_ref[_tile(qi, tq), hslice[g]] = a.T.astype(o_ref.dtype)
            _softmax_restart(m_ref.at[qi % ring, g])

    _causal_tile_pipeline(nq, group, logits, softmax, finalize)


def _diff_attention(proj, qvt, lam, tab, g_t, *, batch, seq, tq, group):
    t = proj.shape[0]
    nq = seq // tq
    w = group * 2 * DIFF_DH
    return pl.pallas_call(
        functools.partial(_diff_kernel, tq=tq, group=group),
        grid=(batch, DIFF_HEADS // group),
        in_specs=[pl.BlockSpec(memory_space=pltpu.SMEM),
                  pl.BlockSpec((nq, w, tq), lambda b, h: (b, h, 0)),
                  pl.BlockSpec((seq, w), lambda b, h: (b, COL_DK // w + h)),
                  pl.BlockSpec((nq, w, tq), lambda b, h: (b, DIFF_W // w + h, 0)),
                  pl.BlockSpec((group,) + tab.shape[1:], lambda b, h: (h, 0, 0, 0)),
                  _resident(g_t.shape)],
        out_specs=pl.BlockSpec((seq, w), lambda b, h: (b, h)),
        out_shape=jax.ShapeDtypeStruct((t, DIFF_W), BF16),
        scratch_shapes=_softmax_scratch(group, 2 * DIFF_DH, tq, 2 * tq, _state_ring(nq)),
        compiler_params=_cparams("parallel", "parallel"),
        name="diff_attention",
    )(lam, qvt, proj, qvt, tab, g_t)


def _mla_kernel(qt_ref, k_ref, vt_ref, mask_ref, o_ref, s_ref, smax_ref, m_ref, acc_ref, *, tq, group):
    hslice = [slice(g * LANES, (g + 1) * LANES) for g in range(group)]
    vslice = [slice(g * MLA_V, (g + 1) * MLA_V) for g in range(group)]
    _softmax_reset(m_ref, acc_ref)
    ring = m_ref.shape[0]

    def logits(g, qi, j, slot):
        qt = jnp.concatenate([qt_ref[qi, hslice[g], :][:MLA_QH], jnp.zeros((LANES - MLA_QH, tq), BF16)], axis=0)
        s = _dot(k_ref[_tile(j, tq), hslice[g]], qt)
        s = s + mask_ref[jnp.minimum(qi - j, 1)]
        _store_logits(s, s_ref.at[slot, g], smax_ref.at[slot, g])

    def softmax(g, qi, j, slot):
        _softmax_update(s_ref.at[slot, g], smax_ref.at[slot, g], vt_ref[j, vslice[g], :],
                        m_ref.at[qi % ring, g], acc_ref.at[qi % ring, g])

    def finalize(qi):
        o = jnp.concatenate([_softmax_result(acc_ref.at[qi % ring, g]) for g in range(group)], axis=0)
        o_ref[_tile(qi, tq), :] = o.T.astype(o_ref.dtype)
        for g in range(group):
            _softmax_restart(m_ref.at[qi % ring, g])

    _causal_tile_pipeline(vt_ref.shape[0], group, logits, softmax, finalize)


def _mla_attention(qt, km, vt, mask, *, batch, seq, tq, group):
    t = km.shape[0]
    nq = seq // tq
    return pl.pallas_call(
        functools.partial(_mla_kernel, tq=tq, group=group),
        grid=(batch, MLA_HEADS // group),
        in_specs=[pl.BlockSpec((nq, group * LANES, tq), lambda b, h: (b, h, 0)),
                  pl.BlockSpec((seq, group * LANES), lambda b, h: (b, h)),
                  pl.BlockSpec((nq, group * MLA_V, tq), lambda b, h: (b, h, 0)),
                  _resident(mask.shape)],
        out_specs=pl.BlockSpec((seq, group * MLA_V), lambda b, h: (b, h)),
        out_shape=jax.ShapeDtypeStruct((t, MLA_V_W), BF16),
        scratch_shapes=_softmax_scratch(group, MLA_V, tq, tq, _state_ring(nq)),
        compiler_params=_cparams("parallel", "parallel"),
        name="mla_attention",
    )(qt, km, vt, mask)


def _merge_kernel(x_ref, a_ref, b_ref, c_ref, g_ref, wg_ref, wa_ref, wb_ref, wc_ref, wo_ref, pg_ref, o_ref):
    x = x_ref[...]
    d = x.shape[1]
    h = (_rms(x) * g_ref[...]).astype(BF16)
    merged = None
    for n, (br_ref, w_ref) in enumerate(((a_ref, wa_ref), (b_ref, wb_ref), (c_ref, wc_ref))):
        gate = jax.nn.sigmoid(_dot(h, wg_ref[:, n * d:(n + 1) * d]))
        part = gate * _dot(br_ref[...], w_ref[...])
        merged = part if merged is None else merged + part
    y = _dot(merged.astype(BF16), wo_ref[...])
    o_ref[...] = x + _rms(y) * pg_ref[...]


def _merge(x2, a, b, c, g, w_gates, wa, wb, wc, wo, layer, pg, *, tm):
    t, d = x2.shape
    rowblk = pl.BlockSpec((tm, d), lambda i: (i, 0))
    return pl.pallas_call(
        _merge_kernel,
        grid=(t // tm,),
        in_specs=[rowblk, rowblk, rowblk, rowblk, _resident((1, d))]
        + [_layer_slab(w, layer) for w in (w_gates, wa, wb, wc, wo)] + [_resident((1, d))],
        out_specs=rowblk,
        out_shape=jax.ShapeDtypeStruct((t, d), F32),
        compiler_params=_cparams("parallel"),
        name="merge",
    )(x2, a, b, c, g, w_gates, wa, wb, wc, wo, pg)


def _mlp_kernel(x_ref, g_ref, wu_ref, wd_ref, pg_ref, o_ref, *, tf):
    x = x_ref[...]
    h = (_rms(x) * g_ref[...]).astype(BF16)
    acc = None
    for c in range(wu_ref.shape[1] // tf):
        u = jnp.maximum(_dot(h, wu_ref[:, c * tf:(c + 1) * tf]), 0.0)
        part = _dot((u * u).astype(BF16), wd_ref[c * tf:(c + 1) * tf, :])
        acc = part if acc is None else acc + part
    o_ref[...] = x + _rms(acc) * pg_ref[...]


def _mlp(x2, g, wu, wd, layer, pg, *, tm, tf):
    t, d = x2.shape
    rowblk = pl.BlockSpec((tm, d), lambda i: (i, 0))
    return pl.pallas_call(
        functools.partial(_mlp_kernel, tf=tf),
        grid=(t // tm,),
        in_specs=[rowblk, _resident((1, d)), _layer_slab(wu, layer), _layer_slab(wd, layer), _resident((1, d))],
        out_specs=rowblk,
        out_shape=jax.ShapeDtypeStruct((t, d), F32),
        compiler_params=_cparams("parallel"),
        name="mlp",
    )(x2, g, wu, wd, pg)


def _rot_half_cols(w):
    half = w.shape[-1] // 2
    return jnp.concatenate([-w[..., half:], w[..., :half]], axis=-1)


def _cast_transposed_kernel(wt_ref, o_ref):
    o_ref[...] = wt_ref[...].T.astype(o_ref.dtype)


def _cast_transposed(wt, n, *, tn, skip=None):
    depth, _, d = wt.shape
    if skip is None:
        src = lambda c: c
    else:
        src = lambda c: jnp.where(c < skip[0] // tn, c, c + (skip[1] - skip[0]) // tn)
    return pl.pallas_call(
        _cast_transposed_kernel,
        grid=(depth, n // tn),
        in_specs=[pl.BlockSpec((None, tn, d), lambda l, c: (l, src(c), 0))],
        out_specs=pl.BlockSpec((None, d, tn), lambda l, c: (l, 0, c)),
        out_shape=jax.ShapeDtypeStruct((depth, d, n), BF16),
        compiler_params=_cparams("parallel", "parallel"),
        name="w_in_prep",
    )(wt)


def _prep_w_in(w_in, *, tn):
    wt = jnp.swapaxes(w_in, 1, 2)
    cq_at = W_IN_DV + DIFF_W
    kpe_at = cq_at + MLA_Q_LORA + MLA_KV_LORA
    gates_at = kpe_at + MLA_ROPE
    kpe = wt[:, kpe_at:gates_at]
    half = MLA_ROPE // 2
    zeros = jnp.zeros((wt.shape[0], MLA_NOPE, wt.shape[2]), wt.dtype)
    small = jnp.concatenate([wt[:, cq_at:kpe_at], zeros, kpe, -kpe[:, half:], kpe[:, :half]], axis=1)
    dqv = jnp.concatenate([wt[:, W_IN_DQ:W_IN_DK], wt[:, W_IN_DV:cq_at]], axis=1)
    return (_cast_transposed(wt, N_IN_MAIN, tn=tn, skip=(W_IN_DQ, W_IN_DK)),
            dqv.astype(BF16),
            _cast_transposed(wt[:, gates_at:], N_BRANCH * D_MODEL, tn=tn),
            _cast_transposed(small, N_IN_SMALL, tn=tn))


def _in_colscale():
    cs = np.ones((1, N_IN_MAIN), np.float32)
    cs[0, COL_RK:COL_RK + RET_QK_W] = RET_DK ** -0.5
    return jnp.asarray(cs)


def _prep_w_uq(w):
    w = w.reshape(MLA_Q_LORA, MLA_HEADS, MLA_QH)
    pe = w[..., MLA_NOPE:]
    w = jnp.concatenate([w, _rot_half_cols(pe)], axis=-1).reshape(MLA_Q_LORA, MLA_PAD_W)
    return w.T.astype(BF16)


def _prep_w_ukv(w):
    w = w.reshape(MLA_KV_LORA, MLA_HEADS, MLA_KVH)
    wk = jnp.concatenate([w[..., :MLA_NOPE], jnp.zeros_like(w[..., :MLA_NOPE])], axis=-1)
    return (wk.reshape(MLA_KV_LORA, MLA_PAD_W).astype(BF16),
            w[..., MLA_NOPE:].reshape(MLA_KV_LORA, MLA_V_W).T.astype(BF16))


def _ret_tables(seq):
    pos = np.arange(seq, dtype=np.float64)
    inv_freq = 1.0 / (10000.0 ** np.linspace(0.0, 1.0, RET_DK // 2))
    ang = pos[:, None] * inv_freq[None, :]
    cos, sin = np.cos(ang), np.sin(ang)
    cos_t = np.concatenate([cos, cos], axis=-1)
    sin_t = np.concatenate([-sin, sin], axis=-1)
    log_g = np.log1p(-np.exp2(-5.0 - np.arange(RET_HEADS, dtype=np.float64)))
    idx = np.arange(RET_CHUNK, dtype=np.float64)
    rel = idx[:, None] - idx[None, :]
    dmask = np.where(rel >= 0, np.exp(np.maximum(rel, 0.0)[None] * log_g[:, None, None]), 0.0)
    xi = np.exp((idx + 1.0)[None, :] * log_g[:, None])[:, :, None]
    zeta = np.exp((RET_CHUNK - 1.0 - idx)[None, :] * log_g[:, None])[:, :, None]
    g_chunk = np.exp(RET_CHUNK * log_g)[:, None, None]
    tabs = (cos_t, sin_t, dmask,
            np.broadcast_to(xi, (RET_HEADS, RET_CHUNK, RET_DK)),
            np.broadcast_to(zeta, (RET_HEADS, RET_CHUNK, RET_DK)),
            np.broadcast_to(g_chunk, (RET_HEADS, 1, RET_DV)))
    return tuple(jnp.asarray(t, F32) for t in tabs)


def _mla_tables(seq):
    pos = np.arange(seq, dtype=np.float64)
    inv_freq = 1.0 / (ROPE_THETA ** (np.arange(0, MLA_ROPE, 2, dtype=np.float64) / MLA_ROPE))
    ang = pos[:, None] * inv_freq[None, :]
    cos, sin = np.cos(ang), np.sin(ang)
    z_nope = np.zeros((seq, MLA_NOPE))
    z_rope = np.zeros((seq, MLA_ROPE))
    cos_k = np.concatenate([z_nope, cos, cos, z_rope], axis=-1)
    sin_k = np.concatenate([z_nope, sin, sin, z_rope], axis=-1)
    qscale = MLA_QH ** -0.5 * LOG2E
    cos_q = np.concatenate([np.ones((seq, MLA_NOPE)), cos, cos, z_rope], axis=-1) * qscale
    sin_q = sin_k * qscale
    return tuple(jnp.asarray(t, F32) for t in (cos_q.T, sin_q.T, cos_k, sin_k))


def _t5_bucket(n):
    max_exact = REL_BUCKETS // 2
    nf = jnp.maximum(n, 1).astype(F32)
    large = max_exact + (jnp.log(nf / max_exact) / math.log(REL_MAX_DIST / max_exact)
                         * (REL_BUCKETS - max_exact)).astype(jnp.int32)
    large = jnp.minimum(large, REL_BUCKETS - 1)
    return jnp.where(n < max_exact, n, large)


def _toeplitz(g, tq):
    h = g.shape[0]
    padded = jnp.concatenate([g, jnp.zeros((h, 1), g.dtype)], axis=1)
    skewed = jnp.tile(padded, (1, tq))[:, :tq * (2 * tq - 1)].reshape(h, tq, 2 * tq - 1)
    return skewed[:, :, tq - 1:]


def _bias_table(rel_bias, tq):
    assert tq >= REL_MAX_DIST
    bias = rel_bias[_t5_bucket(jnp.arange(2 * tq))].astype(F32).T
    bias = (bias - rel_bias[REL_BUCKETS - 1].astype(F32)[:, None]) * LOG2E
    masked = jnp.full((bias.shape[0], tq - 1), NEG_BIG, F32)
    diagonal = jnp.concatenate([masked, bias[:, :tq]], axis=1)
    behind = bias[:, 1:]
    zeros = jnp.zeros((bias.shape[0], tq, tq), F32)
    return jnp.stack([_toeplitz(diagonal, tq), _toeplitz(behind, tq), zeros], axis=1)


def _causal_table(tq):
    kk = np.arange(tq)[:, None]
    qq = np.arange(tq)[None, :]
    return jnp.asarray(np.stack([np.where(qq >= kk, 0.0, NEG_BIG), np.zeros((tq, tq))]), F32)


def _pick(t, want):
    return want if t % want == 0 else t


class _Tiles(NamedTuple):
    attn: int
    in_rows: int
    in_cols: int
    merge_rows: int
    mlp_rows: int
    mlp_cols: int
    diff_group: int
    mla_group: int


def _plan_tiles(batch, seq):
    t = batch * seq
    return _Tiles(attn=_pick(seq, 256), in_rows=_pick(seq, 512), in_cols=512, merge_rows=_pick(t, 512),
                  mlp_rows=_pick(t, 1024), mlp_cols=1024, diff_group=4, mla_group=8)


def kernel(x, rel_bias, pre_mix_g, w_in, w_ret_o, lambda_q1, lambda_k1, lambda_q2, lambda_k2, diff_subln_g, w_diff_o, mla_q_norm_g, w_mla_uq, mla_kv_norm_g, w_mla_ukv, w_mla_o, w_out, post_mix_g, pre_mlp_g, w_up, w_down, post_mlp_g):
    batch, seq, d = x.shape
    t = batch * seq
    depth = w_in.shape[0]
    tiles = _plan_tiles(batch, seq)
    tq = tiles.attn
    assert d == D_MODEL and w_in.shape[1:] == (D_MODEL, W_IN_DV + DIFF_W + N_IN_SMALL - LANES + MLA_ROPE
                                               + N_BRANCH * D_MODEL)
    assert seq % RET_CHUNK == 0 and seq % tq == 0
    assert DIFF_HEADS % tiles.diff_group == 0 and MLA_HEADS % tiles.mla_group == 0
    ret_tabs = _ret_tables(seq)
    mla_tabs = _mla_tables(seq)
    bias_tab = _bias_table(rel_bias, tq)
    causal_tab = _causal_table(tq)
    colscale = _in_colscale()
    w_main, w_dvt, w_gates, w_small = _prep_w_in(w_in, tn=tiles.in_cols)
    merge_w = tuple(w.astype(BF16) for w in (w_ret_o, w_diff_o, w_mla_o, w_out))
    mlp_w = (w_up.astype(BF16), w_down.astype(BF16))
    row = lambda v: v.reshape(1, -1).astype(F32)

    x2 = x.reshape(t, d)
    for l in range(depth):
        wk, wvt = _prep_w_ukv(w_mla_ukv[l])
        proj, dvt, qt, km, vt = _in_proj(x2, row(pre_mix_g[l]), w_main, w_dvt, w_small, l, colscale,
                                         row(mla_q_norm_g[l]), row(mla_kv_norm_g[l]),
                                         _prep_w_uq(w_mla_uq[l]), wk, wvt, mla_tabs,
                                         seq=seq, tm=tiles.in_rows, tn=tiles.in_cols, tk=tq)

        ret = _retention(proj, ret_tabs, batch=batch, seq=seq)

        lambda_init = 0.8 - 0.6 * math.exp(-0.3 * l)
        lam = (jnp.exp(jnp.sum(lambda_q1[l] * lambda_k1[l]).astype(F32))
               - jnp.exp(jnp.sum(lambda_q2[l] * lambda_k2[l]).astype(F32)) + lambda_init)
        subln_g = jnp.broadcast_to((diff_subln_g[l].astype(F32) * (1.0 - lambda_init))[:, None], (2 * DIFF_DH, tq))
        da = _diff_attention(proj, dvt, lam.reshape(1), bias_tab, subln_g,
                             batch=batch, seq=seq, tq=tq, group=tiles.diff_group)

        mo = _mla_attention(qt, km, vt, causal_tab, batch=batch, seq=seq, tq=tq, group=tiles.mla_group)

        x2 = _merge(x2, ret, da, mo, row(pre_mix_g[l]), w_gates, *merge_w, l, row(post_mix_g[l]),
                    tm=tiles.merge_rows)

        x2 = _mlp(x2, row(pre_mlp_g[l]), *mlp_w, l, row(post_mlp_g[l]), tm=tiles.mlp_rows, tf=tiles.mlp_cols)
    return x2.reshape(batch, seq, d)
```

```python
import functools
import math
from typing import NamedTuple

import jax
import jax.numpy as jnp
import numpy as np
from jax import lax
from jax.experimental import pallas as pl
from jax.experimental.pallas import tpu as pltpu

F32 = jnp.float32
BF16 = jnp.bfloat16

D_MODEL = 1024
EPS = 1e-6
RET_HEADS = 4
RET_DK = 128
RET_DV = 256
RET_CHUNK = 128
DIFF_HEADS = 8
DIFF_DH = 64
MLA_HEADS = 16
MLA_Q_LORA = 256
MLA_KV_LORA = 128
MLA_NOPE = 64
MLA_ROPE = 32
MLA_V = 64
ROPE_THETA = 10000.0
REL_BUCKETS = 32
REL_MAX_DIST = 128
N_BRANCH = 3

LANES = 128
V7X_VMEM_BYTES = 64 * 1024 * 1024
VMEM_LIMIT = V7X_VMEM_BYTES * 7 // 8
LOG2E = math.log2(math.e)
NEG_BIG = -1e30

RET_QK_W = RET_HEADS * RET_DK
RET_V_W = RET_HEADS * RET_DV
DIFF_W = DIFF_HEADS * 2 * DIFF_DH
MLA_QH = MLA_NOPE + MLA_ROPE
MLA_KVH = MLA_NOPE + MLA_V
MLA_V_W = MLA_HEADS * MLA_V
MLA_PAD_W = MLA_HEADS * LANES

COL_RQ = 0
COL_RK = COL_RQ + RET_QK_W
COL_RV = COL_RK + RET_QK_W
COL_RG = COL_RV + RET_V_W
COL_DK = COL_RG + RET_V_W
N_IN_MAIN = COL_DK + DIFF_W
W_IN_DQ = COL_DK
W_IN_DK = W_IN_DQ + DIFF_W
W_IN_DV = W_IN_DK + DIFF_W
DIFF_Q_SCALE = DIFF_DH ** -0.5 * LOG2E
N_IN_SMALL = MLA_Q_LORA + MLA_KV_LORA + LANES


def _cparams(*sem):
    return pltpu.CompilerParams(dimension_semantics=sem, vmem_limit_bytes=VMEM_LIMIT)


def _rms(xf):
    return xf * lax.rsqrt(jnp.mean(xf * xf, axis=-1, keepdims=True) + EPS)


def _dot(a, b):
    return jnp.dot(a, b, preferred_element_type=F32)


def _dot_nt(a, b):
    return lax.dot_general(a, b, (((1,), (1,)), ((), ())), preferred_element_type=F32)


HEADS_PER_UP = 4


def _in_proj_kernel(x_ref, g_ref, wm_ref, wdvt_ref, ws_ref, cs_ref, gq_ref, gkv_ref, wqt_ref, wk_ref, wvt_ref,
                    cqt_tab, sqt_tab, ck_tab, sk_tab, o_ref, dvt_ref, qt_ref, k_ref, vt_ref, *, tn):
    h = (_rms(x_ref[...]) * g_ref[...]).astype(BF16)

    small = _dot(h, ws_ref[...])
    hq = (_rms(small[:, :MLA_Q_LORA]) * gq_ref[...]).astype(BF16)
    hkv = (_rms(small[:, MLA_Q_LORA:MLA_Q_LORA + MLA_KV_LORA]) * gkv_ref[...]).astype(BF16)
    kpe = small[:, MLA_Q_LORA + MLA_KV_LORA:]
    kpe = kpe * ck_tab[...] + pltpu.roll(kpe, LANES - MLA_ROPE, 1) * sk_tab[...]
    tk = vt_ref.shape[-1]
    n_tiles = vt_ref.shape[0]
    for c in range(n_tiles):
        vt_ref[c] = _dot_nt(wvt_ref[...], hkv[c * tk:(c + 1) * tk]).astype(vt_ref.dtype)
    cqt = cqt_tab[...]
    sqt = sqt_tab[...]
    up = HEADS_PER_UP * LANES
    for hg in range(MLA_HEADS // HEADS_PER_UP):
        q_t = _dot_nt(wqt_ref[hg * up:(hg + 1) * up, :], hq)
        ka = _dot(hkv, wk_ref[:, hg * up:(hg + 1) * up])
        for hh in range(HEADS_PER_UP):
            hs = slice(hh * LANES, (hh + 1) * LANES)
            out = slice(hg * up + hh * LANES, hg * up + (hh + 1) * LANES)
            blk = q_t[hs]
            partner = jnp.concatenate([blk[MLA_ROPE:], blk[:MLA_ROPE]], axis=0)
            blk = blk * cqt + partner * sqt
            for c in range(n_tiles):
                qt_ref[c, out, :] = blk[:, c * tk:(c + 1) * tk].astype(qt_ref.dtype)
            k_ref[:, out] = (ka[:, hs] + kpe).astype(k_ref.dtype)

    for c in range(wm_ref.shape[1] // tn):
        cols = slice(c * tn, (c + 1) * tn)
        o_ref[:, cols] = (_dot(h, wm_ref[:, cols]) * cs_ref[:, cols]).astype(o_ref.dtype)
    for r in range(wdvt_ref.shape[0] // tn):
        rows = slice(r * tn, (r + 1) * tn)
        dqv_t = _dot_nt(wdvt_ref[rows, :], h)
        if r * tn < DIFF_W:
            dqv_t = dqv_t * DIFF_Q_SCALE
        for c in range(n_tiles):
            dvt_ref[c, rows, :] = dqv_t[:, c * tk:(c + 1) * tk].astype(dvt_ref.dtype)


def _resident(shape):
    return pl.BlockSpec(shape, lambda *_: (0,) * len(shape), pipeline_mode=pl.Buffered(1))


def _layer_slab(w, layer):
    return pl.BlockSpec((None,) + w.shape[1:], lambda *_: (layer, 0, 0), pipeline_mode=pl.Buffered(1))


def _in_proj(x2, g, w_main, w_dvt, w_small, layer, colscale, gq, gkv, wqt, wk, wvt, tabs, *, seq, tm, tn, tk):
    t, d = x2.shape
    n = w_main.shape[2]
    assert n % tn == 0 and w_dvt.shape[1] % tn == 0 and tm % tk == 0 and seq % tm == 0
    ns = seq // tm
    nc = tm // tk
    cqt, sqt, ck, sk = tabs
    tab_t = pl.BlockSpec((LANES, tm), lambda i: (0, i % ns))
    tab = pl.BlockSpec((tm, LANES), lambda i: (i % ns, 0))
    return pl.pallas_call(
        functools.partial(_in_proj_kernel, tn=tn),
        grid=(t // tm,),
        in_specs=[pl.BlockSpec((tm, d), lambda i: (i, 0)), _resident((1, d)),
                  _layer_slab(w_main, layer), _layer_slab(w_dvt, layer), _layer_slab(w_small, layer),
                  _resident((1, n)), _resident(gq.shape), _resident(gkv.shape),
                  _resident(wqt.shape), _resident(wk.shape), _resident(wvt.shape),
                  tab_t, tab_t, tab, tab],
        out_specs=[pl.BlockSpec((tm, n), lambda i: (i, 0)),
                   pl.BlockSpec((nc, w_dvt.shape[1], tk), lambda i: (i, 0, 0)),
                   pl.BlockSpec((nc, MLA_PAD_W, tk), lambda i: (i, 0, 0)),
                   pl.BlockSpec((tm, MLA_PAD_W), lambda i: (i, 0)),
                   pl.BlockSpec((nc, MLA_V_W, tk), lambda i: (i, 0, 0))],
        out_shape=[jax.ShapeDtypeStruct((t, n), BF16),
                   jax.ShapeDtypeStruct((t // tk, w_dvt.shape[1], tk), BF16),
                   jax.ShapeDtypeStruct((t // tk, MLA_PAD_W, tk), BF16),
                   jax.ShapeDtypeStruct((t, MLA_PAD_W), BF16),
                   jax.ShapeDtypeStruct((t // tk, MLA_V_W, tk), BF16)],
        compiler_params=_cparams("parallel"),
        name="in_proj",
    )(x2, g, w_main, w_dvt, w_small, colscale, gq, gkv, wqt, wk, wvt, cqt, sqt, ck, sk)


def _ret_kernel(q_ref, k_ref, v_ref, rg_ref, cos_ref, sin_ref, dm_ref, xi_ref, zt_ref, gc_ref,
                o_ref, st_ref):
    st_ref[...] = jnp.zeros_like(st_ref)

    def chunk(n, carry):
        rows = _tile(n, RET_CHUNK)
        cos = cos_ref[rows, :]
        sin = sin_ref[rows, :]
        for h in range(RET_HEADS):
            qs = slice(h * RET_DK, (h + 1) * RET_DK)
            vs = slice(h * RET_DV, (h + 1) * RET_DV)
            q = q_ref[rows, qs].astype(F32)
            k = k_ref[rows, qs].astype(F32)
            qr = q * cos + pltpu.roll(q, RET_DK // 2, 1) * sin
            kr = k * cos + pltpu.roll(k, RET_DK // 2, 1) * sin
            v = v_ref[rows, vs]
            st = st_ref[h]
            inner = _dot_nt(qr.astype(BF16), kr.astype(BF16)) * dm_ref[h]
            out = _dot(inner.astype(BF16), v) + _dot((qr * xi_ref[h]).astype(BF16), st.astype(BF16))
            kz_t = (kr * zt_ref[h]).T.astype(BF16)
            st_ref[h] = gc_ref[h] * st + _dot(kz_t, v)
            rg = rg_ref[rows, vs].astype(F32)
            o_ref[rows, vs] = (_rms(out) * (rg * jax.nn.sigmoid(rg))).astype(o_ref.dtype)
        return carry

    lax.fori_loop(0, q_ref.shape[0] // RET_CHUNK, chunk, 0, unroll=4)


def _retention(proj, tabs, *, batch, seq):
    t = proj.shape[0]
    c = RET_CHUNK
    cos, sin, dmask, xi, zeta, gch = tabs
    return pl.pallas_call(
        _ret_kernel,
        grid=(batch,),
        in_specs=[pl.BlockSpec((seq, RET_QK_W), lambda b: (b, COL_RQ // RET_QK_W)),
                  pl.BlockSpec((seq, RET_QK_W), lambda b: (b, COL_RK // RET_QK_W)),
                  pl.BlockSpec((seq, RET_V_W), lambda b: (b, COL_RV // RET_V_W)),
                  pl.BlockSpec((seq, RET_V_W), lambda b: (b, COL_RG // RET_V_W)),
                  _resident((seq, RET_DK)), _resident((seq, RET_DK)),
                  _resident((RET_HEADS, c, c)), _resident((RET_HEADS, c, RET_DK)),
                  _resident((RET_HEADS, c, RET_DK)), _resident((RET_HEADS, 1, RET_DV))],
        out_specs=pl.BlockSpec((seq, RET_V_W), lambda b: (b, 0)),
        out_shape=jax.ShapeDtypeStruct((t, RET_V_W), BF16),
        scratch_shapes=[pltpu.VMEM((RET_HEADS, RET_DK, RET_DV), F32)],
        compiler_params=_cparams("parallel"),
        name="retention",
    )(proj, proj, proj, proj, cos, sin, dmask, xi, zeta, gch)


SUM_ROWS = 16
PAIRS_PER_BLOCK = 8


def _state_ring(nq):
    pairs = [(qi, j) for qi in range(nq) for j in range(qi + 1)]
    rows = max(len({qi for qi, _ in pairs[i:i + PAIRS_PER_BLOCK]}) for i in range(0, len(pairs), PAIRS_PER_BLOCK))
    return 1 << (rows - 1).bit_length()


def _store_logits(s, s_ref, smax_ref):
    s_ref[...] = s
    smax_ref[...] = jnp.max(s, axis=0, keepdims=True)


def _softmax_update(s_ref, smax_ref, vt, m_ref, acc_ref):
    m = m_ref[...]
    m_new = jnp.maximum(m, smax_ref[...])
    alpha = jnp.exp2(m - m_new)
    p = jnp.exp2(s_ref[...] - m_new).astype(BF16)
    m_ref[...] = m_new
    vt_ones = jnp.concatenate([vt, jnp.ones((SUM_ROWS, vt.shape[1]), BF16)], axis=0)
    acc_ref[...] = alpha * acc_ref[...] + _dot(vt_ones, p)


def _softmax_result(acc_ref):
    dv = acc_ref.shape[0] - SUM_ROWS
    return acc_ref[:dv, :] / acc_ref[dv:dv + 1, :]


def _softmax_reset(m_ref, acc_ref):
    m_ref[...] = jnp.full(m_ref.shape, NEG_BIG, F32)
    acc_ref[...] = jnp.zeros(acc_ref.shape, F32)


def _softmax_restart(m_ref):
    m_ref[...] = jnp.full(m_ref.shape, NEG_BIG, F32)


def _softmax_scratch(group, dv, tk, n, ring):
    return [pltpu.VMEM((2, group, tk, n), F32), pltpu.VMEM((2, group, 1, n), F32),
            pltpu.VMEM((ring, group, 1, n), F32), pltpu.VMEM((ring, group, dv + SUM_ROWS, n), F32)]


def _tile(i, size):
    return pl.ds(pl.multiple_of(i * size, size), size)


def _causal_tile_pipeline(nq, group, logits, softmax, finalize):
    def following(pair):
        qi, j = pair
        last = j == qi
        return jnp.where(last, qi + 1, qi), jnp.where(last, 0, j + 1)

    def run(first, count, prefetch_next):
        pairs = [first]
        for _ in range(count):
            pairs.append(following(pairs[-1]))
        for i in range(count):
            if i + 1 < count or prefetch_next:
                qn, jn = pairs[i + 1]
                qn = jnp.minimum(qn, nq - 1)
                for g in range(group):
                    logits(g, qn, jn, (i + 1) % 2)
            for g in range(group):
                softmax(g, *pairs[i], i % 2)
        for qi, j in pairs[:count]:
            @pl.when(j == qi)
            def _(qi=qi):
                finalize(qi)
        return pairs[count]

    n_pairs = nq * (nq + 1) // 2
    zero = jnp.int32(0)
    for g in range(group):
        logits(g, zero, zero, 0)
    rest = lax.fori_loop(0, n_pairs // PAIRS_PER_BLOCK, lambda _, c: run(c, PAIRS_PER_BLOCK, True), (zero, zero))
    if n_pairs % PAIRS_PER_BLOCK:
        run(rest, n_pairs % PAIRS_PER_BLOCK, False)


def _diff_kernel(lam_ref, qt_ref, k_ref, vt_ref, tab_ref, gt_ref, o_ref, s_ref, smax_ref, m_ref, acc_ref,
                 *, tq, group):
    w = 2 * DIFF_DH
    nq = vt_ref.shape[0]
    hslice = [slice(g * w, (g + 1) * w) for g in range(group)]
    _softmax_reset(m_ref, acc_ref)
    ring = m_ref.shape[0]
    zeros = jnp.zeros((DIFF_DH, tq), BF16)

    def logits(g, qi, j, slot):
        qt = qt_ref[qi, hslice[g], :]
        q2 = jnp.concatenate([jnp.concatenate([qt[:DIFF_DH], zeros], axis=0),
                              jnp.concatenate([zeros, qt[DIFF_DH:]], axis=0)], axis=1)
        s = _dot(k_ref[_tile(j, tq), hslice[g]], q2)
        bias = tab_ref[g, jnp.minimum(qi - j, tab_ref.shape[1] - 1)]
        s = s + jnp.concatenate([bias, bias], axis=1)
        _store_logits(s, s_ref.at[slot, g], smax_ref.at[slot, g])

    def softmax(g, qi, j, slot):
        _softmax_update(s_ref.at[slot, g], smax_ref.at[slot, g], vt_ref[j, hslice[g], :],
                        m_ref.at[qi % ring, g], acc_ref.at[qi % ring, g])

    def finalize(qi):
        for g in range(group):
            o = _softmax_result(acc_ref.at[qi % ring, g])
            a = o[:, :tq] - lam_ref[0] * o[:, tq:]
            a = a * lax.rsqrt(jnp.mean(a * a, axis=0, keepdims=True) + EPS) * gt_ref[...]
            o_ref[_tile(qi, tq), hslice[g]] = a.T.astype(o_ref.dtype)
            _softmax_restart(m_ref.at[qi % ring, g])

    _causal_tile_pipeline(nq, group, logits, softmax, finalize)


def _diff_attention(proj, qvt, lam, tab, g_t, *, batch, seq, tq, group):
    t = proj.shape[0]
    nq = seq // tq
    w = group * 2 * DIFF_DH
    return pl.pallas_call(
        functools.partial(_diff_kernel, tq=tq, group=group),
        grid=(batch, DIFF_HEADS // group),
        in_specs=[pl.BlockSpec(memory_space=pltpu.SMEM),
                  pl.BlockSpec((nq, w, tq), lambda b, h: (b, h, 0)),
                  pl.BlockSpec((seq, w), lambda b, h: (b, COL_DK // w + h)),
                  pl.BlockSpec((nq, w, tq), lambda b, h: (b, DIFF_W // w + h, 0)),
                  pl.BlockSpec((group,) + tab.shape[1:], lambda b, h: (h, 0, 0, 0)),
                  _resident(g_t.shape)],
        out_specs=pl.BlockSpec((seq, w), lambda b, h: (b, h)),
        out_shape=jax.ShapeDtypeStruct((t, DIFF_W), BF16),
        scratch_shapes=_softmax_scratch(group, 2 * DIFF_DH, tq, 2 * tq, _state_ring(nq)),
        compiler_params=_cparams("parallel", "parallel"),
        name="diff_attention",
    )(lam, qvt, proj, qvt, tab, g_t)


def _mla_kernel(qt_ref, k_ref, vt_ref, mask_ref, o_ref, s_ref, smax_ref, m_ref, acc_ref, *, tq, group):
    hslice = [slice(g * LANES, (g + 1) * LANES) for g in range(group)]
    vslice = [slice(g * MLA_V, (g + 1) * MLA_V) for g in range(group)]
    _softmax_reset(m_ref, acc_ref)
    ring = m_ref.shape[0]

    def logits(g, qi, j, slot):
        qt = jnp.concatenate([qt_ref[qi, hslice[g], :][:MLA_QH], jnp.zeros((LANES - MLA_QH, tq), BF16)], axis=0)
        s = _dot(k_ref[_tile(j, tq), hslice[g]], qt)
        s = s + mask_ref[jnp.minimum(qi - j, 1)]
        _store_logits(s, s_ref.at[slot, g], smax_ref.at[slot, g])

    def softmax(g, qi, j, slot):
        _softmax_update(s_ref.at[slot, g], smax_ref.at[slot, g], vt_ref[j, vslice[g], :],
                        m_ref.at[qi % ring, g], acc_ref.at[qi % ring, g])

    def finalize(qi):
        o = jnp.concatenate([_softmax_result(acc_ref.at[qi % ring, g]) for g in range(group)], axis=0)
        o_ref[_tile(qi, tq), :] = o.T.astype(o_ref.dtype)
        for g in range(group):
            _softmax_restart(m_ref.at[qi % ring, g])

    _causal_tile_pipeline(vt_ref.shape[0], group, logits, softmax, finalize)


def _mla_attention(qt, km, vt, mask, *, batch, seq, tq, group):
    t = km.shape[0]
    nq = seq // tq
    return pl.pallas_call(
        functools.partial(_mla_kernel, tq=tq, group=group),
        grid=(batch, MLA_HEADS // group),
        in_specs=[pl.BlockSpec((nq, group * LANES, tq), lambda b, h: (b, h, 0)),
                  pl.BlockSpec((seq, group * LANES), lambda b, h: (b, h)),
                  pl.BlockSpec((nq, group * MLA_V, tq), lambda b, h: (b, h, 0)),
                  _resident(mask.shape)],
        out_specs=pl.BlockSpec((seq, group * MLA_V), lambda b, h: (b, h)),
        out_shape=jax.ShapeDtypeStruct((t, MLA_V_W), BF16),
        scratch_shapes=_softmax_scratch(group, MLA_V, tq, tq, _state_ring(nq)),
        compiler_params=_cparams("parallel", "parallel"),
        name="mla_attention",
    )(qt, km, vt, mask)


def _merge_kernel(x_ref, a_ref, b_ref, c_ref, g_ref, wg_ref, wa_ref, wb_ref, wc_ref, wo_ref, pg_ref, o_ref):
    x = x_ref[...]
    d = x.shape[1]
    h = (_rms(x) * g_ref[...]).astype(BF16)
    merged = None
    for n, (br_ref, w_ref) in enumerate(((a_ref, wa_ref), (b_ref, wb_ref), (c_ref, wc_ref))):
        gate = jax.nn.sigmoid(_dot(h, wg_ref[:, n * d:(n + 1) * d]))
        part = gate * _dot(br_ref[...], w_ref[...])
        merged = part if merged is None else merged + part
    y = _dot(merged.astype(BF16), wo_ref[...])
    o_ref[...] = x + _rms(y) * pg_ref[...]


def _merge(x2, a, b, c, g, w_gates, wa, wb, wc, wo, layer, pg, *, tm):
    t, d = x2.shape
    rowblk = pl.BlockSpec((tm, d), lambda i: (i, 0))
    return pl.pallas_call(
        _merge_kernel,
        grid=(t // tm,),
        in_specs=[rowblk, rowblk, rowblk, rowblk, _resident((1, d))]
        + [_layer_slab(w, layer) for w in (w_gates, wa, wb, wc, wo)] + [_resident((1, d))],
        out_specs=rowblk,
        out_shape=jax.ShapeDtypeStruct((t, d), F32),
        compiler_params=_cparams("parallel"),
        name="merge",
    )(x2, a, b, c, g, w_gates, wa, wb, wc, wo, pg)


def _mlp_kernel(x_ref, g_ref, wu_ref, wd_ref, pg_ref, o_ref, *, tf):
    x = x_ref[...]
    h = (_rms(x) * g_ref[...]).astype(BF16)
    acc = None
    for c in range(wu_ref.shape[1] // tf):
        u = jnp.maximum(_dot(h, wu_ref[:, c * tf:(c + 1) * tf]), 0.0)
        part = _dot((u * u).astype(BF16), wd_ref[c * tf:(c + 1) * tf, :])
        acc = part if acc is None else acc + part
    o_ref[...] = x + _rms(acc) * pg_ref[...]


def _mlp(x2, g, wu, wd, layer, pg, *, tm, tf):
    t, d = x2.shape
    rowblk = pl.BlockSpec((tm, d), lambda i: (i, 0))
    return pl.pallas_call(
        functools.partial(_mlp_kernel, tf=tf),
        grid=(t // tm,),
        in_specs=[rowblk, _resident((1, d)), _layer_slab(wu, layer), _layer_slab(wd, layer), _resident((1, d))],
        out_specs=rowblk,
        out_shape=jax.ShapeDtypeStruct((t, d), F32),
        compiler_params=_cparams("parallel"),
        name="mlp",
    )(x2, g, wu, wd, pg)


def _rot_half_cols(w):
    half = w.shape[-1] // 2
    return jnp.concatenate([-w[..., half:], w[..., :half]], axis=-1)


def _cast_transposed_kernel(wt_ref, o_ref):
    o_ref[...] = wt_ref[...].T.astype(o_ref.dtype)


def _cast_transposed(wt, n, *, tn, skip=None):
    depth, _, d = wt.shape
    if skip is None:
        src = lambda c: c
    else:
        src = lambda c: jnp.where(c < skip[0] // tn, c, c + (skip[1] - skip[0]) // tn)
    return pl.pallas_call(
        _cast_transposed_kernel,
        grid=(depth, n // tn),
        in_specs=[pl.BlockSpec((None, tn, d), lambda l, c: (l, src(c), 0))],
        out_specs=pl.BlockSpec((None, d, tn), lambda l, c: (l, 0, c)),
        out_shape=jax.ShapeDtypeStruct((depth, d, n), BF16),
        compiler_params=_cparams("parallel", "parallel"),
        name="w_in_prep",
    )(wt)


def _prep_w_in(w_in, *, tn):
    wt = jnp.swapaxes(w_in, 1, 2)
    cq_at = W_IN_DV + DIFF_W
    kpe_at = cq_at + MLA_Q_LORA + MLA_KV_LORA
    gates_at = kpe_at + MLA_ROPE
    kpe = wt[:, kpe_at:gates_at]
    half = MLA_ROPE // 2
    zeros = jnp.zeros((wt.shape[0], MLA_NOPE, wt.shape[2]), wt.dtype)
    small = jnp.concatenate([wt[:, cq_at:kpe_at], zeros, kpe, -kpe[:, half:], kpe[:, :half]], axis=1)
    dqv = jnp.concatenate([wt[:, W_IN_DQ:W_IN_DK], wt[:, W_IN_DV:cq_at]], axis=1)
    return (_cast_transposed(wt, N_IN_MAIN, tn=tn, skip=(W_IN_DQ, W_IN_DK)),
            dqv.astype(BF16),
            _cast_transposed(wt[:, gates_at:], N_BRANCH * D_MODEL, tn=tn),
            _cast_transposed(small, N_IN_SMALL, tn=tn))


def _in_colscale():
    cs = np.ones((1, N_IN_MAIN), np.float32)
    cs[0, COL_RK:COL_RK + RET_QK_W] = RET_DK ** -0.5
    return jnp.asarray(cs)


def _prep_w_uq(w):
    w = w.reshape(MLA_Q_LORA, MLA_HEADS, MLA_QH)
    pe = w[..., MLA_NOPE:]
    w = jnp.concatenate([w, _rot_half_cols(pe)], axis=-1).reshape(MLA_Q_LORA, MLA_PAD_W)
    return w.T.astype(BF16)


def _prep_w_ukv(w):
    w = w.reshape(MLA_KV_LORA, MLA_HEADS, MLA_KVH)
    wk = jnp.concatenate([w[..., :MLA_NOPE], jnp.zeros_like(w[..., :MLA_NOPE])], axis=-1)
    return (wk.reshape(MLA_KV_LORA, MLA_PAD_W).astype(BF16),
            w[..., MLA_NOPE:].reshape(MLA_KV_LORA, MLA_V_W).T.astype(BF16))


def _ret_tables(seq):
    pos = np.arange(seq, dtype=np.float64)
    inv_freq = 1.0 / (10000.0 ** np.linspace(0.0, 1.0, RET_DK // 2))
    ang = pos[:, None] * inv_freq[None, :]
    cos, sin = np.cos(ang), np.sin(ang)
    cos_t = np.concatenate([cos, cos], axis=-1)
    sin_t = np.concatenate([-sin, sin], axis=-1)
    log_g = np.log1p(-np.exp2(-5.0 - np.arange(RET_HEADS, dtype=np.float64)))
    idx = np.arange(RET_CHUNK, dtype=np.float64)
    rel = idx[:, None] - idx[None, :]
    dmask = np.where(rel >= 0, np.exp(np.maximum(rel, 0.0)[None] * log_g[:, None, None]), 0.0)
    xi = np.exp((idx + 1.0)[None, :] * log_g[:, None])[:, :, None]
    zeta = np.exp((RET_CHUNK - 1.0 - idx)[None, :] * log_g[:, None])[:, :, None]
    g_chunk = np.exp(RET_CHUNK * log_g)[:, None, None]
    tabs = (cos_t, sin_t, dmask,
            np.broadcast_to(xi, (RET_HEADS, RET_CHUNK, RET_DK)),
            np.broadcast_to(zeta, (RET_HEADS, RET_CHUNK, RET_DK)),
            np.broadcast_to(g_chunk, (RET_HEADS, 1, RET_DV)))
    return tuple(jnp.asarray(t, F32) for t in tabs)


def _mla_tables(seq):
    pos = np.arange(seq, dtype=np.float64)
    inv_freq = 1.0 / (ROPE_THETA ** (np.arange(0, MLA_ROPE, 2, dtype=np.float64) / MLA_ROPE))
    ang = pos[:, None] * inv_freq[None, :]
    cos, sin = np.cos(ang), np.sin(ang)
    z_nope = np.zeros((seq, MLA_NOPE))
    z_rope = np.zeros((seq, MLA_ROPE))
    cos_k = np.concatenate([z_nope, cos, cos, z_rope], axis=-1)
    sin_k = np.concatenate([z_nope, sin, sin, z_rope], axis=-1)
    qscale = MLA_QH ** -0.5 * LOG2E
    cos_q = np.concatenate([np.ones((seq, MLA_NOPE)), cos, cos, z_rope], axis=-1) * qscale
    sin_q = sin_k * qscale
    return tuple(jnp.asarray(t, F32) for t in (cos_q.T, sin_q.T, cos_k, sin_k))


def _t5_bucket(n):
    max_exact = REL_BUCKETS // 2
    nf = jnp.maximum(n, 1).astype(F32)
    large = max_exact + (jnp.log(nf / max_exact) / math.log(REL_MAX_DIST / max_exact)
                         * (REL_BUCKETS - max_exact)).astype(jnp.int32)
    large = jnp.minimum(large, REL_BUCKETS - 1)
    return jnp.where(n < max_exact, n, large)


def _toeplitz(g, tq):
    h = g.shape[0]
    padded = jnp.concatenate([g, jnp.zeros((h, 1), g.dtype)], axis=1)
    skewed = jnp.tile(padded, (1, tq))[:, :tq * (2 * tq - 1)].reshape(h, tq, 2 * tq - 1)
    return skewed[:, :, tq - 1:]


def _bias_table(rel_bias, tq):
    assert tq >= REL_MAX_DIST
    bias = rel_bias[_t5_bucket(jnp.arange(2 * tq))].astype(F32).T
    bias = (bias - rel_bias[REL_BUCKETS - 1].astype(F32)[:, None]) * LOG2E
    masked = jnp.full((bias.shape[0], tq - 1), NEG_BIG, F32)
    diagonal = jnp.concatenate([masked, bias[:, :tq]], axis=1)
    behind = bias[:, 1:]
    zeros = jnp.zeros((bias.shape[0], tq, tq), F32)
    return jnp.stack([_toeplitz(diagonal, tq), _toeplitz(behind, tq), zeros], axis=1)


def _causal_table(tq):
    kk = np.arange(tq)[:, None]
    qq = np.arange(tq)[None, :]
    return jnp.asarray(np.stack([np.where(qq >= kk, 0.0, NEG_BIG), np.zeros((tq, tq))]), F32)


def _pick(t, want):
    return want if t % want == 0 else t


class _Tiles(NamedTuple):
    attn: int
    in_rows: int
    in_cols: int
    merge_rows: int
    mlp_rows: int
    mlp_cols: int
    diff_group: int
    mla_group: int


def _plan_tiles(batch, seq):
    t = batch * seq
    return _Tiles(attn=_pick(seq, 256), in_rows=_pick(seq, 512), in_cols=512, merge_rows=_pick(t, 512),
                  mlp_rows=_pick(t, 1024), mlp_cols=1024, diff_group=4, mla_group=8)


def kernel(x, rel_bias, pre_mix_g, w_in, w_ret_o, lambda_q1, lambda_k1, lambda_q2, lambda_k2, diff_subln_g, w_diff_o, mla_q_norm_g, w_mla_uq, mla_kv_norm_g, w_mla_ukv, w_mla_o, w_out, post_mix_g, pre_mlp_g, w_up, w_down, post_mlp_g):
    batch, seq, d = x.shape
    t = batch * seq
    depth = w_in.shape[0]
    tiles = _plan_tiles(batch, seq)
    tq = tiles.attn
    assert d == D_MODEL and w_in.shape[1:] == (D_MODEL, W_IN_DV + DIFF_W + N_IN_SMALL - LANES + MLA_ROPE
                                               + N_BRANCH * D_MODEL)
    assert seq % RET_CHUNK == 0 and seq % tq == 0
    assert DIFF_HEADS % tiles.diff_group == 0 and MLA_HEADS % tiles.mla_group == 0
    ret_tabs = _ret_tables(seq)
    mla_tabs = _mla_tables(seq)
    bias_tab = _bias_table(rel_bias, tq)
    causal_tab = _causal_table(tq)
    colscale = _in_colscale()
    w_main, w_dvt, w_gates, w_small = _prep_w_in(w_in, tn=tiles.in_cols)
    merge_w = tuple(w.astype(BF16) for w in (w_ret_o, w_diff_o, w_mla_o, w_out))
    mlp_w = (w_up.astype(BF16), w_down.astype(BF16))
    row = lambda v: v.reshape(1, -1).astype(F32)

    x2 = x.reshape(t, d)
    for l in range(depth):
        wk, wvt = _prep_w_ukv(w_mla_ukv[l])
        proj, dvt, qt, km, vt = _in_proj(x2, row(pre_mix_g[l]), w_main, w_dvt, w_small, l, colscale,
                                         row(mla_q_norm_g[l]), row(mla_kv_norm_g[l]),
                                         _prep_w_uq(w_mla_uq[l]), wk, wvt, mla_tabs,
                                         seq=seq, tm=tiles.in_rows, tn=tiles.in_cols, tk=tq)

        ret = _retention(proj, ret_tabs, batch=batch, seq=seq)

        lambda_init = 0.8 - 0.6 * math.exp(-0.3 * l)
        lam = (jnp.exp(jnp.sum(lambda_q1[l] * lambda_k1[l]).astype(F32))
               - jnp.exp(jnp.sum(lambda_q2[l] * lambda_k2[l]).astype(F32)) + lambda_init)
        subln_g = jnp.broadcast_to((diff_subln_g[l].astype(F32) * (1.0 - lambda_init))[:, None], (2 * DIFF_DH, tq))
        da = _diff_attention(proj, dvt, lam.reshape(1), bias_tab, subln_g,
                             batch=batch, seq=seq, tq=tq, group=tiles.diff_group)

        mo = _mla_attention(qt, km, vt, causal_tab, batch=batch, seq=seq, tq=tq, group=tiles.mla_group)

        x2 = _merge(x2, ret, da, mo, row(pre_mix_g[l]), w_gates, *merge_w, l, row(post_mix_g[l]),
                    tm=tiles.merge_rows)

        x2 = _mlp(x2, row(pre_mlp_g[l]), *mlp_w, l, row(post_mlp_g[l]), tm=tiles.mlp_rows, tf=tiles.mlp_cols)
    return x2.reshape(batch, seq, d)
```

```python
import functools
import math
from typing import NamedTuple

import jax
import jax.numpy as jnp
import numpy as np
from jax import lax
from jax.experimental import pallas as pl
from jax.experimental.pallas import tpu as pltpu

F32 = jnp.float32
BF16 = jnp.bfloat16

D_MODEL = 1024
EPS = 1e-6
RET_HEADS = 4
RET_DK = 128
RET_DV = 256
RET_CHUNK = 128
DIFF_HEADS = 8
DIFF_DH = 64
MLA_HEADS = 16
MLA_Q_LORA = 256
MLA_KV_LORA = 128
MLA_NOPE = 64
MLA_ROPE = 32
MLA_V = 64
ROPE_THETA = 10000.0
REL_BUCKETS = 32
REL_MAX_DIST = 128
N_BRANCH = 3

LANES = 128
V7X_VMEM_BYTES = 64 * 1024 * 1024
VMEM_LIMIT = V7X_VMEM_BYTES * 7 // 8
LOG2E = math.log2(math.e)
NEG_BIG = -1e30

RET_QK_W = RET_HEADS * RET_DK
RET_V_W = RET_HEADS * RET_DV
DIFF_W = DIFF_HEADS * 2 * DIFF_DH
MLA_QH = MLA_NOPE + MLA_ROPE
MLA_KVH = MLA_NOPE + MLA_V
MLA_V_W = MLA_HEADS * MLA_V
MLA_PAD_W = MLA_HEADS * LANES

COL_RQ = 0
COL_RK = COL_RQ + RET_QK_W
COL_RV = COL_RK + RET_QK_W
COL_RG = COL_RV + RET_V_W
COL_DK = COL_RG + RET_V_W
N_IN_MAIN = COL_DK + DIFF_W
W_IN_DQ = COL_DK
W_IN_DK = W_IN_DQ + DIFF_W
W_IN_DV = W_IN_DK + DIFF_W
DIFF_Q_SCALE = DIFF_DH ** -0.5 * LOG2E
N_IN_SMALL = MLA_Q_LORA + MLA_KV_LORA + LANES


def _cparams(*sem):
    return pltpu.CompilerParams(dimension_semantics=sem, vmem_limit_bytes=VMEM_LIMIT)


def _rms(xf):
    return xf * lax.rsqrt(jnp.mean(xf * xf, axis=-1, keepdims=True) + EPS)


def _dot(a, b):
    return jnp.dot(a, b, preferred_element_type=F32)


def _dot_nt(a, b):
    return lax.dot_general(a, b, (((1,), (1,)), ((), ())), preferred_element_type=F32)


HEADS_PER_UP = 4


def _in_proj_kernel(x_ref, g_ref, wm_ref, wdvt_ref, ws_ref, cs_ref, gq_ref, gkv_ref, wqt_ref, wk_ref, wvt_ref,
                    cqt_tab, sqt_tab, ck_tab, sk_tab, o_ref, dvt_ref, qt_ref, k_ref, vt_ref, *, tn):
    h = (_rms(x_ref[...]) * g_ref[...]).astype(BF16)

    small = _dot(h, ws_ref[...])
    hq = (_rms(small[:, :MLA_Q_LORA]) * gq_ref[...]).astype(BF16)
    hkv = (_rms(small[:, MLA_Q_LORA:MLA_Q_LORA + MLA_KV_LORA]) * gkv_ref[...]).astype(BF16)
    kpe = small[:, MLA_Q_LORA + MLA_KV_LORA:]
    kpe = kpe * ck_tab[...] + pltpu.roll(kpe, LANES - MLA_ROPE, 1) * sk_tab[...]
    tk = vt_ref.shape[-1]
    n_tiles = vt_ref.shape[0]
    for c in range(n_tiles):
        vt_ref[c] = _dot_nt(wvt_ref[...], hkv[c * tk:(c + 1) * tk]).astype(vt_ref.dtype)
    cqt = cqt_tab[...]
    sqt = sqt_tab[...]
    up = HEADS_PER_UP * LANES
    for hg in range(MLA_HEADS // HEADS_PER_UP):
        q_t = _dot_nt(wqt_ref[hg * up:(hg + 1) * up, :], hq)
        ka = _dot(hkv, wk_ref[:, hg * up:(hg + 1) * up])
        for hh in range(HEADS_PER_UP):
            hs = slice(hh * LANES, (hh + 1) * LANES)
            out = slice(hg * up + hh * LANES, hg * up + (hh + 1) * LANES)
            blk = q_t[hs]
            partner = jnp.concatenate([blk[MLA_ROPE:], blk[:MLA_ROPE]], axis=0)
            blk = blk * cqt + partner * sqt
            for c in range(n_tiles):
                qt_ref[c, out, :] = blk[:, c * tk:(c + 1) * tk].astype(qt_ref.dtype)
            k_ref[:, out] = (ka[:, hs] + kpe).astype(k_ref.dtype)

    for c in range(wm_ref.shape[1] // tn):
        cols = slice(c * tn, (c + 1) * tn)
        o_ref[:, cols] = (_dot(h, wm_ref[:, cols]) * cs_ref[:, cols]).astype(o_ref.dtype)
    for r in range(wdvt_ref.shape[0] // tn):
        rows = slice(r * tn, (r + 1) * tn)
        dqv_t = _dot_nt(wdvt_ref[rows, :], h)
        if r * tn < DIFF_W:
            dqv_t = dqv_t * DIFF_Q_SCALE
        for c in range(n_tiles):
            dvt_ref[c, rows, :] = dqv_t[:, c * tk:(c + 1) * tk].astype(dvt_ref.dtype)


def _resident(shape):
    return pl.BlockSpec(shape, lambda *_: (0,) * len(shape), pipeline_mode=pl.Buffered(1))


def _layer_slab(w, layer):
    return pl.BlockSpec((None,) + w.shape[1:], lambda *_: (layer, 0, 0), pipeline_mode=pl.Buffered(1))


def _in_proj(x2, g, w_main, w_dvt, w_small, layer, colscale, gq, gkv, wqt, wk, wvt, tabs, *, seq, tm, tn, tk):
    t, d = x2.shape
    n = w_main.shape[2]
    assert n % tn == 0 and w_dvt.shape[1] % tn == 0 and tm % tk == 0 and seq % tm == 0
    ns = seq // tm
    nc = tm // tk
    cqt, sqt, ck, sk = tabs
    tab_t = pl.BlockSpec((LANES, tm), lambda i: (0, i % ns))
    tab = pl.BlockSpec((tm, LANES), lambda i: (i % ns, 0))
    return pl.pallas_call(
        functools.partial(_in_proj_kernel, tn=tn),
        grid=(t // tm,),
        in_specs=[pl.BlockSpec((tm, d), lambda i: (i, 0)), _resident((1, d)),
                  _layer_slab(w_main, layer), _layer_slab(w_dvt, layer), _layer_slab(w_small, layer),
                  _resident((1, n)), _resident(gq.shape), _resident(gkv.shape),
                  _resident(wqt.shape), _resident(wk.shape), _resident(wvt.shape),
                  tab_t, tab_t, tab, tab],
        out_specs=[pl.BlockSpec((tm, n), lambda i: (i, 0)),
                   pl.BlockSpec((nc, w_dvt.shape[1], tk), lambda i: (i, 0, 0)),
                   pl.BlockSpec((nc, MLA_PAD_W, tk), lambda i: (i, 0, 0)),
                   pl.BlockSpec((tm, MLA_PAD_W), lambda i: (i, 0)),
                   pl.BlockSpec((nc, MLA_V_W, tk), lambda i: (i, 0, 0))],
        out_shape=[jax.ShapeDtypeStruct((t, n), BF16),
                   jax.ShapeDtypeStruct((t // tk, w_dvt.shape[1], tk), BF16),
                   jax.ShapeDtypeStruct((t // tk, MLA_PAD_W, tk), BF16),
                   jax.ShapeDtypeStruct((t, MLA_PAD_W), BF16),
                   jax.ShapeDtypeStruct((t // tk, MLA_V_W, tk), BF16)],
        compiler_params=_cparams("parallel"),
        name="in_proj",
    )(x2, g, w_main, w_dvt, w_small, colscale, gq, gkv, wqt, wk, wvt, cqt, sqt, ck, sk)


def _ret_kernel(q_ref, k_ref, v_ref, rg_ref, cos_ref, sin_ref, dm_ref, xi_ref, zt_ref, gc_ref,
                o_ref, st_ref):
    st_ref[...] = jnp.zeros_like(st_ref)

    def chunk(n, carry):
        rows = _tile(n, RET_CHUNK)
        cos = cos_ref[rows, :]
        sin = sin_ref[rows, :]
        for h in range(RET_HEADS):
            qs = slice(h * RET_DK, (h + 1) * RET_DK)
            vs = slice(h * RET_DV, (h + 1) * RET_DV)
            q = q_ref[rows, qs].astype(F32)
            k = k_ref[rows, qs].astype(F32)
            qr = q * cos + pltpu.roll(q, RET_DK // 2, 1) * sin
            kr = k * cos + pltpu.roll(k, RET_DK // 2, 1) * sin
            v = v_ref[rows, vs]
            st = st_ref[h]
            inner = _dot_nt(qr.astype(BF16), kr.astype(BF16)) * dm_ref[h]
            out = _dot(inner.astype(BF16), v) + _dot((qr * xi_ref[h]).astype(BF16), st.astype(BF16))
            kz_t = (kr * zt_ref[h]).T.astype(BF16)
            st_ref[h] = gc_ref[h] * st + _dot(kz_t, v)
            rg = rg_ref[rows, vs].astype(F32)
            o_ref[rows, vs] = (_rms(out) * (rg * jax.nn.sigmoid(rg))).astype(o_ref.dtype)
        return carry

    lax.fori_loop(0, q_ref.shape[0] // RET_CHUNK, chunk, 0, unroll=4)


def _retention(proj, tabs, *, batch, seq):
    t = proj.shape[0]
    c = RET_CHUNK
    cos, sin, dmask, xi, zeta, gch = tabs
    return pl.pallas_call(
        _ret_kernel,
        grid=(batch,),
        in_specs=[pl.BlockSpec((seq, RET_QK_W), lambda b: (b, COL_RQ // RET_QK_W)),
                  pl.BlockSpec((seq, RET_QK_W), lambda b: (b, COL_RK // RET_QK_W)),
                  pl.BlockSpec((seq, RET_V_W), lambda b: (b, COL_RV // RET_V_W)),
                  pl.BlockSpec((seq, RET_V_W), lambda b: (b, COL_RG // RET_V_W)),
                  _resident((seq, RET_DK)), _resident((seq, RET_DK)),
                  _resident((RET_HEADS, c, c)), _resident((RET_HEADS, c, RET_DK)),
                  _resident((RET_HEADS, c, RET_DK)), _resident((RET_HEADS, 1, RET_DV))],
        out_specs=pl.BlockSpec((seq, RET_V_W), lambda b: (b, 0)),
        out_shape=jax.ShapeDtypeStruct((t, RET_V_W), BF16),
        scratch_shapes=[pltpu.VMEM((RET_HEADS, RET_DK, RET_DV), F32)],
        compiler_params=_cparams("parallel"),
        name="retention",
    )(proj, proj, proj, proj, cos, sin, dmask, xi, zeta, gch)


SUM_ROWS = 16


def _state_ring(nq, block):
    pairs = [(qi, j) for qi in range(nq) for j in range(qi + 1)]
    rows = max(len({qi for qi, _ in pairs[i:i + block]}) for i in range(0, len(pairs), block))
    return 1 << (rows - 1).bit_length()


def _store_logits(s, s_ref, smax_ref):
    s_ref[...] = s
    smax_ref[...] = jnp.max(s, axis=0, keepdims=True)


def _softmax_update(s_ref, smax_ref, vt, m_ref, acc_ref):
    m = m_ref[...]
    m_new = jnp.maximum(m, smax_ref[...])
    alpha = jnp.exp2(m - m_new)
    p = jnp.exp2(s_ref[...] - m_new).astype(BF16)
    m_ref[...] = m_new
    vt_ones = jnp.concatenate([vt, jnp.ones((SUM_ROWS, vt.shape[1]), BF16)], axis=0)
    acc_ref[...] = alpha * acc_ref[...] + _dot(vt_ones, p)


def _softmax_result(acc_ref):
    dv = acc_ref.shape[0] - SUM_ROWS
    return acc_ref[:dv, :] / acc_ref[dv:dv + 1, :]


def _softmax_reset(m_ref, acc_ref):
    m_ref[...] = jnp.full(m_ref.shape, NEG_BIG, F32)
    acc_ref[...] = jnp.zeros(acc_ref.shape, F32)


def _softmax_restart(m_ref):
    m_ref[...] = jnp.full(m_ref.shape, NEG_BIG, F32)


def _softmax_scratch(group, dv, tk, n, ring):
    return [pltpu.VMEM((2, group, tk, n), F32), pltpu.VMEM((2, group, 1, n), F32),
            pltpu.VMEM((ring, group, 1, n), F32), pltpu.VMEM((ring, group, dv + SUM_ROWS, n), F32)]


def _tile(i, size):
    return pl.ds(pl.multiple_of(i * size, size), size)


def _causal_tile_pipeline(nq, group, block, logits, softmax, finalize):
    def following(pair):
        qi, j = pair
        last = j == qi
        return jnp.where(last, qi + 1, qi), jnp.where(last, 0, j + 1)

    def run(first, count, prefetch_next):
        pairs = [first]
        for _ in range(count):
            pairs.append(following(pairs[-1]))
        for i in range(count):
            if i + 1 < count or prefetch_next:
                qn, jn = pairs[i + 1]
                qn = jnp.minimum(qn, nq - 1)
                for g in range(group):
                    logits(g, qn, jn, (i + 1) % 2)
            for g in range(group):
                softmax(g, *pairs[i], i % 2)
        for qi, j in pairs[:count]:
            @pl.when(j == qi)
            def _(qi=qi):
                finalize(qi)
        return pairs[count]

    assert block % 2 == 0
    n_pairs = nq * (nq + 1) // 2
    zero = jnp.int32(0)
    for g in range(group):
        logits(g, zero, zero, 0)
    rest = lax.fori_loop(0, n_pairs // block, lambda _, c: run(c, block, True), (zero, zero))
    if n_pairs % block:
        run(rest, n_pairs % block, False)


def _diff_kernel(lam_ref, qt_ref, k_ref, vt_ref, tab_ref, gt_ref, o_ref, s_ref, smax_ref, m_ref, acc_ref,
                 *, tq, group, block):
    w = 2 * DIFF_DH
    nq = vt_ref.shape[0]
    hslice = [slice(g * w, (g + 1) * w) for g in range(group)]
    _softmax_reset(m_ref, acc_ref)
    ring = m_ref.shape[0]
    zeros = jnp.zeros((DIFF_DH, tq), BF16)

    def logits(g, qi, j, slot):
        qt = qt_ref[qi, hslice[g], :]
        q2 = jnp.concatenate([jnp.concatenate([qt[:DIFF_DH], zeros], axis=0),
                              jnp.concatenate([zeros, qt[DIFF_DH:]], axis=0)], axis=1)
        s = _dot(k_ref[_tile(j, tq), hslice[g]], q2)
        bias = tab_ref[g, jnp.minimum(qi - j, tab_ref.shape[1] - 1)]
        s = s + jnp.concatenate([bias, bias], axis=1)
        _store_logits(s, s_ref.at[slot, g], smax_ref.at[slot, g])

    def softmax(g, qi, j, slot):
        _softmax_update(s_ref.at[slot, g], smax_ref.at[slot, g], vt_ref[j, hslice[g], :],
                        m_ref.at[qi % ring, g], acc_ref.at[qi % ring, g])

    def finalize(qi):
        for g in range(group):
            o = _softmax_result(acc_ref.at[qi % ring, g])
            a = o[:, :tq] - lam_ref[0] * o[:, tq:]
            a = a * lax.rsqrt(jnp.mean(a * a, axis=0, keepdims=True) + EPS) * gt_ref[...]
            o_ref[_tile(qi, tq), hslice[g]] = a.T.astype(o_ref.dtype)
            _softmax_restart(m_ref.at[qi % ring, g])

    _causal_tile_pipeline(nq, group, block, logits, softmax, finalize)


def _diff_attention(proj, qvt, lam, tab, g_t, *, batch, seq, tq, group, block):
    t = proj.shape[0]
    nq = seq // tq
    w = group * 2 * DIFF_DH
    return pl.pallas_call(
        functools.partial(_diff_kernel, tq=tq, group=group, block=block),
        grid=(batch, DIFF_HEADS // group),
        in_specs=[pl.BlockSpec(memory_space=pltpu.SMEM),
                  pl.BlockSpec((nq, w, tq), lambda b, h: (b, h, 0)),
                  pl.BlockSpec((seq, w), lambda b, h: (b, COL_DK // w + h)),
                  pl.BlockSpec((nq, w, tq), lambda b, h: (b, DIFF_W // w + h, 0)),
                  pl.BlockSpec((group,) + tab.shape[1:], lambda b, h: (h, 0, 0, 0)),
                  _resident(g_t.shape)],
        out_specs=pl.BlockSpec((seq, w), lambda b, h: (b, h)),
        out_shape=jax.ShapeDtypeStruct((t, DIFF_W), BF16),
        scratch_shapes=_softmax_scratch(group, 2 * DIFF_DH, tq, 2 * tq, _state_ring(nq, block)),
        compiler_params=_cparams("parallel", "parallel"),
        name="diff_attention",
    )(lam, qvt, proj, qvt, tab, g_t)


def _mla_kernel(qt_ref, k_ref, vt_ref, mask_ref, o_ref, s_ref, smax_ref, m_ref, acc_ref, *, tq, group, block):
    hslice = [slice(g * LANES, (g + 1) * LANES) for g in range(group)]
    vslice = [slice(g * MLA_V, (g + 1) * MLA_V) for g in range(group)]
    _softmax_reset(m_ref, acc_ref)
    ring = m_ref.shape[0]

    def logits(g, qi, j, slot):
        qt = jnp.concatenate([qt_ref[qi, hslice[g], :][:MLA_QH], jnp.zeros((LANES - MLA_QH, tq), BF16)], axis=0)
        s = _dot(k_ref[_tile(j, tq), hslice[g]], qt)
        s = s + mask_ref[jnp.minimum(qi - j, 1)]
        _store_logits(s, s_ref.at[slot, g], smax_ref.at[slot, g])

    def softmax(g, qi, j, slot):
        _softmax_update(s_ref.at[slot, g], smax_ref.at[slot, g], vt_ref[j, vslice[g], :],
                        m_ref.at[qi % ring, g], acc_ref.at[qi % ring, g])

    def finalize(qi):
        o = jnp.concatenate([_softmax_result(acc_ref.at[qi % ring, g]) for g in range(group)], axis=0)
        o_ref[_tile(qi, tq), :] = o.T.astype(o_ref.dtype)
        for g in range(group):
            _softmax_restart(m_ref.at[qi % ring, g])

    _causal_tile_pipeline(vt_ref.shape[0], group, block, logits, softmax, finalize)


def _mla_attention(qt, km, vt, mask, *, batch, seq, tq, group, block):
    t = km.shape[0]
    nq = seq // tq
    return pl.pallas_call(
        functools.partial(_mla_kernel, tq=tq, group=group, block=block),
        grid=(batch, MLA_HEADS // group),
        in_specs=[pl.BlockSpec((nq, group * LANES, tq), lambda b, h: (b, h, 0)),
                  pl.BlockSpec((seq, group * LANES), lambda b, h: (b, h)),
                  pl.BlockSpec((nq, group * MLA_V, tq), lambda b, h: (b, h, 0)),
                  _resident(mask.shape)],
        out_specs=pl.BlockSpec((seq, group * MLA_V), lambda b, h: (b, h)),
        out_shape=jax.ShapeDtypeStruct((t, MLA_V_W), BF16),
        scratch_shapes=_softmax_scratch(group, MLA_V, tq, tq, _state_ring(nq, block)),
        compiler_params=_cparams("parallel", "parallel"),
        name="mla_attention",
    )(qt, km, vt, mask)


def _merge_kernel(x_ref, a_ref, b_ref, c_ref, g_ref, wg_ref, wa_ref, wb_ref, wc_ref, wo_ref, pg_ref, o_ref):
    x = x_ref[...]
    d = x.shape[1]
    h = (_rms(x) * g_ref[...]).astype(BF16)
    merged = None
    for n, (br_ref, w_ref) in enumerate(((a_ref, wa_ref), (b_ref, wb_ref), (c_ref, wc_ref))):
        gate = jax.nn.sigmoid(_dot(h, wg_ref[:, n * d:(n + 1) * d]))
        part = gate * _dot(br_ref[...], w_ref[...])
        merged = part if merged is None else merged + part
    y = _dot(merged.astype(BF16), wo_ref[...])
    o_ref[...] = x + _rms(y) * pg_ref[...]


def _merge(x2, a, b, c, g, w_gates, wa, wb, wc, wo, layer, pg, *, tm):
    t, d = x2.shape
    rowblk = pl.BlockSpec((tm, d), lambda i: (i, 0))
    return pl.pallas_call(
        _merge_kernel,
        grid=(t // tm,),
        in_specs=[rowblk, rowblk, rowblk, rowblk, _resident((1, d))]
        + [_layer_slab(w, layer) for w in (w_gates, wa, wb, wc, wo)] + [_resident((1, d))],
        out_specs=rowblk,
        out_shape=jax.ShapeDtypeStruct((t, d), F32),
        compiler_params=_cparams("parallel"),
        name="merge",
    )(x2, a, b, c, g, w_gates, wa, wb, wc, wo, pg)


def _mlp_kernel(x_ref, g_ref, wu_ref, wd_ref, pg_ref, o_ref, *, tf):
    x = x_ref[...]
    h = (_rms(x) * g_ref[...]).astype(BF16)
    acc = None
    for c in range(wu_ref.shape[1] // tf):
        u = jnp.maximum(_dot(h, wu_ref[:, c * tf:(c + 1) * tf]), 0.0)
        part = _dot((u * u).astype(BF16), wd_ref[c * tf:(c + 1) * tf, :])
        acc = part if acc is None else acc + part
    o_ref[...] = x + _rms(acc) * pg_ref[...]


def _mlp(x2, g, wu, wd, layer, pg, *, tm, tf):
    t, d = x2.shape
    rowblk = pl.BlockSpec((tm, d), lambda i: (i, 0))
    return pl.pallas_call(
        functools.partial(_mlp_kernel, tf=tf),
        grid=(t // tm,),
        in_specs=[rowblk, _resident((1, d)), _layer_slab(wu, layer), _layer_slab(wd, layer), _resident((1, d))],
        out_specs=rowblk,
        out_shape=jax.ShapeDtypeStruct((t, d), F32),
        compiler_params=_cparams("parallel"),
        name="mlp",
    )(x2, g, wu, wd, pg)


def _rot_half_cols(w):
    half = w.shape[-1] // 2
    return jnp.concatenate([-w[..., half:], w[..., :half]], axis=-1)


def _cast_transposed_kernel(wt_ref, o_ref):
    o_ref[...] = wt_ref[...].T.astype(o_ref.dtype)


def _cast_transposed(wt, n, *, tn, skip=None):
    depth, _, d = wt.shape
    if skip is None:
        src = lambda c: c
    else:
        src = lambda c: jnp.where(c < skip[0] // tn, c, c + (skip[1] - skip[0]) // tn)
    return pl.pallas_call(
        _cast_transposed_kernel,
        grid=(depth, n // tn),
        in_specs=[pl.BlockSpec((None, tn, d), lambda l, c: (l, src(c), 0))],
        out_specs=pl.BlockSpec((None, d, tn), lambda l, c: (l, 0, c)),
        out_shape=jax.ShapeDtypeStruct((depth, d, n), BF16),
        compiler_params=_cparams("parallel", "parallel"),
        name="w_in_prep",
    )(wt)


def _prep_w_in(w_in, *, tn):
    wt = jnp.swapaxes(w_in, 1, 2)
    cq_at = W_IN_DV + DIFF_W
    kpe_at = cq_at + MLA_Q_LORA + MLA_KV_LORA
    gates_at = kpe_at + MLA_ROPE
    kpe = wt[:, kpe_at:gates_at]
    half = MLA_ROPE // 2
    zeros = jnp.zeros((wt.shape[0], MLA_NOPE, wt.shape[2]), wt.dtype)
    small = jnp.concatenate([wt[:, cq_at:kpe_at], zeros, kpe, -kpe[:, half:], kpe[:, :half]], axis=1)
    dqv = jnp.concatenate([wt[:, W_IN_DQ:W_IN_DK], wt[:, W_IN_DV:cq_at]], axis=1)
    return (_cast_transposed(wt, N_IN_MAIN, tn=tn, skip=(W_IN_DQ, W_IN_DK)),
            dqv.astype(BF16),
            _cast_transposed(wt[:, gates_at:], N_BRANCH * D_MODEL, tn=tn),
            _cast_transposed(small, N_IN_SMALL, tn=tn))


def _in_colscale():
    cs = np.ones((1, N_IN_MAIN), np.float32)
    cs[0, COL_RK:COL_RK + RET_QK_W] = RET_DK ** -0.5
    return jnp.asarray(cs)


def _prep_w_uq(w):
    w = w.reshape(MLA_Q_LORA, MLA_HEADS, MLA_QH)
    pe = w[..., MLA_NOPE:]
    w = jnp.concatenate([w, _rot_half_cols(pe)], axis=-1).reshape(MLA_Q_LORA, MLA_PAD_W)
    return w.T.astype(BF16)


def _prep_w_ukv(w):
    w = w.reshape(MLA_KV_LORA, MLA_HEADS, MLA_KVH)
    wk = jnp.concatenate([w[..., :MLA_NOPE], jnp.zeros_like(w[..., :MLA_NOPE])], axis=-1)
    return (wk.reshape(MLA_KV_LORA, MLA_PAD_W).astype(BF16),
            w[..., MLA_NOPE:].reshape(MLA_KV_LORA, MLA_V_W).T.astype(BF16))


def _ret_tables(seq):
    pos = np.arange(seq, dtype=np.float64)
    inv_freq = 1.0 / (10000.0 ** np.linspace(0.0, 1.0, RET_DK // 2))
    ang = pos[:, None] * inv_freq[None, :]
    cos, sin = np.cos(ang), np.sin(ang)
    cos_t = np.concatenate([cos, cos], axis=-1)
    sin_t = np.concatenate([-sin, sin], axis=-1)
    log_g = np.log1p(-np.exp2(-5.0 - np.arange(RET_HEADS, dtype=np.float64)))
    idx = np.arange(RET_CHUNK, dtype=np.float64)
    rel = idx[:, None] - idx[None, :]
    dmask = np.where(rel >= 0, np.exp(np.maximum(rel, 0.0)[None] * log_g[:, None, None]), 0.0)
    xi = np.exp((idx + 1.0)[None, :] * log_g[:, None])[:, :, None]
    zeta = np.exp((RET_CHUNK - 1.0 - idx)[None, :] * log_g[:, None])[:, :, None]
    g_chunk = np.exp(RET_CHUNK * log_g)[:, None, None]
    tabs = (cos_t, sin_t, dmask,
            np.broadcast_to(xi, (RET_HEADS, RET_CHUNK, RET_DK)),
            np.broadcast_to(zeta, (RET_HEADS, RET_CHUNK, RET_DK)),
            np.broadcast_to(g_chunk, (RET_HEADS, 1, RET_DV)))
    return tuple(jnp.asarray(t, F32) for t in tabs)


def _mla_tables(seq):
    pos = np.arange(seq, dtype=np.float64)
    inv_freq = 1.0 / (ROPE_THETA ** (np.arange(0, MLA_ROPE, 2, dtype=np.float64) / MLA_ROPE))
    ang = pos[:, None] * inv_freq[None, :]
    cos, sin = np.cos(ang), np.sin(ang)
    z_nope = np.zeros((seq, MLA_NOPE))
    z_rope = np.zeros((seq, MLA_ROPE))
    cos_k = np.concatenate([z_nope, cos, cos, z_rope], axis=-1)
    sin_k = np.concatenate([z_nope, sin, sin, z_rope], axis=-1)
    qscale = MLA_QH ** -0.5 * LOG2E
    cos_q = np.concatenate([np.ones((seq, MLA_NOPE)), cos, cos, z_rope], axis=-1) * qscale
    sin_q = sin_k * qscale
    return tuple(jnp.asarray(t, F32) for t in (cos_q.T, sin_q.T, cos_k, sin_k))


def _t5_bucket(n):
    max_exact = REL_BUCKETS // 2
    nf = jnp.maximum(n, 1).astype(F32)
    large = max_exact + (jnp.log(nf / max_exact) / math.log(REL_MAX_DIST / max_exact)
                         * (REL_BUCKETS - max_exact)).astype(jnp.int32)
    large = jnp.minimum(large, REL_BUCKETS - 1)
    return jnp.where(n < max_exact, n, large)


def _toeplitz(g, tq):
    h = g.shape[0]
    padded = jnp.concatenate([g, jnp.zeros((h, 1), g.dtype)], axis=1)
    skewed = jnp.tile(padded, (1, tq))[:, :tq * (2 * tq - 1)].reshape(h, tq, 2 * tq - 1)
    return skewed[:, :, tq - 1:]


def _bias_table(rel_bias, tq):
    assert tq >= REL_MAX_DIST
    bias = rel_bias[_t5_bucket(jnp.arange(2 * tq))].astype(F32).T
    bias = (bias - rel_bias[REL_BUCKETS - 1].astype(F32)[:, None]) * LOG2E
    masked = jnp.full((bias.shape[0], tq - 1), NEG_BIG, F32)
    diagonal = jnp.concatenate([masked, bias[:, :tq]], axis=1)
    behind = bias[:, 1:]
    zeros = jnp.zeros((bias.shape[0], tq, tq), F32)
    return jnp.stack([_toeplitz(diagonal, tq), _toeplitz(behind, tq), zeros], axis=1)


def _causal_table(tq):
    kk = np.arange(tq)[:, None]
    qq = np.arange(tq)[None, :]
    return jnp.asarray(np.stack([np.where(qq >= kk, 0.0, NEG_BIG), np.zeros((tq, tq))]), F32)


def _pick(t, want):
    return want if t % want == 0 else t


class _Tiles(NamedTuple):
    attn: int
    in_rows: int
    in_cols: int
    merge_rows: int
    mlp_rows: int
    mlp_cols: int
    diff_group: int
    mla_group: int
    diff_block: int
    mla_block: int


def _plan_tiles(batch, seq):
    t = batch * seq
    return _Tiles(attn=_pick(seq, 256), in_rows=_pick(seq, 512), in_cols=512, merge_rows=_pick(t, 512),
                  mlp_rows=_pick(t, 1024), mlp_cols=1024, diff_group=4, mla_group=8, diff_block=4, mla_block=8)


def kernel(x, rel_bias, pre_mix_g, w_in, w_ret_o, lambda_q1, lambda_k1, lambda_q2, lambda_k2, diff_subln_g, w_diff_o, mla_q_norm_g, w_mla_uq, mla_kv_norm_g, w_mla_ukv, w_mla_o, w_out, post_mix_g, pre_mlp_g, w_up, w_down, post_mlp_g):
    batch, seq, d = x.shape
    t = batch * seq
    depth = w_in.shape[0]
    tiles = _plan_tiles(batch, seq)
    tq = tiles.attn
    assert d == D_MODEL and w_in.shape[1:] == (D_MODEL, W_IN_DV + DIFF_W + N_IN_SMALL - LANES + MLA_ROPE
                                               + N_BRANCH * D_MODEL)
    assert seq % RET_CHUNK == 0 and seq % tq == 0
    assert DIFF_HEADS % tiles.diff_group == 0 and MLA_HEADS % tiles.mla_group == 0
    ret_tabs = _ret_tables(seq)
    mla_tabs = _mla_tables(seq)
    bias_tab = _bias_table(rel_bias, tq)
    causal_tab = _causal_table(tq)
    colscale = _in_colscale()
    w_main, w_dvt, w_gates, w_small = _prep_w_in(w_in, tn=tiles.in_cols)
    merge_w = tuple(w.astype(BF16) for w in (w_ret_o, w_diff_o, w_mla_o, w_out))
    mlp_w = (w_up.astype(BF16), w_down.astype(BF16))
    row = lambda v: v.reshape(1, -1).astype(F32)

    x2 = x.reshape(t, d)
    for l in range(depth):
        wk, wvt = _prep_w_ukv(w_mla_ukv[l])
        proj, dvt, qt, km, vt = _in_proj(x2, row(pre_mix_g[l]), w_main, w_dvt, w_small, l, colscale,
                                         row(mla_q_norm_g[l]), row(mla_kv_norm_g[l]),
                                         _prep_w_uq(w_mla_uq[l]), wk, wvt, mla_tabs,
                                         seq=seq, tm=tiles.in_rows, tn=tiles.in_cols, tk=tq)

        ret = _retention(proj, ret_tabs, batch=batch, seq=seq)

        lambda_init = 0.8 - 0.6 * math.exp(-0.3 * l)
        lam = (jnp.exp(jnp.sum(lambda_q1[l] * lambda_k1[l]).astype(F32))
               - jnp.exp(jnp.sum(lambda_q2[l] * lambda_k2[l]).astype(F32)) + lambda_init)
        subln_g = jnp.broadcast_to((diff_subln_g[l].astype(F32) * (1.0 - lambda_init))[:, None], (2 * DIFF_DH, tq))
        da = _diff_attention(proj, dvt, lam.reshape(1), bias_tab, subln_g,
                             batch=batch, seq=seq, tq=tq, group=tiles.diff_group, block=tiles.diff_block)

        mo = _mla_attention(qt, km, vt, causal_tab, batch=batch, seq=seq, tq=tq, group=tiles.mla_group,
                            block=tiles.mla_block)

        x2 = _merge(x2, ret, da, mo, row(pre_mix_g[l]), w_gates, *merge_w, l, row(post_mix_g[l]),
                    tm=tiles.merge_rows)

        x2 = _mlp(x2, row(pre_mlp_g[l]), *mlp_w, l, row(post_mlp_g[l]), tm=tiles.mlp_rows, tf=tiles.mlp_cols)
    return x2.reshape(batch, seq, d)
```
